```python
import math
import jax, jax.numpy as jnp
from jax import lax
import numpy as np

D_MODEL = 1024
BATCH = 16
SEQ = 256
DEPTH = 4
DEC_BATCH = 8
DEC_SEQ = 2048
PAST_LEN = 512

GRID_W = 64
EPS = 1e-6
CONV_W = 4
CHUNK = 128
Q_BLOCK = 128
N_BRANCH = 3
BRANCH_W = D_MODEL
SSD_HEADS = 16
SSD_HEAD_DIM = BRANCH_W // SSD_HEADS
SSD_INNER = SSD_HEADS * SSD_HEAD_DIM
SSD_GROUPS = 4
SSD_STATE = 64
ATT_HEADS = 16
ATT_HEAD_DIM = BRANCH_W // ATT_HEADS
ATT_KV_HEADS = 4
ROPE_THETA = 10000.0
ML_HEADS = 8
ML_HEAD_DIM = BRANCH_W // ML_HEADS
ML_INNER = ML_HEADS * ML_HEAD_DIM
FFN_HIDDEN = ((8 * D_MODEL // 3 + 255) // 256) * 256
IN_SIZES = (
    SSD_INNER, SSD_INNER, SSD_GROUPS * SSD_STATE, SSD_GROUPS * SSD_STATE, 2 * SSD_HEADS,
    ATT_HEADS * ATT_HEAD_DIM, ATT_KV_HEADS * ATT_HEAD_DIM, ATT_KV_HEADS * ATT_HEAD_DIM,
    ML_INNER, ML_INNER, ML_INNER, ML_INNER, 4 * ML_HEADS,
    N_BRANCH * D_MODEL,
)
IN_DIM = sum(IN_SIZES)

kernel_name = 'hybrid_prefix_diffusion_step'


def rms_norm(x, gain=None):
    xf = x.astype(jnp.float32)
    y = xf * lax.rsqrt(jnp.mean(xf * xf, axis=-1, keepdims=True) + EPS)
    if gain is not None:
        y = y * gain.astype(jnp.float32)
    return y.astype(x.dtype)


def flip(t):
    return jnp.flip(t, axis=1)


def dwconv(x, w, b):
    left = (CONV_W - 1) // 2
    right = CONV_W - 1 - left
    out = lax.conv_general_dilated(x, w[:, None, :], window_strides=(1,), padding=[(left, right)],
                                   dimension_numbers=('NWC', 'WIO', 'NWC'),
                                   feature_group_count=x.shape[-1])
    return out + b


def rope_2d(x):
    L, d = x.shape[1], x.shape[-1]
    rows_n = L // GRID_W
    rows = jnp.repeat(jnp.arange(rows_n), GRID_W)
    cols = jnp.tile(jnp.arange(GRID_W), rows_n)
    half = d // 2
    nf = half // 2
    freqs = ROPE_THETA ** (-jnp.arange(nf, dtype=jnp.float32) / nf)

    def rot(xa, pos):
        ang = pos.astype(jnp.float32)[:, None] * freqs
        cos = jnp.cos(ang)[None, :, None, :]
        sin = jnp.sin(ang)[None, :, None, :]
        x1, x2 = xa[..., :nf], xa[..., nf:]
        return jnp.concatenate([x1 * cos - x2 * sin, x1 * sin + x2 * cos], axis=-1)

    xf = x.astype(jnp.float32)
    return jnp.concatenate([rot(xf[..., :half], rows), rot(xf[..., half:], cols)], axis=-1).astype(x.dtype)


def attention(q, k, v):
    b, Lq = q.shape[0], q.shape[1]
    nb = Lq // Q_BLOCK
    grp = ATT_HEADS // ATT_KV_HEADS
    qb = jnp.moveaxis(q.reshape(b, nb, Q_BLOCK, ATT_KV_HEADS, grp, ATT_HEAD_DIM), 1, 0)
    scale = ATT_HEAD_DIM ** -0.5

    def block(qblk):
        s = jnp.einsum('bqkgd,bskd->bkgqs', qblk, k).astype(jnp.float32) * scale
        pr = jax.nn.softmax(s, axis=-1).astype(v.dtype)
        return jnp.einsum('bkgqs,bskd->bqkgd', pr, v)

    o = lax.map(block, qb)
    return jnp.moveaxis(o, 0, 1).reshape(b, Lq, ATT_HEADS * ATT_HEAD_DIM)


def ssd_scan(x, dt, a, bm, cm, h0):
    b, L = x.shape[0], x.shape[1]
    nc = L // CHUNK
    mask = jnp.tril(jnp.ones((CHUNK, CHUNK), bool))[None, :, :, None]

    def chunks(t):
        return jnp.swapaxes(t.reshape((b, nc, CHUNK) + t.shape[2:]), 0, 1)

    def body(h, xs):
        xc, dc, bc, cc = xs
        acum = jnp.cumsum(dc * a, axis=1)
        seg = jnp.where(mask, acum[:, :, None, :] - acum[:, None, :, :], -jnp.inf)
        scores = jnp.einsum('bihn,bjhn->bijh', cc, bc) * jnp.exp(seg) * dc[:, None, :, :]
        y = (jnp.einsum('bijh,bjhp->bihp', scores, xc)
             + jnp.einsum('bihn,bhpn->bihp', cc, h) * jnp.exp(acum)[..., None])
        w_end = jnp.exp(acum[:, -1:, :] - acum) * dc
        h_new = (jnp.exp(acum[:, -1, :])[:, :, None, None] * h
                 + jnp.einsum('bjh,bjhn,bjhp->bhpn', w_end, bc, xc))
        return h_new, y

    h_fin, ys = lax.scan(body, h0, (chunks(x), chunks(dt), chunks(bm), chunks(cm)))
    return jnp.swapaxes(ys, 0, 1).reshape(x.shape), h_fin


def mlstm_scan(q, k, v, log_i, log_f, c0, n0, m0):
    b, L = q.shape[0], q.shape[1]
    nc = L // CHUNK
    mask = jnp.tril(jnp.ones((CHUNK, CHUNK), bool))[None, :, :, None]

    def chunks(t):
        return jnp.swapaxes(t.reshape((b, nc, CHUNK) + t.shape[2:]), 0, 1)

    def body(carry, xs):
        cp, npv, mp = carry
        qc, kc, vc, ic, fc = xs
        bcum = jnp.cumsum(fc, axis=1)
        dmat = jnp.where(mask, bcum[:, :, None, :] - bcum[:, None, :, :] + ic[:, None, :, :], -jnp.inf)
        m_t = jnp.maximum(bcum + mp[:, None, :], jnp.max(dmat, axis=2))
        s = jnp.einsum('bthd,bjhd->btjh', qc, kc) * jnp.exp(dmat - m_t[:, :, None, :])
        inter = jnp.exp(bcum + mp[:, None, :] - m_t)
        num = (jnp.einsum('btjh,bjhe->bthe', s, vc)
               + inter[..., None] * jnp.einsum('bthd,bhde->bthe', qc, cp))
        den = jnp.sum(s, axis=2) + inter * jnp.einsum('bthd,bhd->bth', qc, npv)
        h = num / jnp.maximum(jnp.abs(den), jnp.exp(-m_t))[..., None]
        b_end = bcum[:, -1, :]
        wj = b_end[:, None, :] - bcum + ic
        m_new = jnp.maximum(b_end + mp, jnp.max(wj, axis=1))
        ew = jnp.exp(wj - m_new[:, None, :])
        sc = jnp.exp(b_end + mp - m_new)
        c_new = sc[..., None, None] * cp + jnp.einsum('bjh,bjhd,bjhe->bhde', ew, kc, vc)
        n_new = sc[..., None] * npv + jnp.einsum('bjh,bjhd->bhd', ew, kc)
        return (c_new, n_new, m_new), h

    fin, hs = lax.scan(body, (c0, n0, m0), (chunks(q), chunks(k), chunks(v), chunks(log_i), chunks(log_f)))
    return jnp.swapaxes(hs, 0, 1).reshape(v.shape), fin


def mix(xm, p, ctx):
    b, L, _ = xm.shape
    dtype = xm.dtype
    f32 = jnp.float32
    u = xm @ p['w_in']
    split_at = [int(i) for i in np.cumsum(IN_SIZES)[:-1]]
    (s_x, s_z, s_b, s_c, s_dt, a_q, a_k, a_v,
     m_q, m_k, m_v, m_o, m_g, g) = jnp.split(u, split_at, axis=-1)

    xbc = jax.nn.silu(dwconv(jnp.concatenate([s_x, s_b, s_c], axis=-1), p['ssd_conv_w'], p['ssd_conv_b']))
    sx, sb, sc = jnp.split(xbc, [SSD_INNER, SSD_INNER + SSD_GROUPS * SSD_STATE], axis=-1)
    rep = SSD_HEADS // SSD_GROUPS
    sx = sx.reshape(b, L, SSD_HEADS, SSD_HEAD_DIM).astype(f32)
    sb = jnp.repeat(sb.reshape(b, L, SSD_GROUPS, SSD_STATE), rep, axis=2).astype(f32)
    sc = jnp.repeat(sc.reshape(b, L, SSD_GROUPS, SSD_STATE), rep, axis=2).astype(f32)
    dt = jax.nn.softplus(s_dt.reshape(b, L, 2, SSD_HEADS).astype(f32) + p['ssd_dt_bias'].astype(f32))
    a = -jnp.exp(p['ssd_a_log'].astype(f32))
    if ctx is None:
        h0 = jnp.zeros((b, 2, SSD_HEADS, SSD_HEAD_DIM, SSD_STATE), f32)
    else:
        h0 = ctx[2].astype(f32)
    y_f, hs_f = ssd_scan(sx, dt[:, :, 0], a[0], sb, sc, h0[:, 0])
    y_b, hs_b = ssd_scan(flip(sx), flip(dt[:, :, 1]), a[1], flip(sb), flip(sc), h0[:, 1])
    y_ssd = y_f + flip(y_b) + p['ssd_d'].astype(f32)[:, None] * sx
    y_ssd = y_ssd.reshape(b, L, SSD_INNER).astype(dtype) * jax.nn.silu(s_z)
    y_ssd = rms_norm(y_ssd, p['ssd_norm'])

    q = rms_norm(a_q.reshape(b, L, ATT_HEADS, ATT_HEAD_DIM), p['att_q_norm'])
    k = rms_norm(a_k.reshape(b, L, ATT_KV_HEADS, ATT_HEAD_DIM), p['att_k_norm'])
    v = a_v.reshape(b, L, ATT_KV_HEADS, ATT_HEAD_DIM)
    if ctx is None:
        y_att = attention(q, k, v)
    else:
        y_att = attention(rope_2d(q), jnp.concatenate([ctx[0].astype(dtype), rope_2d(k)], axis=1),
                          jnp.concatenate([ctx[1].astype(dtype), v], axis=1))

    qk = jax.nn.silu(dwconv(jnp.concatenate([m_q, m_k], axis=-1), p['ml_conv_w'], p['ml_conv_b']))
    mq, mk = jnp.split(qk, 2, axis=-1)
    shp = (b, L, ML_HEADS, ML_HEAD_DIM)
    mq = mq.reshape(shp).astype(f32)
    mk = mk.reshape(shp).astype(f32) * (ML_HEAD_DIM ** -0.5)
    mv = m_v.reshape(shp).astype(f32)
    pre = m_g.reshape(b, L, 2, 2, ML_HEADS).astype(f32) + p['ml_gate_bias'].astype(f32)
    log_i = pre[:, :, :, 0]
    log_f = jax.nn.log_sigmoid(pre[:, :, :, 1])
    if ctx is None:
        c0 = jnp.zeros((b, 2, ML_HEADS, ML_HEAD_DIM, ML_HEAD_DIM), f32)
        n0 = jnp.zeros((b, 2, ML_HEADS, ML_HEAD_DIM), f32)
        m0 = jnp.zeros((b, 2, ML_HEADS), f32)
    else:
        c0, n0, m0 = [t.astype(f32) for t in ctx[3:]]
    h_f, st_f = mlstm_scan(mq, mk, mv, log_i[:, :, 0], log_f[:, :, 0], c0[:, 0], n0[:, 0], m0[:, 0])
    h_b, st_b = mlstm_scan(flip(mq), flip(mk), flip(mv), flip(log_i[:, :, 1]), flip(log_f[:, :, 1]),
                           c0[:, 1], n0[:, 1], m0[:, 1])
    h_ml = rms_norm(h_f + flip(h_b)).reshape(b, L, ML_INNER) * p['ml_norm'].astype(f32)
    y_ml = h_ml.astype(dtype) * jax.nn.sigmoid(m_o)

    gate = jax.nn.sigmoid(g.reshape(b, L, N_BRANCH, D_MODEL))
    branches = jnp.stack([y_ssd, y_att, y_ml], axis=2)
    proj = jnp.einsum('blnc,ncd->blnd', branches, p['w_branch'])
    out = jnp.sum(gate * proj, axis=2) @ p['w_out']
    if ctx is None:
        new_ctx = (k, v, jnp.stack([hs_f, hs_b], axis=1),
                   jnp.stack([st_f[0], st_b[0]], axis=1),
                   jnp.stack([st_f[1], st_b[1]], axis=1),
                   jnp.stack([st_f[2], st_b[2]], axis=1))
    else:
        new_ctx = None
    return out, new_ctx


def layer(x, mod, p, ctx):
    shift1, scale1, gate1, shift2, scale2, gate2 = jnp.split(mod.astype(x.dtype), 6, axis=-1)
    h = rms_norm(x) * (1 + scale1) + shift1
    out, new_ctx = mix(h, p, ctx)
    x = x + gate1 * out
    h = rms_norm(x) * (1 + scale2) + shift2
    ab = h @ p['w_ffn_in']
    a_, b_ = jnp.split(ab, 2, axis=-1)
    x = x + gate2 * ((jax.nn.silu(a_) * b_) @ p['w_ffn_out'])
    return x, new_ctx


def setup_inputs(seed: int = 0) -> dict:
    key = jax.random.key(seed)
    ks = jax.random.split(key, 32)
    f32 = jnp.float32

    def nrm(i, shape, scale=1.0):
        return jax.random.normal(ks[i], shape, f32) * scale

    def gain(i, shape):
        return 1.0 + 0.02 * jax.random.normal(ks[i], shape, f32)

    dt0 = jnp.exp(jax.random.uniform(ks[14], (DEPTH, 2, SSD_HEADS), f32, math.log(1e-3), math.log(1e-1)))
    ssd_dt_bias = dt0 + jnp.log(-jnp.expm1(-dt0))
    ssd_a_log = jnp.log(jax.random.uniform(ks[15], (DEPTH, 2, SSD_HEADS), f32, 1.0, 16.0))
    f_bias = 3.0 + 3.0 * jax.random.uniform(ks[23], (DEPTH, 2, ML_HEADS), f32)
    ml_gate_bias = jnp.stack([nrm(22, (DEPTH, 2, ML_HEADS), 0.1), f_bias], axis=2)
    return {
        'x_prompt': nrm(0, (BATCH, SEQ, D_MODEL)),
        'x_sample': nrm(1, (DEC_BATCH, DEC_SEQ, D_MODEL)),
        'cache_k': nrm(2, (DEC_BATCH, DEPTH, PAST_LEN, ATT_KV_HEADS, ATT_HEAD_DIM)),
        'cache_v': nrm(3, (DEC_BATCH, DEPTH, PAST_LEN, ATT_KV_HEADS, ATT_HEAD_DIM)),
        'state_ssd': nrm(4, (DEC_BATCH, DEPTH, 2, SSD_HEADS, SSD_HEAD_DIM, SSD_STATE), 0.5),
        'state_ml_c': nrm(5, (DEC_BATCH, DEPTH, 2, ML_HEADS, ML_HEAD_DIM, ML_HEAD_DIM), 0.1),
        'state_ml_n': nrm(6, (DEC_BATCH, DEPTH, 2, ML_HEADS, ML_HEAD_DIM), 0.1),
        'state_ml_m': nrm(7, (DEC_BATCH, DEPTH, 2, ML_HEADS)),
        'c': nrm(8, (DEC_BATCH, D_MODEL)),
        'c_ctx': nrm(9, (D_MODEL,)),
        'w_ada': nrm(10, (DEPTH, D_MODEL, 6 * D_MODEL), 0.5 * D_MODEL ** -0.5),
        'b_ada': nrm(11, (DEPTH, 6 * D_MODEL), 0.02),
        'w_in': nrm(12, (DEPTH, D_MODEL, IN_DIM), D_MODEL ** -0.5),
        'ssd_conv_w': nrm(13, (DEPTH, CONV_W, SSD_INNER + 2 * SSD_GROUPS * SSD_STATE), CONV_W ** -0.5),
        'ssd_conv_b': nrm(16, (DEPTH, SSD_INNER + 2 * SSD_GROUPS * SSD_STATE), 0.02),
        'ssd_a_log': ssd_a_log,
        'ssd_dt_bias': ssd_dt_bias,
        'ssd_d': gain(17, (DEPTH, SSD_HEADS)),
        'ssd_norm': gain(18, (DEPTH, SSD_INNER)),
        'att_q_norm': gain(19, (DEPTH, ATT_HEAD_DIM)),
        'att_k_norm': gain(20, (DEPTH, ATT_HEAD_DIM)),
        'ml_conv_w': nrm(21, (DEPTH, CONV_W, 2 * ML_INNER), CONV_W ** -0.5),
        'ml_conv_b': nrm(24, (DEPTH, 2 * ML_INNER), 0.02),
        'ml_gate_bias': ml_gate_bias,
        'ml_norm': gain(25, (DEPTH, ML_INNER)),
        'w_branch': nrm(26, (DEPTH, N_BRANCH, BRANCH_W, D_MODEL), BRANCH_W ** -0.5),
        'w_out': nrm(27, (DEPTH, D_MODEL, D_MODEL), D_MODEL ** -0.5),
        'w_ffn_in': nrm(28, (DEPTH, D_MODEL, 2 * FFN_HIDDEN), D_MODEL ** -0.5),
        'w_ffn_out': nrm(29, (DEPTH, FFN_HIDDEN, D_MODEL), FFN_HIDDEN ** -0.5),
        'final_norm': gain(30, (D_MODEL,)),
    }


def reference(x_prompt, x_sample, cache_k, cache_v, state_ssd, state_ml_c, state_ml_n, state_ml_m,
              c, c_ctx, w_ada, b_ada, w_in, ssd_conv_w, ssd_conv_b, ssd_a_log, ssd_dt_bias, ssd_d,
              ssd_norm, att_q_norm, att_k_norm, ml_conv_w, ml_conv_b, ml_gate_bias, ml_norm,
              w_branch, w_out, w_ffn_in, w_ffn_out, final_norm):
    y_p, y_s = x_prompt, x_sample
    ks, vs, ssd_st, ml_c, ml_n, ml_m = [], [], [], [], [], []
    for l in range(DEPTH):
        p = {'w_in': w_in[l], 'ssd_conv_w': ssd_conv_w[l], 'ssd_conv_b': ssd_conv_b[l],
             'ssd_a_log': ssd_a_log[l], 'ssd_dt_bias': ssd_dt_bias[l], 'ssd_d': ssd_d[l],
             'ssd_norm': ssd_norm[l], 'att_q_norm': att_q_norm[l], 'att_k_norm': att_k_norm[l],
             'ml_conv_w': ml_conv_w[l], 'ml_conv_b': ml_conv_b[l], 'ml_gate_bias': ml_gate_bias[l],
             'ml_norm': ml_norm[l], 'w_branch': w_branch[l], 'w_out': w_out[l],
             'w_ffn_in': w_ffn_in[l], 'w_ffn_out': w_ffn_out[l]}
        mod_ctx = (jax.nn.silu(c_ctx) @ w_ada[l] + b_ada[l])[None, None, :]
        mod_lat = (jax.nn.silu(c) @ w_ada[l] + b_ada[l])[:, None, :]
        y_p, nc = layer(y_p, mod_ctx, p, None)
        ks.append(nc[0]); vs.append(nc[1]); ssd_st.append(nc[2])
        ml_c.append(nc[3]); ml_n.append(nc[4]); ml_m.append(nc[5])
        y_s, _ = layer(y_s, mod_lat, p, (cache_k[:, l], cache_v[:, l], state_ssd[:, l],
                                         state_ml_c[:, l], state_ml_n[:, l], state_ml_m[:, l]))
    dt = x_prompt.dtype
    y_prompt = rms_norm(y_p, final_norm)
    y_sample = rms_norm(y_s, final_norm)
    new_cache_k = jnp.stack(ks, axis=1).astype(dt)
    new_cache_v = jnp.stack(vs, axis=1).astype(dt)
    new_state_ssd = jnp.stack(ssd_st, axis=1).astype(dt)
    new_state_ml_c = jnp.stack(ml_c, axis=1).astype(dt)
    new_state_ml_n = jnp.stack(ml_n, axis=1).astype(dt)
    new_state_ml_m = jnp.stack(ml_m, axis=1).astype(dt)
    return (y_prompt, y_sample, new_cache_k, new_cache_v, new_state_ssd, new_state_ml_c, new_state_ml_n, new_state_ml_m)
```

```python
import functools

import numpy as np
import jax
import jax.numpy as jnp
from jax import lax
from jax.experimental import pallas as pl
from jax.experimental.pallas import tpu as pltpu

F32 = jnp.float32
BF16 = jnp.bfloat16

D_MODEL = 1024
GRID_W = 64
EPS = 1e-6
CONV_W = 4
CHUNK = 128
SSD_HEADS = 16
SSD_HEAD_DIM = 64
SSD_GROUPS = 4
SSD_STATE = 64
SSD_BC = SSD_GROUPS * SSD_STATE
SSD_CONV = D_MODEL + 2 * SSD_BC
ATT_HEADS = 16
ATT_HEAD_DIM = 64
ATT_KV_HEADS = 4
ATT_KV = ATT_KV_HEADS * ATT_HEAD_DIM
ATT_QKV = D_MODEL + 2 * ATT_KV
ROPE_THETA = 10000.0
ML_HEADS = 8
ML_HEAD_DIM = 128
FFN_HIDDEN = 2816
GATE_W = 128

U_CONV = 0
U_ATT = 1536
U_MQ = 3072
U_MK = 4096
U_MV = 5120
U_SZ = 6144
U_MO = 7168
U_G = 8192
U_W = 11264

G_DT = 0
G_LI = 32
G_LF = 48

VMEM_LIMIT = 48 * 1024 * 1024
NT_DIMS = (((1,), (1,)), ((), ()))
TN_DIMS = (((0,), (0,)), ((), ()))


def _cparams(*sem):
    return pltpu.CompilerParams(dimension_semantics=sem, vmem_limit_bytes=VMEM_LIMIT)


def _sigmoid(x):
    return 1.0 / (1.0 + jnp.exp(-x))


def _softplus(x):
    return jnp.maximum(x, 0.0) + jnp.log1p(jnp.exp(-jnp.abs(x)))


def _dot(a, b):
    return jnp.dot(a, b, preferred_element_type=F32)


def _dot_nt(a, b):
    return lax.dot_general(a, b, NT_DIMS, preferred_element_type=F32)


def _dot_tn(a, b):
    return lax.dot_general(a, b, TN_DIMS, preferred_element_type=F32)


def _ada_kernel(c_ref, w_ref, b_ref, o_ref):
    c = c_ref[...]
    s = (c * _sigmoid(c)).astype(BF16)
    o_ref[0] = _dot(s, w_ref[0].astype(BF16)) + b_ref[0]


def _ada(cc, w_ada, b_ada):
    depth, _, n6 = w_ada.shape
    rows = cc.shape[0]
    tn = 512
    return pl.pallas_call(
        _ada_kernel,
        grid=(depth, n6 // tn),
        in_specs=[pl.BlockSpec((rows, D_MODEL), lambda l, j: (0, 0)),
                  pl.BlockSpec((1, D_MODEL, tn), lambda l, j: (l, 0, j)),
                  pl.BlockSpec((1, 1, tn), lambda l, j: (l, 0, j))],
        out_specs=pl.BlockSpec((1, rows, tn), lambda l, j: (l, 0, j)),
        out_shape=jax.ShapeDtypeStruct((depth, rows, n6), F32),
        compiler_params=_cparams("parallel", "parallel"),
        name="ada_mod",
    )(cc, w_ada, b_ada.reshape(depth, 1, n6))


def _rms(x):
    return x * lax.rsqrt(jnp.mean(x * x, axis=-1, keepdims=True) + EPS)


def _norm_mod_kernel(x_ref, sc_ref, sh_ref, o_ref):
    o_ref[...] = (_rms(x_ref[...]) * (1.0 + sc_ref[0]) + sh_ref[0]).astype(o_ref.dtype)


def _norm_gain_kernel(x_ref, g_ref, o_ref):
    o_ref[...] = _rms(x_ref[...]) * g_ref[...]


def _mod_spec(nb, seq, tm):
    if nb == 1:
        return pl.BlockSpec((1, 1, D_MODEL), lambda i: (0, 0, 0))
    return pl.BlockSpec((1, 1, D_MODEL), lambda i: ((i * tm) // seq, 0, 0))


def _norm_mod(x, scale, shift, seq):
    t = x.shape[0]
    tm = min(t, 512)
    nb = scale.shape[0]
    return pl.pallas_call(
        _norm_mod_kernel,
        grid=(t // tm,),
        in_specs=[pl.BlockSpec((tm, D_MODEL), lambda i: (i, 0)),
                  _mod_spec(nb, seq, tm), _mod_spec(nb, seq, tm)],
        out_specs=pl.BlockSpec((tm, D_MODEL), lambda i: (i, 0)),
        out_shape=jax.ShapeDtypeStruct((t, D_MODEL), BF16),
        compiler_params=_cparams("parallel"),
        name="norm_mod",
    )(x, scale, shift)


def _norm_gain(x, gain):
    t = x.shape[0]
    tm = min(t, 512)
    return pl.pallas_call(
        _norm_gain_kernel,
        grid=(t // tm,),
        in_specs=[pl.BlockSpec((tm, D_MODEL), lambda i: (i, 0)),
                  pl.BlockSpec((1, D_MODEL), lambda i: (0, 0))],
        out_specs=pl.BlockSpec((tm, D_MODEL), lambda i: (i, 0)),
        out_shape=jax.ShapeDtypeStruct((t, D_MODEL), F32),
        compiler_params=_cparams("parallel"),
        name="final_norm",
    )(x, gain.reshape(1, D_MODEL))


def _proj_kernel(a_ref, w_ref, o_ref, *, silu_lo, sig_lo):
    j = pl.program_id(1)
    acc = _dot(a_ref[...], w_ref[...])

    @pl.when(j < silu_lo)
    def _():
        o_ref[...] = acc

    @pl.when(jnp.logical_and(j >= silu_lo, j < sig_lo))
    def _():
        o_ref[...] = acc * _sigmoid(acc)

    @pl.when(j >= sig_lo)
    def _():
        o_ref[...] = _sigmoid(acc)


def _proj(h, w_main):
    t = h.shape[0]
    tm = min(t, 1024)
    tn = 1024
    kern = functools.partial(_proj_kernel, silu_lo=U_SZ // tn, sig_lo=U_MO // tn)
    return pl.pallas_call(
        kern,
        grid=(t // tm, U_W // tn),
        in_specs=[pl.BlockSpec((tm, D_MODEL), lambda i, j: (i, 0)),
                  pl.BlockSpec((D_MODEL, tn), lambda i, j: (0, j))],
        out_specs=pl.BlockSpec((tm, tn), lambda i, j: (i, j)),
        out_shape=jax.ShapeDtypeStruct((t, U_W), F32),
        compiler_params=_cparams("parallel", "parallel"),
        name="in_proj",
    )(h, w_main)


def _split3(x):
    hi = x.astype(BF16)
    r1 = x - hi.astype(F32)
    mid = r1.astype(BF16)
    lo = (r1 - mid.astype(F32)).astype(BF16)
    return hi, mid, lo


def _tri_left(tri, x):
    hi, mid, lo = _split3(x)
    return _dot(tri, hi) + _dot(tri, mid) + _dot(tri, lo)


def _tri_right(x, tri):
    hi, mid, lo = _split3(x)
    return _dot(hi, tri) + _dot(mid, tri) + _dot(lo, tri)


def _gate_vals(u, idx, a):
    sp = _softplus(u)
    ls = -_softplus(-u)
    val = jnp.where(idx < G_LI, sp, jnp.where(idx < G_LF, u, jnp.where(idx < G_LF + 16, ls, 0.0)))
    cin = jnp.where(idx < G_LI, sp * a, jnp.where(jnp.logical_and(idx >= G_LF, idx < G_LF + 16), ls, 0.0))
    return val, cin


def _gate_kernel(h_ref, ws_ref, wst_ref, b_ref, bt_ref, a_ref, at_ref, dm_ref, dmt_ref,
                 tlo_ref, tup_ref, val_ref, cum_ref, valt_ref, cumt_ref):
    h = h_ref[...]
    tlo = tlo_ref[...]
    tup = tup_ref[...]
    shape = (CHUNK, GATE_W)
    u = _dot(h, ws_ref[...]) + b_ref[...]
    val, cin = _gate_vals(u, lax.broadcasted_iota(jnp.int32, shape, 1), a_ref[...])
    val_ref[...] = val
    cum_ref[...] = jnp.where(dm_ref[...] > 0.5, _tri_left(tup, cin), _tri_left(tlo, cin))
    ut = _dot_nt(wst_ref[...], h) + bt_ref[...]
    valt, cint = _gate_vals(ut, lax.broadcasted_iota(jnp.int32, shape, 0), at_ref[...])
    valt_ref[...] = valt
    cumt_ref[...] = jnp.where(dmt_ref[...] > 0.5, _tri_right(cint, tlo), _tri_right(cint, tup))


def _gate(h, ws, wst, bias, avec, consts):
    t = h.shape[0]
    dmask, tlo, tup = consts
    row = lambda i: (i, 0)
    col = lambda i: (0, i)
    fix = lambda i: (0, 0)
    tm_out = jax.ShapeDtypeStruct((t, GATE_W), F32)
    fm_out = jax.ShapeDtypeStruct((GATE_W, t), F32)
    return pl.pallas_call(
        _gate_kernel,
        grid=(t // CHUNK,),
        in_specs=[pl.BlockSpec((CHUNK, D_MODEL), row),
                  pl.BlockSpec((D_MODEL, GATE_W), fix),
                  pl.BlockSpec((GATE_W, D_MODEL), fix),
                  pl.BlockSpec((1, GATE_W), fix), pl.BlockSpec((GATE_W, 1), fix),
                  pl.BlockSpec((1, GATE_W), fix), pl.BlockSpec((GATE_W, 1), fix),
                  pl.BlockSpec((1, GATE_W), fix), pl.BlockSpec((GATE_W, 1), fix),
                  pl.BlockSpec((CHUNK, CHUNK), fix), pl.BlockSpec((CHUNK, CHUNK), fix)],
        out_specs=[pl.BlockSpec((CHUNK, GATE_W), row), pl.BlockSpec((CHUNK, GATE_W), row),
                   pl.BlockSpec((GATE_W, CHUNK), col), pl.BlockSpec((GATE_W, CHUNK), col)],
        out_shape=[tm_out, tm_out, fm_out, fm_out],
        compiler_params=_cparams("parallel"),
        name="gate_prep",
    )(h, ws, wst, bias.reshape(1, GATE_W), bias.reshape(GATE_W, 1),
      avec.reshape(1, GATE_W), avec.reshape(GATE_W, 1),
      dmask.reshape(1, GATE_W), dmask.reshape(GATE_W, 1), tlo, tup)


def _conv_kernel(x_ref, w_ref, b_ref, o_ref):
    x = x_ref[...]
    seq = x.shape[0]
    row = lax.broadcasted_iota(jnp.int32, x.shape, 0)
    acc = x * w_ref[1:2, :] + b_ref[...]
    acc = acc + jnp.where(row >= 1, pltpu.roll(x, 1, 0), 0.0) * w_ref[0:1, :]
    acc = acc + jnp.where(row < seq - 1, pltpu.roll(x, seq - 1, 0), 0.0) * w_ref[2:3, :]
    acc = acc + jnp.where(row < seq - 2, pltpu.roll(x, seq - 2, 0), 0.0) * w_ref[3:4, :]
    o_ref[...] = acc * _sigmoid(acc)


def _conv(u, col0, width, w, b, nb, seq):
    tc = 256
    c0 = col0 // tc
    return pl.pallas_call(
        _conv_kernel,
        grid=(nb, width // tc),
        in_specs=[pl.BlockSpec((seq, tc), lambda i, j: (i, c0 + j)),
                  pl.BlockSpec((CONV_W, tc), lambda i, j: (0, j)),
                  pl.BlockSpec((1, tc), lambda i, j: (0, j))],
        out_specs=pl.BlockSpec((seq, tc), lambda i, j: (i, j)),
        out_shape=jax.ShapeDtypeStruct((nb * seq, width), F32),
        compiler_params=_cparams("parallel", "parallel"),
        name="dwconv_silu",
    )(u, w, b.reshape(1, width))


def _tri_mask(fwd):
    row = lax.broadcasted_iota(jnp.int32, (CHUNK, CHUNK), 0)
    col = lax.broadcasted_iota(jnp.int32, (CHUNK, CHUNK), 1)
    return row >= col if fwd else row <= col


def _ssd_dir(d, x_ref, val_ref, cum_ref, valt_ref, cumt_ref, y_ref, h_scr):
    mask = _tri_mask(d == 0)
    end = CHUNK - 1 if d == 0 else 0
    dt_c = val_ref[...]
    ac_c = cum_ref[...]
    dt_r = valt_ref[...]
    ac_r = cumt_ref[...]
    for g in range(SSD_GROUPS):
        b0 = D_MODEL + g * SSD_STATE
        c0 = D_MODEL + SSD_BC + g * SSD_STATE
        bb = x_ref[:, b0:b0 + SSD_STATE].astype(BF16)
        cb = x_ref[:, c0:c0 + SSD_STATE].astype(BF16)
        cbt = _dot_nt(cb, bb)
        for r in range(SSD_HEADS // SSD_GROUPS):
            hh = g * (SSD_HEADS // SSD_GROUPS) + r
            lane = G_DT + d * SSD_HEADS + hh
            a_i = ac_c[:, lane:lane + 1]
            a_j = ac_r[lane:lane + 1, :]
            dt_i = dt_c[:, lane:lane + 1]
            dt_j = dt_r[lane:lane + 1, :]
            seg = jnp.where(mask, a_i - a_j, -jnp.inf)
            scores = (cbt * jnp.exp(seg) * dt_j).astype(BF16)
            xh = x_ref[:, hh * SSD_HEAD_DIM:(hh + 1) * SSD_HEAD_DIM]
            hst = h_scr[d, hh]
            y = _dot(scores, xh.astype(BF16)) + jnp.exp(a_i) * _dot_nt(cb, hst.astype(BF16))
            y_ref[:, hh * SSD_HEAD_DIM:(hh + 1) * SSD_HEAD_DIM] = y
            tot = ac_c[end:end + 1, lane:lane + 1]
            w_end = jnp.exp(tot - a_i) * dt_i
            h_scr[d, hh] = jnp.exp(tot) * hst + _dot_tn((xh * w_end).astype(BF16), bb)


def _ssd_kernel(*refs, has_h0, emit_state):
    xf_ref, xb_ref = refs[0:2]
    gf = refs[2:6]
    gb = refs[6:10]
    k = 10
    if has_h0:
        h0_ref = refs[k]
        k += 1
    yf_ref, yb_ref = refs[k:k + 2]
    k += 2
    if emit_state:
        hout_ref = refs[k]
        k += 1
    h_scr = refs[k]
    c = pl.program_id(1)

    @pl.when(c == 0)
    def _():
        if has_h0:
            h_scr[...] = h0_ref[0, 0]
        else:
            h_scr[...] = jnp.zeros(h_scr.shape, F32)

    _ssd_dir(0, xf_ref, *gf, yf_ref, h_scr)
    _ssd_dir(1, xb_ref, *gb, yb_ref, h_scr)

    if emit_state:
        @pl.when(c == pl.num_programs(1) - 1)
        def _():
            hout_ref[0] = h_scr[...]


def _chunk_specs(nc, width):
    fwd = pl.BlockSpec((CHUNK, width), lambda b, c: (b * nc + c, 0))
    bwd = pl.BlockSpec((CHUNK, width), lambda b, c: (b * nc + nc - 1 - c, 0))
    return fwd, bwd


def _gate_specs(nc):
    tf, tb = _chunk_specs(nc, GATE_W)
    ff = pl.BlockSpec((GATE_W, CHUNK), lambda b, c: (0, b * nc + c))
    fb = pl.BlockSpec((GATE_W, CHUNK), lambda b, c: (0, b * nc + nc - 1 - c))
    return [tf, tf, ff, ff], [tb, tb, fb, fb]


def _ssd(xbc, gates, nb, seq, state, layer):
    nc = seq // CHUNK
    t = nb * seq
    has_h0 = state is not None
    xf, xb = _chunk_specs(nc, SSD_CONV)
    gfs, gbs = _gate_specs(nc)
    yf, yb = _chunk_specs(nc, D_MODEL)
    st_shape = (2, SSD_HEADS, SSD_HEAD_DIM, SSD_STATE)
    in_specs = [xf, xb] + gfs + gbs
    args = [xbc, xbc] + list(gates) + list(gates)
    if has_h0:
        in_specs.append(pl.BlockSpec((1, 1) + st_shape, lambda b, c: (b, layer, 0, 0, 0, 0)))
        args.append(state)
    out_specs = [yf, yb]
    out_shape = [jax.ShapeDtypeStruct((t, D_MODEL), F32)] * 2
    if not has_h0:
        out_specs.append(pl.BlockSpec((1,) + st_shape, lambda b, c: (b, 0, 0, 0, 0)))
        out_shape.append(jax.ShapeDtypeStruct((nb,) + st_shape, F32))
    return pl.pallas_call(
        functools.partial(_ssd_kernel, has_h0=has_h0, emit_state=not has_h0),
        grid=(nb, nc),
        in_specs=in_specs,
        out_specs=out_specs,
        out_shape=out_shape,
        scratch_shapes=[pltpu.VMEM(st_shape, F32)],
        compiler_params=_cparams("parallel", "arbitrary"),
        name="ssd_scan",
    )(*args)


def _ml_dir(d, q_ref, k_ref, v_ref, val_ref, cum_ref, valt_ref, cumt_ref, y_ref, c_scr, n_scr, m_scr):
    mask = _tri_mask(d == 0)
    end = CHUNK - 1 if d == 0 else 0
    li_c = val_ref[...]
    bc_c = cum_ref[...]
    li_r = valt_ref[...]
    bc_r = cumt_ref[...]
    kscale = ML_HEAD_DIM ** -0.5
    for hh in range(ML_HEADS):
        li = G_LI + d * ML_HEADS + hh
        lf = G_LF + d * ML_HEADS + hh
        sl = slice(hh * ML_HEAD_DIM, (hh + 1) * ML_HEAD_DIM)
        b_i = bc_c[:, lf:lf + 1]
        b_j = bc_r[lf:lf + 1, :]
        i_i = li_c[:, li:li + 1]
        i_j = li_r[li:li + 1, :]
        q = q_ref[:, sl]
        k = k_ref[:, sl] * kscale
        vb = v_ref[:, sl].astype(BF16)
        qb = q.astype(BF16)
        mp = m_scr[d * ML_HEADS + hh][0:1, 0:1]
        cst = c_scr[d, hh]
        nst = n_scr[d, hh:hh + 1, :]
        dmat = jnp.where(mask, b_i - b_j + i_j, -jnp.inf)
        m_t = jnp.maximum(b_i + mp, jnp.max(dmat, axis=1, keepdims=True))
        s = _dot_nt(qb, k.astype(BF16)) * jnp.exp(dmat - m_t)
        inter = jnp.exp(b_i + mp - m_t)
        num = _dot(s.astype(BF16), vb) + inter * _dot(qb, cst.astype(BF16))
        den = jnp.sum(s, axis=1, keepdims=True) + inter * jnp.sum(q * nst, axis=1, keepdims=True)
        y_ref[:, sl] = num / jnp.maximum(jnp.abs(den), jnp.exp(-m_t))
        b_end = bc_c[end:end + 1, lf:lf + 1]
        wj = b_end - b_i + i_i
        m_new = jnp.maximum(b_end + mp, jnp.max(wj, axis=0, keepdims=True))
        kw = k * jnp.exp(wj - m_new)
        decay = jnp.exp(b_end + mp - m_new)
        c_scr[d, hh] = decay * cst + _dot_tn(kw.astype(BF16), vb)
        n_scr[d, hh:hh + 1, :] = decay * nst + jnp.sum(kw, axis=0, keepdims=True)
        m_scr[d * ML_HEADS + hh] = jnp.broadcast_to(m_new, (8, 128))


def _ml_kernel(*refs, has_init, emit_state):
    qf_ref, kf_ref, vf_ref, qb_ref, kb_ref, vb_ref = refs[0:6]
    gf = refs[6:10]
    gb = refs[10:14]
    k = 14
    if has_init:
        c0_ref, n0_ref, m0_ref = refs[k:k + 3]
        k += 3
    yf_ref, yb_ref = refs[k:k + 2]
    k += 2
    if emit_state:
        cout_ref, nout_ref, mout_ref = refs[k:k + 3]
        k += 3
    c_scr, n_scr, m_scr = refs[k:k + 3]
    c = pl.program_id(1)

    @pl.when(c == 0)
    def _():
        if has_init:
            c_scr[...] = c0_ref[0, 0]
            n_scr[...] = n0_ref[0, 0]
            for d in range(2):
                for hh in range(ML_HEADS):
                    m_scr[d * ML_HEADS + hh] = jnp.broadcast_to(m0_ref[0, d, hh:hh + 1, 0:1], (8, 128))
        else:
            c_scr[...] = jnp.zeros(c_scr.shape, F32)
            n_scr[...] = jnp.zeros(n_scr.shape, F32)
            m_scr[...] = jnp.zeros(m_scr.shape, F32)

    _ml_dir(0, qf_ref, kf_ref, vf_ref, *gf, yf_ref, c_scr, n_scr, m_scr)
    _ml_dir(1, qb_ref, kb_ref, vb_ref, *gb, yb_ref, c_scr, n_scr, m_scr)

    if emit_state:
        @pl.when(c == pl.num_programs(1) - 1)
        def _():
            cout_ref[0] = c_scr[...]
            nout_ref[0] = n_scr[...]
            for d in range(2):
                for hh in range(ML_HEADS):
                    mout_ref[0, d, hh:hh + 1, :] = m_scr[d * ML_HEADS + hh][0:1, :]


def _mlstm(qk, u, gates, nb, seq, states, layer):
    nc = seq // CHUNK
    t = nb * seq
    has_init = states is not None
    mvb = U_MV // D_MODEL

    def tok(col, rev):
        if rev:
            return pl.BlockSpec((CHUNK, D_MODEL), lambda b, c: (b * nc + nc - 1 - c, col))
        return pl.BlockSpec((CHUNK, D_MODEL), lambda b, c: (b * nc + c, col))

    gfs, gbs = _gate_specs(nc)
    in_specs = [tok(0, False), tok(1, False), tok(mvb, False),
                tok(0, True), tok(1, True), tok(mvb, True)] + gfs + gbs
    args = [qk, qk, u, qk, qk, u] + list(gates) + list(gates)
    c_shape = (2, ML_HEADS, ML_HEAD_DIM, ML_HEAD_DIM)
    n_shape = (2, ML_HEADS, ML_HEAD_DIM)
    if has_init:
        c0, n0, m0 = states
        in_specs += [pl.BlockSpec((1, 1) + c_shape, lambda b, c: (b, layer, 0, 0, 0, 0)),
                     pl.BlockSpec((1, 1) + n_shape, lambda b, c: (b, layer, 0, 0, 0)),
                     pl.BlockSpec((1,) + n_shape, lambda b, c: (b, 0, 0, 0))]
        args += [c0, n0, m0]
    out_specs = [tok(0, False), tok(0, True)]
    out_shape = [jax.ShapeDtypeStruct((t, D_MODEL), F32)] * 2
    if not has_init:
        out_specs += [pl.BlockSpec((1,) + c_shape, lambda b, c: (b, 0, 0, 0, 0)),
                      pl.BlockSpec((1,) + n_shape, lambda b, c: (b, 0, 0, 0)),
                      pl.BlockSpec((1,) + n_shape, lambda b, c: (b, 0, 0, 0))]
        out_shape += [jax.ShapeDtypeStruct((nb,) + c_shape, F32),
                      jax.ShapeDtypeStruct((nb,) + n_shape, F32),
                      jax.ShapeDtypeStruct((nb,) + n_shape, F32)]
    return pl.pallas_call(
        functools.partial(_ml_kernel, has_init=has_init, emit_state=not has_init),
        grid=(nb, nc),
        in_specs=in_specs,
        out_specs=out_specs,
        out_shape=out_shape,
        scratch_shapes=[pltpu.VMEM(c_shape, F32), pltpu.VMEM(n_shape, F32),
                        pltpu.VMEM((2 * ML_HEADS, 8, 128), F32)],
        compiler_params=_cparams("parallel", "arbitrary"),
        name="mlstm_scan",
    )(*args)


def _qkprep_kernel(*refs, rope, emit_f32):
    u_ref, gain_ref = refs[0:2]
    k = 2
    if rope:
        cos_ref, sin_ref = refs[k:k + 2]
        k += 2
    q_ref, k_ref, v_ref = refs[k:k + 3]
    k += 3
    if emit_f32:
        kf_ref, vf_ref = refs[k:k + 2]
        k += 2
    xn_scr = refs[k]
    nqk = D_MODEL + ATT_KV
    for hh in range(nqk // ATT_HEAD_DIM):
        sl = slice(hh * ATT_HEAD_DIM, (hh + 1) * ATT_HEAD_DIM)
        xn_scr[:, sl] = _rms(u_ref[:, sl])
    xn = xn_scr[...] * gain_ref[...]
    v = u_ref[:, nqk:nqk + ATT_KV]
    if emit_f32:
        kf_ref[...] = xn[:, D_MODEL:nqk]
        vf_ref[...] = v
    if rope:
        lane = lax.broadcasted_iota(jnp.int32, xn.shape, 1)
        quarter = ATT_HEAD_DIM // 4
        first = jnp.bitwise_and(lane, 2 * quarter - 1) < quarter
        swapped = jnp.where(first, pltpu.roll(xn, nqk - quarter, 1), pltpu.roll(xn, quarter, 1))
        xn = xn * cos_ref[...] + swapped * sin_ref[...]
    q_ref[...] = (xn[:, :D_MODEL] * (ATT_HEAD_DIM ** -0.5)).astype(BF16)
    k_ref[...] = xn[:, D_MODEL:nqk].astype(BF16)
    v_ref[...] = v.astype(BF16)


def _qkprep(u, gain, tables, nb, seq, emit_f32):
    t = nb * seq
    tl = min(seq, 256)
    nl = seq // tl
    nqk = D_MODEL + ATT_KV
    rope = tables is not None
    ub = U_ATT // ATT_QKV
    tok = lambda w: pl.BlockSpec((tl, w), lambda i, b: (b * nl + i, 0))
    in_specs = [pl.BlockSpec((tl, ATT_QKV), lambda i, b: (b * nl + i, ub)),
                pl.BlockSpec((1, nqk), lambda i, b: (0, 0))]
    args = [u, gain]
    if rope:
        in_specs += [pl.BlockSpec((tl, nqk), lambda i, b: (i, 0))] * 2
        args += list(tables)
    out_specs = [tok(D_MODEL), tok(ATT_KV), tok(ATT_KV)]
    out_shape = [jax.ShapeDtypeStruct((t, D_MODEL), BF16),
                 jax.ShapeDtypeStruct((t, ATT_KV), BF16),
                 jax.ShapeDtypeStruct((t, ATT_KV), BF16)]
    if emit_f32:
        out_specs += [tok(ATT_KV), tok(ATT_KV)]
        out_shape += [jax.ShapeDtypeStruct((t, ATT_KV), F32)] * 2
    return pl.pallas_call(
        functools.partial(_qkprep_kernel, rope=rope, emit_f32=emit_f32),
        grid=(nl, nb),
        in_specs=in_specs,
        out_specs=out_specs,
        out_shape=out_shape,
        scratch_shapes=[pltpu.VMEM((tl, nqk), F32)],
        compiler_params=_cparams("parallel", "parallel"),
        name="qk_prep",
    )(*args)


def _attn_kernel(*refs, has_cache, tq):
    q_ref, kn_ref, vn_ref = refs[0:3]
    k = 3
    if has_cache:
        kc_ref, vc_ref = refs[k:k + 2]
        k += 2
    o_ref = refs[k]
    grp = ATT_HEADS // ATT_KV_HEADS
    hd = ATT_HEAD_DIM
    for kvh in range(ATT_KV_HEADS):
        ksl = slice(kvh * hd, (kvh + 1) * hd)
        qs = jnp.concatenate(
            [q_ref[:, (kvh * grp + g) * hd:(kvh * grp + g + 1) * hd] for g in range(grp)], axis=0)
        s_n = _dot_nt(qs, kn_ref[:, ksl])
        m = jnp.max(s_n, axis=-1, keepdims=True)
        if has_cache:
            s_c = _dot_nt(qs, kc_ref[0, 0, :, ksl].astype(BF16))
            m = jnp.maximum(m, jnp.max(s_c, axis=-1, keepdims=True))
        p_n = jnp.exp(s_n - m)
        l = jnp.sum(p_n, axis=-1, keepdims=True)
        o = _dot(p_n.astype(BF16), vn_ref[:, ksl])
        if has_cache:
            p_c = jnp.exp(s_c - m)
            l = l + jnp.sum(p_c, axis=-1, keepdims=True)
            o = o + _dot(p_c.astype(BF16), vc_ref[0, 0, :, ksl].astype(BF16))
        o = o / l
        for g in range(grp):
            o_ref[:, (kvh * grp + g) * hd:(kvh * grp + g + 1) * hd] = o[g * tq:(g + 1) * tq]


def _attn(q, kn, vn, cache, nb, seq, layer):
    t = nb * seq
    tq = min(seq, 128)
    nq = seq // tq
    has_cache = cache is not None
    in_specs = [pl.BlockSpec((tq, D_MODEL), lambda b, i: (b * nq + i, 0)),
                pl.BlockSpec((seq, ATT_KV), lambda b, i: (b, 0)),
                pl.BlockSpec((seq, ATT_KV), lambda b, i: (b, 0))]
    args = [q, kn, vn]
    if has_cache:
        past = cache[0].shape[2]
        spec = pl.BlockSpec((1, 1, past, ATT_KV), lambda b, i: (b, layer, 0, 0))
        in_specs += [spec, spec]
        args += list(cache)
    return pl.pallas_call(
        functools.partial(_attn_kernel, has_cache=has_cache, tq=tq),
        grid=(nb, nq),
        in_specs=in_specs,
        out_specs=pl.BlockSpec((tq, D_MODEL), lambda b, i: (b * nq + i, 0)),
        out_shape=jax.ShapeDtypeStruct((t, D_MODEL), F32),
        compiler_params=_cparams("parallel", "parallel"),
        name="gqa",
    )(*args)


def _merge_kernel(yf_ref, yb_ref, xbc_ref, z_ref, att_ref, hf_ref, hb_ref, mo_ref,
                  g0_ref, g1_ref, g2_ref, x_ref, gate_ref, dvec_ref, sgain_ref, mgain_ref,
                  wb_ref, wo_ref, o_ref, b3_scr):
    ys = (yf_ref[...] + yb_ref[...] + dvec_ref[...] * xbc_ref[:, :D_MODEL]) * z_ref[...]
    b1 = _rms(ys) * sgain_ref[...]
    for hh in range(ML_HEADS):
        sl = slice(hh * ML_HEAD_DIM, (hh + 1) * ML_HEAD_DIM)
        b3_scr[:, sl] = _rms(hf_ref[:, sl] + hb_ref[:, sl])
    b3 = (b3_scr[...] * mgain_ref[...]) * mo_ref[...]
    merged = (g0_ref[...] * _dot(b1.astype(BF16), wb_ref[0])
              + g1_ref[...] * _dot(att_ref[...].astype(BF16), wb_ref[1])
              + g2_ref[...] * _dot(b3.astype(BF16), wb_ref[2]))
    o_ref[...] = x_ref[...] + gate_ref[0] * _dot(merged.astype(BF16), wo_ref[...])


def _merge(yf, yb, xbc, u, att, hf, hb, x, gate1, dvec, sgain, mgain, wb, wo, seq):
    t = x.shape[0]
    tm = min(t, 256)
    tok = lambda i: (i, 0)
    ucol = lambda off: pl.BlockSpec((tm, D_MODEL), lambda i: (i, off // D_MODEL))
    full = pl.BlockSpec((tm, D_MODEL), tok)
    vec = pl.BlockSpec((1, D_MODEL), lambda i: (0, 0))
    once = pl.Buffered(1)
    in_specs = [full, full, pl.BlockSpec((tm, SSD_CONV), tok), ucol(U_SZ), full, full, full, ucol(U_MO),
                ucol(U_G), ucol(U_G + D_MODEL), ucol(U_G + 2 * D_MODEL), full,
                _mod_spec(gate1.shape[0], seq, tm), vec, vec, vec,
                pl.BlockSpec((3, D_MODEL, D_MODEL), lambda i: (0, 0, 0), pipeline_mode=once),
                pl.BlockSpec((D_MODEL, D_MODEL), lambda i: (0, 0), pipeline_mode=once)]
    return pl.pallas_call(
        _merge_kernel,
        grid=(t // tm,),
        in_specs=in_specs,
        out_specs=full,
        out_shape=jax.ShapeDtypeStruct((t, D_MODEL), F32),
        scratch_shapes=[pltpu.VMEM((tm, D_MODEL), F32)],
        compiler_params=_cparams("parallel"),
        name="branch_merge",
    )(yf, yb, xbc, u, att, hf, hb, u, u, u, u, x, gate1, dvec, sgain, mgain, wb, wo)


def _ffn_kernel(x_ref, sc_ref, sh_ref, gate_ref, wi_ref, wo_ref, o_ref):
    x = x_ref[...]
    h = (_rms(x) * (1.0 + sc_ref[0]) + sh_ref[0]).astype(BF16)
    a = _dot(h, wi_ref[:, :FFN_HIDDEN])
    b = _dot(h, wi_ref[:, FFN_HIDDEN:])
    act = (a * _sigmoid(a) * b).astype(BF16)
    o_ref[...] = x + gate_ref[0] * _dot(act, wo_ref[...])


def _ffn(x, scale, shift, gate, wi, wo, seq):
    t = x.shape[0]
    tm = min(t, 256)
    nb = scale.shape[0]
    once = pl.Buffered(1)
    full = pl.BlockSpec((tm, D_MODEL), lambda i: (i, 0))
    return pl.pallas_call(
        _ffn_kernel,
        grid=(t // tm,),
        in_specs=[full, _mod_spec(nb, seq, tm), _mod_spec(nb, seq, tm), _mod_spec(nb, seq, tm),
                  pl.BlockSpec((D_MODEL, 2 * FFN_HIDDEN), lambda i: (0, 0), pipeline_mode=once),
                  pl.BlockSpec((FFN_HIDDEN, D_MODEL), lambda i: (0, 0), pipeline_mode=once)],
        out_specs=full,
        out_shape=jax.ShapeDtypeStruct((t, D_MODEL), F32),
        compiler_params=_cparams("parallel"),
        name="ffn",
    )(x, scale, shift, gate, wi, wo)


def _gate_consts():
    idx = np.arange(GATE_W)
    back = ((idx >= G_DT + SSD_HEADS) & (idx < G_LI)) | ((idx >= G_LF + ML_HEADS) & (idx < G_LF + 2 * ML_HEADS))
    r = np.arange(CHUNK)
    tlo = (r[:, None] >= r[None, :]).astype(np.float32)
    tup = (r[:, None] <= r[None, :]).astype(np.float32)
    return (jnp.asarray(back.astype(np.float32)), jnp.asarray(tlo, BF16), jnp.asarray(tup, BF16))


def _rope_tables(seq):
    pos = np.arange(seq)
    quarter = ATT_HEAD_DIM // 4
    freqs = jnp.asarray(ROPE_THETA, F32) ** (-jnp.arange(quarter, dtype=F32) / quarter)
    ang_r = jnp.asarray(pos // GRID_W, F32)[:, None] * freqs
    ang_c = jnp.asarray(pos % GRID_W, F32)[:, None] * freqs
    cos = jnp.concatenate([jnp.cos(ang_r)] * 2 + [jnp.cos(ang_c)] * 2, axis=-1)
    sin = jnp.concatenate([-jnp.sin(ang_r), jnp.sin(ang_r), -jnp.sin(ang_c), jnp.sin(ang_c)], axis=-1)
    reps = (D_MODEL + ATT_KV) // ATT_HEAD_DIM
    return jnp.tile(cos, (1, reps)), jnp.tile(sin, (1, reps))


def _split_w_in(w_in):
    sizes = (D_MODEL, D_MODEL, SSD_BC, SSD_BC, 2 * SSD_HEADS,
             D_MODEL, ATT_KV, ATT_KV,
             D_MODEL, D_MODEL, D_MODEL, D_MODEL, 4 * ML_HEADS, 3 * D_MODEL)
    offs = np.cumsum((0,) + sizes)
    return [w_in[:, :, offs[i]:offs[i + 1]] for i in range(len(sizes))]


def kernel(x_prompt, x_sample, cache_k, cache_v, state_ssd, state_ml_c, state_ml_n, state_ml_m,
           c, c_ctx, w_ada, b_ada, w_in, ssd_conv_w, ssd_conv_b, ssd_a_log, ssd_dt_bias, ssd_d,
           ssd_norm, att_q_norm, att_k_norm, ml_conv_w, ml_conv_b, ml_gate_bias, ml_norm,
           w_branch, w_out, w_ffn_in, w_ffn_out, final_norm):
    depth = w_in.shape[0]
    nbp, seqp, _ = x_prompt.shape
    nbs, seqs, _ = x_sample.shape
    past = cache_k.shape[2]

    (s_x, s_z, s_b, s_c, s_dt, a_q, a_k, a_v, m_q, m_k, m_v, m_o, m_g, g) = _split_w_in(w_in)
    w_main = jnp.concatenate([s_x, s_b, s_c, a_q, a_k, a_v, m_q, m_k, m_v, s_z, m_o, g], axis=-1).astype(BF16)
    m_g4 = m_g.reshape(depth, D_MODEL, 2, 2, ML_HEADS)
    w_small = jnp.concatenate(
        [s_dt, m_g4[:, :, :, 0].reshape(depth, D_MODEL, 2 * ML_HEADS),
         m_g4[:, :, :, 1].reshape(depth, D_MODEL, 2 * ML_HEADS),
         jnp.zeros((depth, D_MODEL, GATE_W - G_LF - 2 * ML_HEADS), F32)], axis=-1).astype(BF16)
    w_small_t = jnp.swapaxes(w_small, 1, 2)
    pad = jnp.zeros((depth, GATE_W - G_LF - 2 * ML_HEADS), F32)
    gate_bias = jnp.concatenate(
        [ssd_dt_bias.reshape(depth, 2 * SSD_HEADS), ml_gate_bias[:, :, 0].reshape(depth, 2 * ML_HEADS),
         ml_gate_bias[:, :, 1].reshape(depth, 2 * ML_HEADS), pad], axis=-1)
    a_vec = jnp.concatenate(
        [-jnp.exp(ssd_a_log.reshape(depth, 2 * SSD_HEADS)), jnp.zeros((depth, GATE_W - 2 * SSD_HEADS), F32)], axis=-1)
    d_vec = jnp.repeat(ssd_d, SSD_HEAD_DIM, axis=-1).reshape(depth, 1, D_MODEL)
    qk_gain = jnp.concatenate([jnp.tile(att_q_norm, (1, ATT_HEADS)), jnp.tile(att_k_norm, (1, ATT_KV_HEADS))], axis=-1)
    qk_gain = qk_gain.reshape(depth, 1, D_MODEL + ATT_KV)
    w_branch_b = w_branch.astype(BF16)
    w_out_b = w_out.astype(BF16)
    w_ffn_in_b = w_ffn_in.astype(BF16)
    w_ffn_out_b = w_ffn_out.astype(BF16)
    consts = _gate_consts()
    tables = _rope_tables(seqs)
    cache = (cache_k.reshape(nbs, depth, past, ATT_KV), cache_v.reshape(nbs, depth, past, ATT_KV))
    m0_wide = jnp.broadcast_to(state_ml_m[..., None], state_ml_m.shape + (ML_HEAD_DIM,))

    rows = 8 * ((1 + nbs + 7) // 8)
    cc = jnp.concatenate([c_ctx[None], c, jnp.zeros((rows - 1 - nbs, D_MODEL), F32)], axis=0)
    mod = _ada(cc, w_ada, b_ada)

    def mods(layer, lo, hi):
        m = mod[layer, lo:hi].reshape(hi - lo, 1, 6, D_MODEL)
        return [m[:, :, i] for i in range(6)]

    def layer_step(x, layer, nb, seq, mod6, ctx):
        shift1, scale1, gate1, shift2, scale2, gate2 = mod6
        h = _norm_mod(x, scale1, shift1, seq)
        u = _proj(h, w_main[layer])
        gates = _gate(h, w_small[layer], w_small_t[layer], gate_bias[layer], a_vec[layer], consts)
        xbc = _conv(u, U_CONV, SSD_CONV, ssd_conv_w[layer], ssd_conv_b[layer], nb, seq)
        mqk = _conv(u, U_MQ, 2 * D_MODEL, ml_conv_w[layer], ml_conv_b[layer], nb, seq)
        if ctx:
            yf, yb = _ssd(xbc, gates, nb, seq, state_ssd, layer)
            q, kn, vn = _qkprep(u, qk_gain[layer], tables, nb, seq, False)
            att = _attn(q, kn, vn, cache, nb, seq, layer)
            hf, hb = _mlstm(mqk, u, gates, nb, seq, (state_ml_c, state_ml_n, m0_wide[:, layer]), layer)
            new = None
        else:
            yf, yb, hst = _ssd(xbc, gates, nb, seq, None, layer)
            q, kn, vn, kf, vf = _qkprep(u, qk_gain[layer], None, nb, seq, True)
            att = _attn(q, kn, vn, None, nb, seq, layer)
            hf, hb, cst, nst, mst = _mlstm(mqk, u, gates, nb, seq, None, layer)
            new = (kf.reshape(nb, seq, ATT_KV_HEADS, ATT_HEAD_DIM), vf.reshape(nb, seq, ATT_KV_HEADS, ATT_HEAD_DIM),
                   hst, cst, nst, mst[..., 0])
        x = _merge(yf, yb, xbc, u, att, hf, hb, x, gate1, d_vec[layer],
                   ssd_norm[layer].reshape(1, D_MODEL), ml_norm[layer].reshape(1, D_MODEL),
                   w_branch_b[layer], w_out_b[layer], seq)
        x = _ffn(x, scale2, shift2, gate2, w_ffn_in_b[layer], w_ffn_out_b[layer], seq)
        return x, new

    y_p = x_prompt.reshape(nbp * seqp, D_MODEL)
    y_s = x_sample.reshape(nbs * seqs, D_MODEL)
    news = []
    for layer in range(depth):
        y_p, new = layer_step(y_p, layer, nbp, seqp, mods(layer, 0, 1), False)
        news.append(new)
        y_s, _ = layer_step(y_s, layer, nbs, seqs, mods(layer, 1, 1 + nbs), True)

    y_prompt = _norm_gain(y_p, final_norm).reshape(x_prompt.shape)
    y_sample = _norm_gain(y_s, final_norm).reshape(x_sample.shape)
    stacked = [jnp.stack([n[i] for n in news], axis=1) for i in range(6)]
    return (y_prompt, y_sample) + tuple(stacked)
```

```python
import functools

import numpy as np
import jax
import jax.numpy as jnp
from jax import lax
from jax.experimental import pallas as pl
from jax.experimental.pallas import tpu as pltpu

F32 = jnp.float32
BF16 = jnp.bfloat16

D_MODEL = 1024
GRID_W = 64
EPS = 1e-6
CONV_W = 4
CHUNK = 128
SSD_HEADS = 16
SSD_HEAD_DIM = 64
SSD_GROUPS = 4
SSD_STATE = 64
SSD_BC = SSD_GROUPS * SSD_STATE
SSD_CONV = D_MODEL + 2 * SSD_BC
ATT_HEADS = 16
ATT_HEAD_DIM = 64
ATT_KV_HEADS = 4
ATT_KV = ATT_KV_HEADS * ATT_HEAD_DIM
ATT_QKV = D_MODEL + 2 * ATT_KV
ROPE_THETA = 10000.0
ML_HEADS = 8
ML_HEAD_DIM = 128
FFN_HIDDEN = 2816
GATE_W = 128

U_CONV = 0
U_ATT = 1536
U_MQ = 3072
U_MK = 4096
U_MV = 5120
U_SZ = 6144
U_MO = 7168
U_G = 8192
U_W = 11264

G_DT = 0
G_LI = 32
G_LF = 48

VMEM_LIMIT = 48 * 1024 * 1024
NT_DIMS = (((1,), (1,)), ((), ()))
TN_DIMS = (((0,), (0,)), ((), ()))


def _cparams(*sem):
    return pltpu.CompilerParams(dimension_semantics=sem, vmem_limit_bytes=VMEM_LIMIT)


def _sigmoid(x):
    return 1.0 / (1.0 + jnp.exp(-x))


def _softplus(x):
    return jnp.maximum(x, 0.0) + jnp.log1p(jnp.exp(-jnp.abs(x)))


def _dot(a, b):
    return jnp.dot(a, b, preferred_element_type=F32)


def _dot_nt(a, b):
    return lax.dot_general(a, b, NT_DIMS, preferred_element_type=F32)


def _dot_tn(a, b):
    return lax.dot_general(a, b, TN_DIMS, preferred_element_type=F32)


def _ada_kernel(c_ref, w_ref, b_ref, o_ref):
    c = c_ref[...]
    s = (c * _sigmoid(c)).astype(BF16)
    o_ref[0] = _dot(s, w_ref[0].astype(BF16)) + b_ref[0]


def _ada(cc, w_ada, b_ada):
    depth, _, n6 = w_ada.shape
    rows = cc.shape[0]
    tn = 512
    return pl.pallas_call(
        _ada_kernel,
        grid=(depth, n6 // tn),
        in_specs=[pl.BlockSpec((rows, D_MODEL), lambda l, j: (0, 0)),
                  pl.BlockSpec((1, D_MODEL, tn), lambda l, j: (l, 0, j)),
                  pl.BlockSpec((1, 1, tn), lambda l, j: (l, 0, j))],
        out_specs=pl.BlockSpec((1, rows, tn), lambda l, j: (l, 0, j)),
        out_shape=jax.ShapeDtypeStruct((depth, rows, n6), F32),
        compiler_params=_cparams("parallel", "parallel"),
        name="ada_mod",
    )(cc, w_ada, b_ada.reshape(depth, 1, n6))


def _rms(x):
    return x * lax.rsqrt(jnp.mean(x * x, axis=-1, keepdims=True) + EPS)


def _norm_mod_kernel(x_ref, sc_ref, sh_ref, o_ref):
    o_ref[...] = (_rms(x_ref[...]) * (1.0 + sc_ref[0]) + sh_ref[0]).astype(o_ref.dtype)


def _norm_gain_kernel(x_ref, g_ref, o_ref):
    o_ref[...] = _rms(x_ref[...]) * g_ref[...]


def _mod_spec(nb, seq, tm):
    if nb == 1:
        return pl.BlockSpec((1, 1, D_MODEL), lambda i: (0, 0, 0))
    return pl.BlockSpec((1, 1, D_MODEL), lambda i: ((i * tm) // seq, 0, 0))


def _norm_mod(x, scale, shift, seq):
    t = x.shape[0]
    tm = min(t, 512)
    nb = scale.shape[0]
    return pl.pallas_call(
        _norm_mod_kernel,
        grid=(t // tm,),
        in_specs=[pl.BlockSpec((tm, D_MODEL), lambda i: (i, 0)),
                  _mod_spec(nb, seq, tm), _mod_spec(nb, seq, tm)],
        out_specs=pl.BlockSpec((tm, D_MODEL), lambda i: (i, 0)),
        out_shape=jax.ShapeDtypeStruct((t, D_MODEL), BF16),
        compiler_params=_cparams("parallel"),
        name="norm_mod",
    )(x, scale, shift)


def _norm_gain(x, gain):
    t = x.shape[0]
    tm = min(t, 512)
    return pl.pallas_call(
        _norm_gain_kernel,
        grid=(t // tm,),
        in_specs=[pl.BlockSpec((tm, D_MODEL), lambda i: (i, 0)),
                  pl.BlockSpec((1, D_MODEL), lambda i: (0, 0))],
        out_specs=pl.BlockSpec((tm, D_MODEL), lambda i: (i, 0)),
        out_shape=jax.ShapeDtypeStruct((t, D_MODEL), F32),
        compiler_params=_cparams("parallel"),
        name="final_norm",
    )(x, gain.reshape(1, D_MODEL))


def _proj_kernel(a_ref, w_ref, o_ref, *, silu_lo, sig_lo):
    j = pl.program_id(1)
    acc = _dot(a_ref[...], w_ref[...])

    @pl.when(j < silu_lo)
    def _():
        o_ref[...] = acc

    @pl.when(jnp.logical_and(j >= silu_lo, j < sig_lo))
    def _():
        o_ref[...] = acc * _sigmoid(acc)

    @pl.when(j >= sig_lo)
    def _():
        o_ref[...] = _sigmoid(acc)


def _proj(h, w_main):
    t = h.shape[0]
    tm = min(t, 1024)
    tn = 1024
    kern = functools.partial(_proj_kernel, silu_lo=U_SZ // tn, sig_lo=U_MO // tn)
    return pl.pallas_call(
        kern,
        grid=(t // tm, U_W // tn),
        in_specs=[pl.BlockSpec((tm, D_MODEL), lambda i, j: (i, 0)),
                  pl.BlockSpec((D_MODEL, tn), lambda i, j: (0, j))],
        out_specs=pl.BlockSpec((tm, tn), lambda i, j: (i, j)),
        out_shape=jax.ShapeDtypeStruct((t, U_W), F32),
        compiler_params=_cparams("parallel", "parallel"),
        name="in_proj",
    )(h, w_main)


def _split3(x):
    hi = x.astype(BF16)
    r1 = x - hi.astype(F32)
    mid = r1.astype(BF16)
    lo = (r1 - mid.astype(F32)).astype(BF16)
    return hi, mid, lo


def _tri_left(tri, x):
    hi, mid, lo = _split3(x)
    return _dot(tri, hi) + _dot(tri, mid) + _dot(tri, lo)


def _tri_right(x, tri):
    hi, mid, lo = _split3(x)
    return _dot(hi, tri) + _dot(mid, tri) + _dot(lo, tri)


def _gate_vals(u, idx, a):
    sp = _softplus(u)
    ls = -_softplus(-u)
    val = jnp.where(idx < G_LI, sp, jnp.where(idx < G_LF, u, jnp.where(idx < G_LF + 16, ls, 0.0)))
    cin = jnp.where(idx < G_LI, sp * a, jnp.where(jnp.logical_and(idx >= G_LF, idx < G_LF + 16), ls, 0.0))
    return val, cin


def _gate_kernel(h_ref, ws_ref, wst_ref, b_ref, bt_ref, a_ref, at_ref, dm_ref, dmt_ref,
                 tlo_ref, tup_ref, val_ref, cum_ref, valt_ref, cumt_ref):
    h = h_ref[...]
    tlo = tlo_ref[...]
    tup = tup_ref[...]
    shape = (CHUNK, GATE_W)
    u = _dot(h, ws_ref[...]) + b_ref[...]
    val, cin = _gate_vals(u, lax.broadcasted_iota(jnp.int32, shape, 1), a_ref[...])
    val_ref[...] = val
    cum_ref[...] = jnp.where(dm_ref[...] > 0.5, _tri_left(tup, cin), _tri_left(tlo, cin))
    ut = _dot_nt(wst_ref[...], h) + bt_ref[...]
    valt, cint = _gate_vals(ut, lax.broadcasted_iota(jnp.int32, shape, 0), at_ref[...])
    valt_ref[...] = valt
    cumt_ref[...] = jnp.where(dmt_ref[...] > 0.5, _tri_right(cint, tlo), _tri_right(cint, tup))


def _gate(h, ws, wst, bias, avec, consts):
    t = h.shape[0]
    dmask, tlo, tup = consts
    row = lambda i: (i, 0)
    col = lambda i: (0, i)
    fix = lambda i: (0, 0)
    tm_out = jax.ShapeDtypeStruct((t, GATE_W), F32)
    fm_out = jax.ShapeDtypeStruct((GATE_W, t), F32)
    return pl.pallas_call(
        _gate_kernel,
        grid=(t // CHUNK,),
        in_specs=[pl.BlockSpec((CHUNK, D_MODEL), row),
                  pl.BlockSpec((D_MODEL, GATE_W), fix),
                  pl.BlockSpec((GATE_W, D_MODEL), fix),
                  pl.BlockSpec((1, GATE_W), fix), pl.BlockSpec((GATE_W, 1), fix),
                  pl.BlockSpec((1, GATE_W), fix), pl.BlockSpec((GATE_W, 1), fix),
                  pl.BlockSpec((1, GATE_W), fix), pl.BlockSpec((GATE_W, 1), fix),
                  pl.BlockSpec((CHUNK, CHUNK), fix), pl.BlockSpec((CHUNK, CHUNK), fix)],
        out_specs=[pl.BlockSpec((CHUNK, GATE_W), row), pl.BlockSpec((CHUNK, GATE_W), row),
                   pl.BlockSpec((GATE_W, CHUNK), col), pl.BlockSpec((GATE_W, CHUNK), col)],
        out_shape=[tm_out, tm_out, fm_out, fm_out],
        compiler_params=_cparams("parallel"),
        name="gate_prep",
    )(h, ws, wst, bias.reshape(1, GATE_W), bias.reshape(GATE_W, 1),
      avec.reshape(1, GATE_W), avec.reshape(GATE_W, 1),
      dmask.reshape(1, GATE_W), dmask.reshape(GATE_W, 1), tlo, tup)


def _conv_kernel(x_ref, w_ref, b_ref, o_ref):
    x = x_ref[...]
    seq = x.shape[0]
    row = lax.broadcasted_iota(jnp.int32, x.shape, 0)
    acc = x * w_ref[1:2, :] + b_ref[...]
    acc = acc + jnp.where(row >= 1, pltpu.roll(x, 1, 0), 0.0) * w_ref[0:1, :]
    acc = acc + jnp.where(row < seq - 1, pltpu.roll(x, seq - 1, 0), 0.0) * w_ref[2:3, :]
    acc = acc + jnp.where(row < seq - 2, pltpu.roll(x, seq - 2, 0), 0.0) * w_ref[3:4, :]
    o_ref[...] = acc * _sigmoid(acc)


def _conv(u, col0, width, w, b, nb, seq):
    tc = 256
    c0 = col0 // tc
    return pl.pallas_call(
        _conv_kernel,
        grid=(nb, width // tc),
        in_specs=[pl.BlockSpec((seq, tc), lambda i, j: (i, c0 + j)),
                  pl.BlockSpec((CONV_W, tc), lambda i, j: (0, j)),
                  pl.BlockSpec((1, tc), lambda i, j: (0, j))],
        out_specs=pl.BlockSpec((seq, tc), lambda i, j: (i, j)),
        out_shape=jax.ShapeDtypeStruct((nb * seq, width), F32),
        compiler_params=_cparams("parallel", "parallel"),
        name="dwconv_silu",
    )(u, w, b.reshape(1, width))


def _tri_mask(fwd):
    row = lax.broadcasted_iota(jnp.int32, (CHUNK, CHUNK), 0)
    col = lax.broadcasted_iota(jnp.int32, (CHUNK, CHUNK), 1)
    return row >= col if fwd else row <= col


def _lane_expand(x, e):
    hi, mid, _ = _split3(x)
    return _dot(jnp.concatenate([hi, mid], axis=1), e)


SSD_GROUP_W = (SSD_HEADS // SSD_GROUPS) * SSD_HEAD_DIM


def _ssd_dir(d, x_ref, val_ref, cum_ref, valt_ref, cumt_ref, e64_ref, e128_ref, eye_ref, y_ref, st_scr):
    mask = _tri_mask(d == 0)
    end = CHUNK - 1 if d == 0 else 0
    x = x_ref[:, :D_MODEL]
    bb = x_ref[:, D_MODEL:D_MODEL + SSD_BC].astype(BF16)
    cm = x_ref[:, D_MODEL + SSD_BC:]
    lane_c = lax.broadcasted_iota(jnp.int32, cm.shape, 1)
    lane_x = lax.broadcasted_iota(jnp.int32, (CHUNK, 2 * SSD_HEAD_DIM), 1)
    a128 = _lane_expand(cum_ref[...], e128_ref[d])
    a64 = jnp.concatenate(
        [jnp.where(lane_x < SSD_HEAD_DIM, a128[:, (2 * p) * CHUNK:(2 * p + 1) * CHUNK],
                   a128[:, (2 * p + 1) * CHUNK:(2 * p + 2) * CHUNK]) for p in range(SSD_HEADS // 2)], axis=1)
    dt64 = _dot(val_ref[...].astype(BF16), e64_ref[d])
    dt_r = valt_ref[...]
    ac_r = cumt_ref[...]
    st = st_scr[d]
    inter = _dot(cm.astype(BF16), st.astype(BF16))
    ea = jnp.exp(a64)
    c_stack = jnp.concatenate(
        [jnp.where(jnp.logical_and(lane_c >= g * SSD_STATE, lane_c < (g + 1) * SSD_STATE), cm, 0.0).astype(BF16)
         for g in range(SSD_GROUPS)], axis=0)
    cbt_all = _dot_nt(c_stack, bb)
    for g in range(SSD_GROUPS):
        cbt = cbt_all[g * CHUNK:(g + 1) * CHUNK]
        for pr in range(2):
            h0 = g * (SSD_HEADS // SSD_GROUPS) + 2 * pr
            parts = []
            for hh in (h0, h0 + 1):
                lane = G_DT + d * SSD_HEADS + hh
                seg = jnp.where(mask, a128[:, hh * CHUNK:(hh + 1) * CHUNK] - ac_r[lane:lane + 1, :], -jnp.inf)
                parts.append((cbt * jnp.exp(seg) * dt_r[lane:lane + 1, :]).astype(BF16))
            cs = slice(h0 * SSD_HEAD_DIM, (h0 + 2) * SSD_HEAD_DIM)
            xp = x[:, cs]
            xbd = jnp.concatenate([jnp.where(lane_x < SSD_HEAD_DIM, xp, 0.0),
                                   jnp.where(lane_x >= SSD_HEAD_DIM, xp, 0.0)], axis=0).astype(BF16)
            y_ref[:, cs] = _dot(jnp.concatenate(parts, axis=1), xbd) + ea[:, cs] * inter[:, cs]
    tot = a64[end:end + 1, :]
    xw = (x * (jnp.exp(tot - a64) * dt64)).astype(BF16)
    bt = _dot_nt(eye_ref[...], bb).astype(BF16)
    decay = jnp.exp(tot)
    for g in range(SSD_GROUPS):
        rs = slice(g * SSD_STATE, (g + 1) * SSD_STATE)
        cs = slice(g * SSD_GROUP_W, (g + 1) * SSD_GROUP_W)
        st_scr[d, rs, cs] = decay[:, cs] * st[rs, cs] + _dot(bt[rs, :], xw[:, cs])


def _ssd_kernel(*refs, has_h0, emit_state):
    xf_ref, xb_ref = refs[0:2]
    gf = refs[2:6]
    gb = refs[6:10]
    consts = refs[10:13]
    k = 13
    if has_h0:
        h0_ref = refs[k]
        k += 1
    yf_ref, yb_ref = refs[k:k + 2]
    k += 2
    if emit_state:
        hout_ref = refs[k]
        k += 1
    st_scr = refs[k]
    c = pl.program_id(1)

    @pl.when(c == 0)
    def _():
        st_scr[...] = jnp.zeros(st_scr.shape, F32)
        if has_h0:
            for d in range(2):
                for g in range(SSD_GROUPS):
                    st_scr[d, g * SSD_STATE:(g + 1) * SSD_STATE,
                           g * SSD_GROUP_W:(g + 1) * SSD_GROUP_W] = h0_ref[0, 0, d, g]

    _ssd_dir(0, xf_ref, *gf, *consts, yf_ref, st_scr)
    _ssd_dir(1, xb_ref, *gb, *consts, yb_ref, st_scr)

    if emit_state:
        @pl.when(c == pl.num_programs(1) - 1)
        def _():
            for d in range(2):
                for g in range(SSD_GROUPS):
                    hout_ref[0, d, g] = st_scr[d, g * SSD_STATE:(g + 1) * SSD_STATE,
                                               g * SSD_GROUP_W:(g + 1) * SSD_GROUP_W]


def _chunk_specs(nc, width):
    fwd = pl.BlockSpec((CHUNK, width), lambda b, c: (b * nc + c, 0))
    bwd = pl.BlockSpec((CHUNK, width), lambda b, c: (b * nc + nc - 1 - c, 0))
    return fwd, bwd


def _gate_specs(nc):
    tf, tb = _chunk_specs(nc, GATE_W)
    ff = pl.BlockSpec((GATE_W, CHUNK), lambda b, c: (0, b * nc + c))
    fb = pl.BlockSpec((GATE_W, CHUNK), lambda b, c: (0, b * nc + nc - 1 - c))
    return [tf, tf, ff, ff], [tb, tb, fb, fb]


def _const_spec(arr):
    nd = arr.ndim
    return pl.BlockSpec(arr.shape, lambda b, c: (0,) * nd)


def _ssd(xbc, gates, consts, nb, seq, state, layer):
    nc = seq // CHUNK
    t = nb * seq
    has_h0 = state is not None
    xf, xb = _chunk_specs(nc, SSD_CONV)
    gfs, gbs = _gate_specs(nc)
    yf, yb = _chunk_specs(nc, D_MODEL)
    st_shape = (2, SSD_GROUPS, SSD_STATE, SSD_GROUP_W)
    in_specs = [xf, xb] + gfs + gbs + [_const_spec(a) for a in consts]
    args = [xbc, xbc] + list(gates) + list(gates) + list(consts)
    if has_h0:
        in_specs.append(pl.BlockSpec((1, 1) + st_shape, lambda b, c: (b, layer, 0, 0, 0, 0)))
        args.append(state)
    out_specs = [yf, yb]
    out_shape = [jax.ShapeDtypeStruct((t, D_MODEL), F32)] * 2
    if not has_h0:
        out_specs.append(pl.BlockSpec((1,) + st_shape, lambda b, c: (b, 0, 0, 0, 0)))
        out_shape.append(jax.ShapeDtypeStruct((nb,) + st_shape, F32))
    return pl.pallas_call(
        functools.partial(_ssd_kernel, has_h0=has_h0, emit_state=not has_h0),
        grid=(nb, nc),
        in_specs=in_specs,
        out_specs=out_specs,
        out_shape=out_shape,
        scratch_shapes=[pltpu.VMEM((2, SSD_BC, D_MODEL), F32)],
        compiler_params=_cparams("parallel", "arbitrary"),
        name="ssd_scan",
    )(*args)


def _ml_dir(d, q_ref, k_ref, v_ref, val_ref, cum_ref, valt_ref, cumt_ref, eb_ref, ei_ref, eye_ref,
            y_ref, cn_scr, m_scr):
    mask = _tri_mask(d == 0)
    end = CHUNK - 1 if d == 0 else 0
    b128 = _lane_expand(cum_ref[...], eb_ref[d])
    i128 = _lane_expand(val_ref[...], ei_ref[d])
    li_r = valt_ref[...]
    bc_r = cumt_ref[...]
    m_row = m_scr[d]
    q = q_ref[...]
    k = k_ref[...] * (ML_HEAD_DIM ** -0.5)
    v = v_ref[...]
    b_end = b128[end:end + 1, :]
    wj = b_end - b128 + i128
    m_new = jnp.maximum(b_end + m_row, jnp.max(wj, axis=0, keepdims=True))
    kw = (k * jnp.exp(wj - m_new)).astype(BF16)
    decay = jnp.exp(b_end + m_row - m_new)
    bm = b128 + m_row
    ones = jnp.ones((CHUNK, ML_HEAD_DIM), BF16)
    for hh in range(ML_HEADS):
        sl = slice(hh * ML_HEAD_DIM, (hh + 1) * ML_HEAD_DIM)
        li = G_LI + d * ML_HEADS + hh
        lf = G_LF + d * ML_HEADS + hh
        dmat = jnp.where(mask, b128[:, sl] - bc_r[lf:lf + 1, :] + li_r[li:li + 1, :], -jnp.inf)
        m_t = jnp.maximum(bm[:, sl], jnp.max(dmat, axis=1, keepdims=True))
        qb = q[:, sl].astype(BF16)
        s = _dot_nt(qb, k[:, sl].astype(BF16)) * jnp.exp(dmat - m_t)
        inter = jnp.exp(bm[:, sl] - m_t)
        v1 = jnp.concatenate([v[:, sl].astype(BF16), ones], axis=1)
        cn = cn_scr[d, hh]
        kwt = _dot_nt(eye_ref[...], kw[:, sl]).astype(BF16)
        sv = _dot(jnp.concatenate([s.astype(BF16), kwt], axis=0), v1)
        r = sv[:CHUNK] + jnp.concatenate([inter, inter], axis=1) * _dot(qb, cn.astype(BF16))
        y_ref[:, sl] = r[:, :ML_HEAD_DIM] / jnp.maximum(jnp.abs(r[:, ML_HEAD_DIM:]), jnp.exp(-m_t))
        dch = decay[:, sl]
        cn_scr[d, hh] = jnp.concatenate([dch, dch], axis=1) * cn + sv[CHUNK:]
    m_scr[d] = m_new


def _ml_kernel(*refs, has_init, emit_state):
    qf_ref, kf_ref, vf_ref, qb_ref, kb_ref, vb_ref = refs[0:6]
    gf = refs[6:10]
    gb = refs[10:14]
    consts = refs[14:17]
    k = 17
    if has_init:
        cn0_ref, m0_ref = refs[k:k + 2]
        k += 2
    yf_ref, yb_ref = refs[k:k + 2]
    k += 2
    if emit_state:
        cnout_ref, mout_ref = refs[k:k + 2]
        k += 2
    cn_scr, m_scr = refs[k:k + 2]
    c = pl.program_id(1)

    @pl.when(c == 0)
    def _():
        if has_init:
            cn_scr[...] = cn0_ref[0, 0]
            m_scr[...] = m0_ref[0, 0]
        else:
            cn_scr[...] = jnp.zeros(cn_scr.shape, F32)
            m_scr[...] = jnp.zeros(m_scr.shape, F32)

    _ml_dir(0, qf_ref, kf_ref, vf_ref, *gf, *consts, yf_ref, cn_scr, m_scr)
    _ml_dir(1, qb_ref, kb_ref, vb_ref, *gb, *consts, yb_ref, cn_scr, m_scr)

    if emit_state:
        @pl.when(c == pl.num_programs(1) - 1)
        def _():
            cnout_ref[0] = cn_scr[...]
            mout_ref[0] = m_scr[...]


def _mlstm(qk, u, gates, consts, nb, seq, states, layer):
    nc = seq // CHUNK
    t = nb * seq
    has_init = states is not None
    mvb = U_MV // D_MODEL

    def tok(col, rev):
        if rev:
            return pl.BlockSpec((CHUNK, D_MODEL), lambda b, c: (b * nc + nc - 1 - c, col))
        return pl.BlockSpec((CHUNK, D_MODEL), lambda b, c: (b * nc + c, col))

    gfs, gbs = _gate_specs(nc)
    in_specs = ([tok(0, False), tok(1, False), tok(mvb, False),
                 tok(0, True), tok(1, True), tok(mvb, True)] + gfs + gbs
                + [_const_spec(a) for a in consts])
    args = [qk, qk, u, qk, qk, u] + list(gates) + list(gates) + list(consts)
    cn_shape = (2, ML_HEADS, ML_HEAD_DIM, 2 * ML_HEAD_DIM)
    m_shape = (2, 1, D_MODEL)
    if has_init:
        in_specs += [pl.BlockSpec((1, 1) + cn_shape, lambda b, c: (b, layer, 0, 0, 0, 0)),
                     pl.BlockSpec((1, 1) + m_shape, lambda b, c: (b, layer, 0, 0, 0))]
        args += list(states)
    out_specs = [tok(0, False), tok(0, True)]
    out_shape = [jax.ShapeDtypeStruct((t, D_MODEL), F32)] * 2
    if not has_init:
        out_specs += [pl.BlockSpec((1,) + cn_shape, lambda b, c: (b, 0, 0, 0, 0)),
                      pl.BlockSpec((1,) + m_shape, lambda b, c: (b, 0, 0, 0))]
        out_shape += [jax.ShapeDtypeStruct((nb,) + cn_shape, F32),
                      jax.ShapeDtypeStruct((nb,) + m_shape, F32)]
    return pl.pallas_call(
        functools.partial(_ml_kernel, has_init=has_init, emit_state=not has_init),
        grid=(nb, nc),
        in_specs=in_specs,
        out_specs=out_specs,
        out_shape=out_shape,
        scratch_shapes=[pltpu.VMEM(cn_shape, F32), pltpu.VMEM(m_shape, F32)],
        compiler_params=_cparams("parallel", "arbitrary"),
        name="mlstm_scan",
    )(*args)


def _qkprep_kernel(*refs, rope, emit_f32):
    u_ref, gain_ref = refs[0:2]
    k = 2
    if rope:
        cos_ref, sin_ref = refs[k:k + 2]
        k += 2
    q_ref, k_ref, v_ref = refs[k:k + 3]
    k += 3
    if emit_f32:
        kf_ref, vf_ref = refs[k:k + 2]
        k += 2
    xn_scr = refs[k]
    nqk = D_MODEL + ATT_KV
    for hh in range(nqk // ATT_HEAD_DIM):
        sl = slice(hh * ATT_HEAD_DIM, (hh + 1) * ATT_HEAD_DIM)
        xn_scr[:, sl] = _rms(u_ref[:, sl])
    xn = xn_scr[...] * gain_ref[...]
    v = u_ref[:, nqk:nqk + ATT_KV]
    if emit_f32:
        kf_ref[...] = xn[:, D_MODEL:nqk]
        vf_ref[...] = v
    if rope:
        lane = lax.broadcasted_iota(jnp.int32, xn.shape, 1)
        quarter = ATT_HEAD_DIM // 4
        first = jnp.bitwise_and(lane, 2 * quarter - 1) < quarter
        swapped = jnp.where(first, pltpu.roll(xn, nqk - quarter, 1), pltpu.roll(xn, quarter, 1))
        xn = xn * cos_ref[...] + swapped * sin_ref[...]
    q_ref[...] = (xn[:, :D_MODEL] * (ATT_HEAD_DIM ** -0.5)).astype(BF16)
    k_ref[...] = xn[:, D_MODEL:nqk].astype(BF16)
    v_ref[...] = v.astype(BF16)


def _qkprep(u, gain, tables, nb, seq, emit_f32):
    t = nb * seq
    tl = min(seq, 256)
    nl = seq // tl
    nqk = D_MODEL + ATT_KV
    rope = tables is not None
    ub = U_ATT // ATT_QKV
    tok = lambda w: pl.BlockSpec((tl, w), lambda i, b: (b * nl + i, 0))
    in_specs = [pl.BlockSpec((tl, ATT_QKV), lambda i, b: (b * nl + i, ub)),
                pl.BlockSpec((1, nqk), lambda i, b: (0, 0))]
    args = [u, gain]
    if rope:
        in_specs += [pl.BlockSpec((tl, nqk), lambda i, b: (i, 0))] * 2
        args += list(tables)
    out_specs = [tok(D_MODEL), tok(ATT_KV), tok(ATT_KV)]
    out_shape = [jax.ShapeDtypeStruct((t, D_MODEL), BF16),
                 jax.ShapeDtypeStruct((t, ATT_KV), BF16),
                 jax.ShapeDtypeStruct((t, ATT_KV), BF16)]
    if emit_f32:
        out_specs += [tok(ATT_KV), tok(ATT_KV)]
        out_shape += [jax.ShapeDtypeStruct((t, ATT_KV), F32)] * 2
    return pl.pallas_call(
        functools.partial(_qkprep_kernel, rope=rope, emit_f32=emit_f32),
        grid=(nl, nb),
        in_specs=in_specs,
        out_specs=out_specs,
        out_shape=out_shape,
        scratch_shapes=[pltpu.VMEM((tl, nqk), F32)],
        compiler_params=_cparams("parallel", "parallel"),
        name="qk_prep",
    )(*args)


def _attn_kernel(*refs, has_cache, tq):
    q_ref, kn_ref, vn_ref = refs[0:3]
    k = 3
    if has_cache:
        kc_ref, vc_ref = refs[k:k + 2]
        k += 2
    o_ref = refs[k]
    grp = ATT_HEADS // ATT_KV_HEADS
    hd = ATT_HEAD_DIM
    for kvh in range(ATT_KV_HEADS):
        ksl = slice(kvh * hd, (kvh + 1) * hd)
        qs = jnp.concatenate(
            [q_ref[:, (kvh * grp + g) * hd:(kvh * grp + g + 1) * hd] for g in range(grp)], axis=0)
        s_n = _dot_nt(qs, kn_ref[:, ksl])
        m = jnp.max(s_n, axis=-1, keepdims=True)
        if has_cache:
            s_c = _dot_nt(qs, kc_ref[0, 0, :, ksl].astype(BF16))
            m = jnp.maximum(m, jnp.max(s_c, axis=-1, keepdims=True))
        p_n = jnp.exp(s_n - m)
        l = jnp.sum(p_n, axis=-1, keepdims=True)
        o = _dot(p_n.astype(BF16), vn_ref[:, ksl])
        if has_cache:
            p_c = jnp.exp(s_c - m)
            l = l + jnp.sum(p_c, axis=-1, keepdims=True)
            o = o + _dot(p_c.astype(BF16), vc_ref[0, 0, :, ksl].astype(BF16))
        o = o / l
        for g in range(grp):
            o_ref[:, (kvh * grp + g) * hd:(kvh * grp + g + 1) * hd] = o[g * tq:(g + 1) * tq]


def _attn(q, kn, vn, cache, nb, seq, layer):
    t = nb * seq
    tq = min(seq, 128)
    nq = seq // tq
    has_cache = cache is not None
    in_specs = [pl.BlockSpec((tq, D_MODEL), lambda b, i: (b * nq + i, 0)),
                pl.BlockSpec((seq, ATT_KV), lambda b, i: (b, 0)),
                pl.BlockSpec((seq, ATT_KV), lambda b, i: (b, 0))]
    args = [q, kn, vn]
    if has_cache:
        past = cache[0].shape[2]
        spec = pl.BlockSpec((1, 1, past, ATT_KV), lambda b, i: (b, layer, 0, 0))
        in_specs += [spec, spec]
        args += list(cache)
    return pl.pallas_call(
        functools.partial(_attn_kernel, has_cache=has_cache, tq=tq),
        grid=(nb, nq),
        in_specs=in_specs,
        out_specs=pl.BlockSpec((tq, D_MODEL), lambda b, i: (b * nq + i, 0)),
        out_shape=jax.ShapeDtypeStruct((t, D_MODEL), F32),
        compiler_params=_cparams("parallel", "parallel"),
        name="gqa",
    )(*args)


def _merge_kernel(yf_ref, yb_ref, xbc_ref, z_ref, att_ref, hf_ref, hb_ref, mo_ref,
                  g0_ref, g1_ref, g2_ref, x_ref, gate_ref, dvec_ref, sgain_ref, mgain_ref,
                  wb_ref, wo_ref, o_ref, b3_scr):
    ys = (yf_ref[...] + yb_ref[...] + dvec_ref[...] * xbc_ref[:, :D_MODEL]) * z_ref[...]
    b1 = _rms(ys) * sgain_ref[...]
    for hh in range(ML_HEADS):
        sl = slice(hh * ML_HEAD_DIM, (hh + 1) * ML_HEAD_DIM)
        b3_scr[:, sl] = _rms(hf_ref[:, sl] + hb_ref[:, sl])
    b3 = (b3_scr[...] * mgain_ref[...]) * mo_ref[...]
    merged = (g0_ref[...] * _dot(b1.astype(BF16), wb_ref[0])
              + g1_ref[...] * _dot(att_ref[...].astype(BF16), wb_ref[1])
              + g2_ref[...] * _dot(b3.astype(BF16), wb_ref[2]))
    o_ref[...] = x_ref[...] + gate_ref[0] * _dot(merged.astype(BF16), wo_ref[...])


def _merge(yf, yb, xbc, u, att, hf, hb, x, gate1, dvec, sgain, mgain, wb, wo, seq):
    t = x.shape[0]
    tm = min(t, 256)
    tok = lambda i: (i, 0)
    ucol = lambda off: pl.BlockSpec((tm, D_MODEL), lambda i: (i, off // D_MODEL))
    full = pl.BlockSpec((tm, D_MODEL), tok)
    vec = pl.BlockSpec((1, D_MODEL), lambda i: (0, 0))
    once = pl.Buffered(1)
    in_specs = [full, full, pl.BlockSpec((tm, SSD_CONV), tok), ucol(U_SZ), full, full, full, ucol(U_MO),
                ucol(U_G), ucol(U_G + D_MODEL), ucol(U_G + 2 * D_MODEL), full,
                _mod_spec(gate1.shape[0], seq, tm), vec, vec, vec,
                pl.BlockSpec((3, D_MODEL, D_MODEL), lambda i: (0, 0, 0), pipeline_mode=once),
                pl.BlockSpec((D_MODEL, D_MODEL), lambda i: (0, 0), pipeline_mode=once)]
    return pl.pallas_call(
        _merge_kernel,
        grid=(t // tm,),
        in_specs=in_specs,
        out_specs=full,
        out_shape=jax.ShapeDtypeStruct((t, D_MODEL), F32),
        scratch_shapes=[pltpu.VMEM((tm, D_MODEL), F32)],
        compiler_params=_cparams("parallel"),
        name="branch_merge",
    )(yf, yb, xbc, u, att, hf, hb, u, u, u, u, x, gate1, dvec, sgain, mgain, wb, wo)


def _ffn_kernel(x_ref, sc_ref, sh_ref, gate_ref, wi_ref, wo_ref, o_ref):
    x = x_ref[...]
    h = (_rms(x) * (1.0 + sc_ref[0]) + sh_ref[0]).astype(BF16)
    a = _dot(h, wi_ref[:, :FFN_HIDDEN])
    b = _dot(h, wi_ref[:, FFN_HIDDEN:])
    act = (a * _sigmoid(a) * b).astype(BF16)
    o_ref[...] = x + gate_ref[0] * _dot(act, wo_ref[...])


def _ffn(x, scale, shift, gate, wi, wo, seq):
    t = x.shape[0]
    tm = min(t, 256)
    nb = scale.shape[0]
    once = pl.Buffered(1)
    full = pl.BlockSpec((tm, D_MODEL), lambda i: (i, 0))
    return pl.pallas_call(
        _ffn_kernel,
        grid=(t // tm,),
        in_specs=[full, _mod_spec(nb, seq, tm), _mod_spec(nb, seq, tm), _mod_spec(nb, seq, tm),
                  pl.BlockSpec((D_MODEL, 2 * FFN_HIDDEN), lambda i: (0, 0), pipeline_mode=once),
                  pl.BlockSpec((FFN_HIDDEN, D_MODEL), lambda i: (0, 0), pipeline_mode=once)],
        out_specs=full,
        out_shape=jax.ShapeDtypeStruct((t, D_MODEL), F32),
        compiler_params=_cparams("parallel"),
        name="ffn",
    )(x, scale, shift, gate, wi, wo)


def _gate_consts():
    idx = np.arange(GATE_W)
    back = ((idx >= G_DT + SSD_HEADS) & (idx < G_LI)) | ((idx >= G_LF + ML_HEADS) & (idx < G_LF + 2 * ML_HEADS))
    r = np.arange(CHUNK)
    tlo = (r[:, None] >= r[None, :]).astype(np.float32)
    tup = (r[:, None] <= r[None, :]).astype(np.float32)
    return (jnp.asarray(back.astype(np.float32)), jnp.asarray(tlo, BF16), jnp.asarray(tup, BF16))


def _scan_consts():
    def expand(first, heads, width, parts):
        e = np.zeros((2, parts * GATE_W, heads * width), np.float32)
        for d in range(2):
            for h in range(heads):
                for part in range(parts):
                    e[d, part * GATE_W + first + d * heads + h, h * width:(h + 1) * width] = 1.0
        return jnp.asarray(e, BF16)

    ssd = (expand(G_DT, SSD_HEADS, SSD_HEAD_DIM, 1), expand(G_DT, SSD_HEADS, CHUNK, 2),
           jnp.asarray(np.eye(SSD_BC, dtype=np.float32), BF16))
    ml = (expand(G_LF, ML_HEADS, ML_HEAD_DIM, 2), expand(G_LI, ML_HEADS, ML_HEAD_DIM, 2),
          jnp.asarray(np.eye(CHUNK, dtype=np.float32), BF16))
    return ssd, ml


def _ssd_state_to_compact(s):
    lead = s.shape[:-3]
    rep = SSD_HEADS // SSD_GROUPS
    s = s.reshape(lead + (SSD_GROUPS, rep, SSD_HEAD_DIM, SSD_STATE))
    nd = len(lead)
    s = jnp.transpose(s, tuple(range(nd)) + (nd, nd + 3, nd + 1, nd + 2))
    return s.reshape(lead + (SSD_GROUPS, SSD_STATE, SSD_GROUP_W))


def _ssd_state_from_compact(s):
    lead = s.shape[:-3]
    rep = SSD_HEADS // SSD_GROUPS
    s = s.reshape(lead + (SSD_GROUPS, SSD_STATE, rep, SSD_HEAD_DIM))
    nd = len(lead)
    s = jnp.transpose(s, tuple(range(nd)) + (nd, nd + 2, nd + 3, nd + 1))
    return s.reshape(lead + (SSD_HEADS, SSD_HEAD_DIM, SSD_STATE))


def _rope_tables(seq):
    pos = np.arange(seq)
    quarter = ATT_HEAD_DIM // 4
    freqs = jnp.asarray(ROPE_THETA, F32) ** (-jnp.arange(quarter, dtype=F32) / quarter)
    ang_r = jnp.asarray(pos // GRID_W, F32)[:, None] * freqs
    ang_c = jnp.asarray(pos % GRID_W, F32)[:, None] * freqs
    cos = jnp.concatenate([jnp.cos(ang_r)] * 2 + [jnp.cos(ang_c)] * 2, axis=-1)
    sin = jnp.concatenate([-jnp.sin(ang_r), jnp.sin(ang_r), -jnp.sin(ang_c), jnp.sin(ang_c)], axis=-1)
    reps = (D_MODEL + ATT_KV) // ATT_HEAD_DIM
    return jnp.tile(cos, (1, reps)), jnp.tile(sin, (1, reps))


def _split_w_in(w_in):
    sizes = (D_MODEL, D_MODEL, SSD_BC, SSD_BC, 2 * SSD_HEADS,
             D_MODEL, ATT_KV, ATT_KV,
             D_MODEL, D_MODEL, D_MODEL, D_MODEL, 4 * ML_HEADS, 3 * D_MODEL)
    offs = np.cumsum((0,) + sizes)
    return [w_in[:, :, offs[i]:offs[i + 1]] for i in range(len(sizes))]


def kernel(x_prompt, x_sample, cache_k, cache_v, state_ssd, state_ml_c, state_ml_n, state_ml_m,
           c, c_ctx, w_ada, b_ada, w_in, ssd_conv_w, ssd_conv_b, ssd_a_log, ssd_dt_bias, ssd_d,
           ssd_norm, att_q_norm, att_k_norm, ml_conv_w, ml_conv_b, ml_gate_bias, ml_norm,
           w_branch, w_out, w_ffn_in, w_ffn_out, final_norm):
    depth = w_in.shape[0]
    nbp, seqp, _ = x_prompt.shape
    nbs, seqs, _ = x_sample.shape
    past = cache_k.shape[2]

    (s_x, s_z, s_b, s_c, s_dt, a_q, a_k, a_v, m_q, m_k, m_v, m_o, m_g, g) = _split_w_in(w_in)
    w_main = jnp.concatenate([s_x, s_b, s_c, a_q, a_k, a_v, m_q, m_k, m_v, s_z, m_o, g], axis=-1).astype(BF16)
    m_g4 = m_g.reshape(depth, D_MODEL, 2, 2, ML_HEADS)
    w_small = jnp.concatenate(
        [s_dt, m_g4[:, :, :, 0].reshape(depth, D_MODEL, 2 * ML_HEADS),
         m_g4[:, :, :, 1].reshape(depth, D_MODEL, 2 * ML_HEADS),
         jnp.zeros((depth, D_MODEL, GATE_W - G_LF - 2 * ML_HEADS), F32)], axis=-1).astype(BF16)
    w_small_t = jnp.swapaxes(w_small, 1, 2)
    pad = jnp.zeros((depth, GATE_W - G_LF - 2 * ML_HEADS), F32)
    gate_bias = jnp.concatenate(
        [ssd_dt_bias.reshape(depth, 2 * SSD_HEADS), ml_gate_bias[:, :, 0].reshape(depth, 2 * ML_HEADS),
         ml_gate_bias[:, :, 1].reshape(depth, 2 * ML_HEADS), pad], axis=-1)
    a_vec = jnp.concatenate(
        [-jnp.exp(ssd_a_log.reshape(depth, 2 * SSD_HEADS)), jnp.zeros((depth, GATE_W - 2 * SSD_HEADS), F32)], axis=-1)
    d_vec = jnp.repeat(ssd_d, SSD_HEAD_DIM, axis=-1).reshape(depth, 1, D_MODEL)
    qk_gain = jnp.concatenate([jnp.tile(att_q_norm, (1, ATT_HEADS)), jnp.tile(att_k_norm, (1, ATT_KV_HEADS))], axis=-1)
    qk_gain = qk_gain.reshape(depth, 1, D_MODEL + ATT_KV)
    w_branch_b = w_branch.astype(BF16)
    w_out_b = w_out.astype(BF16)
    w_ffn_in_b = w_ffn_in.astype(BF16)
    w_ffn_out_b = w_ffn_out.astype(BF16)
    consts = _gate_consts()
    tables = _rope_tables(seqs)
    cache = (cache_k.reshape(nbs, depth, past, ATT_KV), cache_v.reshape(nbs, depth, past, ATT_KV))
    ssd_consts, ml_consts = _scan_consts()
    ssd_h0 = _ssd_state_to_compact(state_ssd)
    ml_cn0 = jnp.concatenate(
        [state_ml_c, jnp.broadcast_to(state_ml_n[..., None], state_ml_c.shape)], axis=-1)
    ml_m0 = jnp.repeat(state_ml_m, ML_HEAD_DIM, axis=-1).reshape(nbs, depth, 2, 1, D_MODEL)

    rows = 8 * ((1 + nbs + 7) // 8)
    cc = jnp.concatenate([c_ctx[None], c, jnp.zeros((rows - 1 - nbs, D_MODEL), F32)], axis=0)
    mod = _ada(cc, w_ada, b_ada)

    def mods(layer, lo, hi):
        m = mod[layer, lo:hi].reshape(hi - lo, 1, 6, D_MODEL)
        return [m[:, :, i] for i in range(6)]

    def layer_step(x, layer, nb, seq, mod6, ctx):
        shift1, scale1, gate1, shift2, scale2, gate2 = mod6
        h = _norm_mod(x, scale1, shift1, seq)
        u = _proj(h, w_main[layer])
        gates = _gate(h, w_small[layer], w_small_t[layer], gate_bias[layer], a_vec[layer], consts)
        xbc = _conv(u, U_CONV, SSD_CONV, ssd_conv_w[layer], ssd_conv_b[layer], nb, seq)
        mqk = _conv(u, U_MQ, 2 * D_MODEL, ml_conv_w[layer], ml_conv_b[layer], nb, seq)
        if ctx:
            yf, yb = _ssd(xbc, gates, ssd_consts, nb, seq, ssd_h0, layer)
            q, kn, vn = _qkprep(u, qk_gain[layer], tables, nb, seq, False)
            att = _attn(q, kn, vn, cache, nb, seq, layer)
            hf, hb = _mlstm(mqk, u, gates, ml_consts, nb, seq, (ml_cn0, ml_m0), layer)
            new = None
        else:
            yf, yb, hst = _ssd(xbc, gates, ssd_consts, nb, seq, None, layer)
            q, kn, vn, kf, vf = _qkprep(u, qk_gain[layer], None, nb, seq, True)
            att = _attn(q, kn, vn, None, nb, seq, layer)
            hf, hb, cn, mrow = _mlstm(mqk, u, gates, ml_consts, nb, seq, None, layer)
            new = (kf.reshape(nb, seq, ATT_KV_HEADS, ATT_HEAD_DIM), vf.reshape(nb, seq, ATT_KV_HEADS, ATT_HEAD_DIM),
                   _ssd_state_from_compact(hst), cn[..., :ML_HEAD_DIM], cn[..., ML_HEAD_DIM],
                   mrow[:, :, 0, ::ML_HEAD_DIM])
        x = _merge(yf, yb, xbc, u, att, hf, hb, x, gate1, d_vec[layer],
                   ssd_norm[layer].reshape(1, D_MODEL), ml_norm[layer].reshape(1, D_MODEL),
                   w_branch_b[layer], w_out_b[layer], seq)
        x = _ffn(x, scale2, shift2, gate2, w_ffn_in_b[layer], w_ffn_out_b[layer], seq)
        return x, new

    y_p = x_prompt.reshape(nbp * seqp, D_MODEL)
    y_s = x_sample.reshape(nbs * seqs, D_MODEL)
    news = []
    for layer in range(depth):
        y_p, new = layer_step(y_p, layer, nbp, seqp, mods(layer, 0, 1), False)
        news.append(new)
        y_s, _ = layer_step(y_s, layer, nbs, seqs, mods(layer, 1, 1 + nbs), True)

    y_prompt = _norm_gain(y_p, final_norm).reshape(x_prompt.shape)
    y_sample = _norm_gain(y_s, final_norm).reshape(x_sample.shape)
    stacked = [jnp.stack([n[i] for n in news], axis=1) for i in range(6)]
    return (y_prompt, y_sample) + tuple(stacked)
```

```python
import functools

import numpy as np
import jax
import jax.numpy as jnp
from jax import lax
from jax.experimental import pallas as pl
from jax.experimental.pallas import tpu as pltpu

F32 = jnp.float32
BF16 = jnp.bfloat16

D_MODEL = 1024
GRID_W = 64
EPS = 1e-6
CONV_W = 4
CHUNK = 128
SSD_HEADS = 16
SSD_HEAD_DIM = 64
SSD_GROUPS = 4
SSD_STATE = 64
SSD_BC = SSD_GROUPS * SSD_STATE
SSD_CONV = D_MODEL + 2 * SSD_BC
ATT_HEADS = 16
ATT_HEAD_DIM = 64
ATT_KV_HEADS = 4
ATT_KV = ATT_KV_HEADS * ATT_HEAD_DIM
ATT_QKV = D_MODEL + 2 * ATT_KV
ROPE_THETA = 10000.0
ML_HEADS = 8
ML_HEAD_DIM = 128
FFN_HIDDEN = 2816
GATE_W = 128

U_CONV = 0
U_ATT = 1536
U_MQ = 3072
U_MK = 4096
U_MV = 5120
U_SZ = 6144
U_MO = 7168
U_G = 8192
U_W = 11264

G_DT = 0
G_LI = 32
G_LF = 48

VMEM_LIMIT = 48 * 1024 * 1024
NT_DIMS = (((1,), (1,)), ((), ()))
TN_DIMS = (((0,), (0,)), ((), ()))


def _cparams(*sem):
    return pltpu.CompilerParams(dimension_semantics=sem, vmem_limit_bytes=VMEM_LIMIT)


def _sigmoid(x):
    return 1.0 / (1.0 + jnp.exp(-x))


def _softplus(x):
    return jnp.maximum(x, 0.0) + jnp.log1p(jnp.exp(-jnp.abs(x)))


def _dot(a, b):
    return jnp.dot(a, b, preferred_element_type=F32)


def _dot_nt(a, b):
    return lax.dot_general(a, b, NT_DIMS, preferred_element_type=F32)


def _dot_tn(a, b):
    return lax.dot_general(a, b, TN_DIMS, preferred_element_type=F32)


def _ada_kernel(c_ref, w_ref, b_ref, o_ref):
    c = c_ref[...]
    s = (c * _sigmoid(c)).astype(BF16)
    o_ref[0] = _dot(s, w_ref[0].astype(BF16)) + b_ref[0]


def _ada(cc, w_ada, b_ada):
    depth, _, n6 = w_ada.shape
    rows = cc.shape[0]
    tn = 512
    return pl.pallas_call(
        _ada_kernel,
        grid=(depth, n6 // tn),
        in_specs=[pl.BlockSpec((rows, D_MODEL), lambda l, j: (0, 0)),
                  pl.BlockSpec((1, D_MODEL, tn), lambda l, j: (l, 0, j)),
                  pl.BlockSpec((1, 1, tn), lambda l, j: (l, 0, j))],
        out_specs=pl.BlockSpec((1, rows, tn), lambda l, j: (l, 0, j)),
        out_shape=jax.ShapeDtypeStruct((depth, rows, n6), F32),
        compiler_params=_cparams("parallel", "parallel"),
        name="ada_mod",
    )(cc, w_ada, b_ada.reshape(depth, 1, n6))


def _rms(x):
    return x * lax.rsqrt(jnp.mean(x * x, axis=-1, keepdims=True) + EPS)


def _norm_mod_kernel(x_ref, sc_ref, sh_ref, o_ref):
    o_ref[...] = (_rms(x_ref[...]) * (1.0 + sc_ref[0]) + sh_ref[0]).astype(o_ref.dtype)


def _norm_gain_kernel(x_ref, g_ref, o_ref):
    o_ref[...] = _rms(x_ref[...]) * g_ref[...]


def _mod_spec(nb, seq, tm):
    if nb == 1:
        return pl.BlockSpec((1, 1, D_MODEL), lambda i: (0, 0, 0))
    return pl.BlockSpec((1, 1, D_MODEL), lambda i: ((i * tm) // seq, 0, 0))


def _norm_mod(x, scale, shift, seq):
    t = x.shape[0]
    tm = min(t, 512)
    nb = scale.shape[0]
    return pl.pallas_call(
        _norm_mod_kernel,
        grid=(t // tm,),
        in_specs=[pl.BlockSpec((tm, D_MODEL), lambda i: (i, 0)),
                  _mod_spec(nb, seq, tm), _mod_spec(nb, seq, tm)],
        out_specs=pl.BlockSpec((tm, D_MODEL), lambda i: (i, 0)),
        out_shape=jax.ShapeDtypeStruct((t, D_MODEL), BF16),
        compiler_params=_cparams("parallel"),
        name="norm_mod",
    )(x, scale, shift)


def _norm_gain(x, gain):
    t = x.shape[0]
    tm = min(t, 512)
    return pl.pallas_call(
        _norm_gain_kernel,
        grid=(t // tm,),
        in_specs=[pl.BlockSpec((tm, D_MODEL), lambda i: (i, 0)),
                  pl.BlockSpec((1, D_MODEL), lambda i: (0, 0))],
        out_specs=pl.BlockSpec((tm, D_MODEL), lambda i: (i, 0)),
        out_shape=jax.ShapeDtypeStruct((t, D_MODEL), F32),
        compiler_params=_cparams("parallel"),
        name="final_norm",
    )(x, gain.reshape(1, D_MODEL))


def _sigmoid_tanh(x):
    return 0.5 * jnp.tanh(0.5 * x) + 0.5


def _proj_kernel(a_ref, w_ref, o_ref, *, silu_lo, sig_lo, sub):
    j = pl.program_id(1)
    w = w_ref[...]

    def run(act):
        for r in range(a_ref.shape[0] // sub):
            rows = slice(r * sub, (r + 1) * sub)
            o_ref[rows, :] = act(_dot(a_ref[rows, :], w)).astype(o_ref.dtype)

    @pl.when(j < silu_lo)
    def _():
        run(lambda acc: acc)

    @pl.when(jnp.logical_and(j >= silu_lo, j < sig_lo))
    def _():
        run(lambda acc: acc * _sigmoid_tanh(acc))

    @pl.when(j >= sig_lo)
    def _():
        run(_sigmoid_tanh)


def _proj(h, w_main, layer):
    t = h.shape[0]
    tm = min(t, 2048)
    tn = 1024
    kern = functools.partial(_proj_kernel, silu_lo=U_SZ // tn, sig_lo=U_MO // tn, sub=min(tm, 512))
    return pl.pallas_call(
        kern,
        grid=(t // tm, U_W // tn),
        in_specs=[pl.BlockSpec((tm, D_MODEL), lambda i, j: (i, 0)),
                  pl.BlockSpec((None, D_MODEL, tn), lambda i, j: (layer, 0, j))],
        out_specs=pl.BlockSpec((tm, tn), lambda i, j: (i, j)),
        out_shape=jax.ShapeDtypeStruct((t, U_W), BF16),
        compiler_params=_cparams("parallel", "parallel"),
        name="in_proj",
    )(h, w_main)


def _split3(x):
    hi = x.astype(BF16)
    r1 = x - hi.astype(F32)
    mid = r1.astype(BF16)
    lo = (r1 - mid.astype(F32)).astype(BF16)
    return hi, mid, lo


def _tri_left(tri, x):
    hi, mid, lo = _split3(x)
    return _dot(tri, hi) + _dot(tri, mid) + _dot(tri, lo)


def _tri_right(x, tri):
    hi, mid, lo = _split3(x)
    return _dot(hi, tri) + _dot(mid, tri) + _dot(lo, tri)


def _gate_vals(u, idx, a):
    sp = _softplus(u)
    ls = -_softplus(-u)
    val = jnp.where(idx < G_LI, sp, jnp.where(idx < G_LF, u, jnp.where(idx < G_LF + 16, ls, 0.0)))
    cin = jnp.where(idx < G_LI, sp * a, jnp.where(jnp.logical_and(idx >= G_LF, idx < G_LF + 16), ls, 0.0))
    return val, cin


def _gate_kernel(h_ref, ws_ref, wst_ref, b_ref, bt_ref, a_ref, at_ref, dm_ref, dmt_ref,
                 tlo_ref, tup_ref, val_ref, cum_ref, valt_ref, cumt_ref):
    h = h_ref[...]
    tlo = tlo_ref[...]
    tup = tup_ref[...]
    rows = h.shape[0]
    u = _dot(h, ws_ref[...]) + b_ref[...]
    val, cin = _gate_vals(u, lax.broadcasted_iota(jnp.int32, (rows, GATE_W), 1), a_ref[...])
    val_ref[...] = val
    ut = _dot_nt(wst_ref[...], h) + bt_ref[...]
    valt, cint = _gate_vals(ut, lax.broadcasted_iota(jnp.int32, (GATE_W, rows), 0), at_ref[...])
    valt_ref[...] = valt
    back = dm_ref[...] > 0.5
    back_t = dmt_ref[...] > 0.5
    for c in range(rows // CHUNK):
        sl = slice(c * CHUNK, (c + 1) * CHUNK)
        cum_ref[sl, :] = jnp.where(back, _tri_left(tup, cin[sl]), _tri_left(tlo, cin[sl]))
        cumt_ref[:, sl] = jnp.where(back_t, _tri_right(cint[:, sl], tlo), _tri_right(cint[:, sl], tup))


def _gate(h, ws, wst, bias, avec, consts):
    t = h.shape[0]
    tm = min(t, 4 * CHUNK)
    dmask, tlo, tup = consts
    row = lambda i: (i, 0)
    col = lambda i: (0, i)
    fix = lambda i: (0, 0)
    tm_out = jax.ShapeDtypeStruct((t, GATE_W), F32)
    fm_out = jax.ShapeDtypeStruct((GATE_W, t), F32)
    return pl.pallas_call(
        _gate_kernel,
        grid=(t // tm,),
        in_specs=[pl.BlockSpec((tm, D_MODEL), row),
                  pl.BlockSpec((D_MODEL, GATE_W), fix),
                  pl.BlockSpec((GATE_W, D_MODEL), fix),
                  pl.BlockSpec((1, GATE_W), fix), pl.BlockSpec((GATE_W, 1), fix),
                  pl.BlockSpec((1, GATE_W), fix), pl.BlockSpec((GATE_W, 1), fix),
                  pl.BlockSpec((1, GATE_W), fix), pl.BlockSpec((GATE_W, 1), fix),
                  pl.BlockSpec((CHUNK, CHUNK), fix), pl.BlockSpec((CHUNK, CHUNK), fix)],
        out_specs=[pl.BlockSpec((tm, GATE_W), row), pl.BlockSpec((tm, GATE_W), row),
                   pl.BlockSpec((GATE_W, tm), col), pl.BlockSpec((GATE_W, tm), col)],
        out_shape=[tm_out, tm_out, fm_out, fm_out],
        compiler_params=_cparams("parallel"),
        name="gate_prep",
    )(h, ws, wst, bias.reshape(1, GATE_W), bias.reshape(GATE_W, 1),
      avec.reshape(1, GATE_W), avec.reshape(GATE_W, 1),
      dmask.reshape(1, GATE_W), dmask.reshape(GATE_W, 1), tlo, tup)


def _conv_kernel(x_ref, w_ref, b_ref, o_ref):
    x = x_ref[...].astype(F32)
    seq = x.shape[0]
    row = lax.broadcasted_iota(jnp.int32, x.shape, 0)
    acc = x * w_ref[1:2, :] + b_ref[...]
    acc = acc + jnp.where(row >= 1, pltpu.roll(x, 1, 0), 0.0) * w_ref[0:1, :]
    acc = acc + jnp.where(row < seq - 1, pltpu.roll(x, seq - 1, 0), 0.0) * w_ref[2:3, :]
    acc = acc + jnp.where(row < seq - 2, pltpu.roll(x, seq - 2, 0), 0.0) * w_ref[3:4, :]
    o_ref[...] = (acc * _sigmoid(acc)).astype(o_ref.dtype)


def _conv(u, col0, width, w, b, nb, seq, out_dtype):
    tc = 256
    c0 = col0 // tc
    return pl.pallas_call(
        _conv_kernel,
        grid=(nb, width // tc),
        in_specs=[pl.BlockSpec((seq, tc), lambda i, j: (i, c0 + j)),
                  pl.BlockSpec((CONV_W, tc), lambda i, j: (0, j)),
                  pl.BlockSpec((1, tc), lambda i, j: (0, j))],
        out_specs=pl.BlockSpec((seq, tc), lambda i, j: (i, j)),
        out_shape=jax.ShapeDtypeStruct((nb * seq, width), out_dtype),
        compiler_params=_cparams("parallel", "parallel"),
        name="dwconv_silu",
    )(u, w, b.reshape(1, width))


def _tri_mask(fwd):
    row = lax.broadcasted_iota(jnp.int32, (CHUNK, CHUNK), 0)
    col = lax.broadcasted_iota(jnp.int32, (CHUNK, CHUNK), 1)
    return row >= col if fwd else row <= col


def _lane_expand(x, e):
    hi, mid, _ = _split3(x)
    return _dot(jnp.concatenate([hi, mid], axis=1), e)


SSD_GROUP_W = (SSD_HEADS // SSD_GROUPS) * SSD_HEAD_DIM


def _ssd_dir(d, x_ref, bc_ref, val_ref, cum_ref, valt_ref, cumt_ref, e64_ref, e128_ref, eye_ref, skip,
             y_ref, st_scr):
    mask = _tri_mask(d == 0)
    end = CHUNK - 1 if d == 0 else 0
    x = x_ref[...]
    bb = bc_ref[:, :SSD_BC]
    cm = bc_ref[:, SSD_BC:]
    lane_c = lax.broadcasted_iota(jnp.int32, cm.shape, 1)
    lane_x = lax.broadcasted_iota(jnp.int32, (CHUNK, 2 * SSD_HEAD_DIM), 1)
    a128 = _lane_expand(cum_ref[...], e128_ref[d])
    a64 = jnp.concatenate(
        [jnp.where(lane_x < SSD_HEAD_DIM, a128[:, (2 * p) * CHUNK:(2 * p + 1) * CHUNK],
                   a128[:, (2 * p + 1) * CHUNK:(2 * p + 2) * CHUNK]) for p in range(SSD_HEADS // 2)], axis=1)
    dt64 = _dot(val_ref[...].astype(BF16), e64_ref[d])
    dt_r = valt_ref[...]
    ac_r = cumt_ref[...]
    st = st_scr[d]
    inter = _dot(cm, st.astype(BF16))
    ea = jnp.exp(a64)
    cm32 = cm.astype(F32)
    c_stack = jnp.concatenate(
        [jnp.where(jnp.logical_and(lane_c >= g * SSD_STATE, lane_c < (g + 1) * SSD_STATE), cm32, 0.0).astype(BF16)
         for g in range(SSD_GROUPS)], axis=0)
    cbt_all = _dot_nt(c_stack, bb)
    for g in range(SSD_GROUPS):
        cbt = cbt_all[g * CHUNK:(g + 1) * CHUNK]
        for pr in range(2):
            h0 = g * (SSD_HEADS // SSD_GROUPS) + 2 * pr
            parts = []
            for hh in (h0, h0 + 1):
                lane = G_DT + d * SSD_HEADS + hh
                seg = jnp.where(mask, a128[:, hh * CHUNK:(hh + 1) * CHUNK] - ac_r[lane:lane + 1, :], -jnp.inf)
                parts.append((cbt * jnp.exp(seg) * dt_r[lane:lane + 1, :]).astype(BF16))
            cs = slice(h0 * SSD_HEAD_DIM, (h0 + 2) * SSD_HEAD_DIM)
            xp = x[:, cs]
            xbd = jnp.concatenate([jnp.where(lane_x < SSD_HEAD_DIM, xp, 0.0),
                                   jnp.where(lane_x >= SSD_HEAD_DIM, xp, 0.0)], axis=0).astype(BF16)
            y = _dot(jnp.concatenate(parts, axis=1), xbd) + ea[:, cs] * inter[:, cs]
            if skip is not None:
                y = y + skip[:, cs] * xp
            y_ref[:, cs] = y
    tot = a64[end:end + 1, :]
    xw = (x * (jnp.exp(tot - a64) * dt64)).astype(BF16)
    bt = _dot_nt(eye_ref[...], bb).astype(BF16)
    decay = jnp.exp(tot)
    for g in range(SSD_GROUPS):
        rs = slice(g * SSD_STATE, (g + 1) * SSD_STATE)
        cs = slice(g * SSD_GROUP_W, (g + 1) * SSD_GROUP_W)
        st_scr[d, rs, cs] = decay[:, cs] * st[rs, cs] + _dot(bt[rs, :], xw[:, cs])


def _ssd_kernel(*refs, has_h0, emit_state):
    xf_ref, bcf_ref, xb_ref, bcb_ref = refs[0:4]
    gf = refs[4:8]
    gb = refs[8:12]
    consts = refs[12:15]
    dvec_ref = refs[15]
    k = 16
    if has_h0:
        h0_ref = refs[k]
        k += 1
    yf_ref, yb_ref = refs[k:k + 2]
    k += 2
    if emit_state:
        hout_ref = refs[k]
        k += 1
    st_scr = refs[k]
    c = pl.program_id(1)

    @pl.when(c == 0)
    def _():
        st_scr[...] = jnp.zeros(st_scr.shape, F32)
        if has_h0:
            for d in range(2):
                for g in range(SSD_GROUPS):
                    st_scr[d, g * SSD_STATE:(g + 1) * SSD_STATE,
                           g * SSD_GROUP_W:(g + 1) * SSD_GROUP_W] = h0_ref[0, 0, d, g]

    _ssd_dir(0, xf_ref, bcf_ref, *gf, *consts, dvec_ref[...], yf_ref, st_scr)
    _ssd_dir(1, xb_ref, bcb_ref, *gb, *consts, None, yb_ref, st_scr)

    if emit_state:
        @pl.when(c == pl.num_programs(1) - 1)
        def _():
            for d in range(2):
                for g in range(SSD_GROUPS):
                    hout_ref[0, d, g] = st_scr[d, g * SSD_STATE:(g + 1) * SSD_STATE,
                                               g * SSD_GROUP_W:(g + 1) * SSD_GROUP_W]


def _chunk_specs(nc, width):
    fwd = pl.BlockSpec((CHUNK, width), lambda b, c: (b * nc + c, 0))
    bwd = pl.BlockSpec((CHUNK, width), lambda b, c: (b * nc + nc - 1 - c, 0))
    return fwd, bwd


def _gate_specs(nc):
    tf, tb = _chunk_specs(nc, GATE_W)
    ff = pl.BlockSpec((GATE_W, CHUNK), lambda b, c: (0, b * nc + c))
    fb = pl.BlockSpec((GATE_W, CHUNK), lambda b, c: (0, b * nc + nc - 1 - c))
    return [tf, tf, ff, ff], [tb, tb, fb, fb]


def _const_spec(arr):
    nd = arr.ndim
    return pl.BlockSpec(arr.shape, lambda b, c: (0,) * nd)


def _ssd(x, bc, gates, consts, dvec, nb, seq, state, layer):
    nc = seq // CHUNK
    t = nb * seq
    has_h0 = state is not None
    xf, xb = _chunk_specs(nc, D_MODEL)
    bcf, bcb = _chunk_specs(nc, 2 * SSD_BC)
    gfs, gbs = _gate_specs(nc)
    yf, yb = _chunk_specs(nc, D_MODEL)
    st_shape = (2, SSD_GROUPS, SSD_STATE, SSD_GROUP_W)
    in_specs = [xf, bcf, xb, bcb] + gfs + gbs + [_const_spec(a) for a in consts] + [_const_spec(dvec)]
    args = [x, bc, x, bc] + list(gates) + list(gates) + list(consts) + [dvec]
    if has_h0:
        in_specs.append(pl.BlockSpec((1, 1) + st_shape, lambda b, c: (b, layer, 0, 0, 0, 0)))
        args.append(state)
    out_specs = [yf, yb]
    out_shape = [jax.ShapeDtypeStruct((t, D_MODEL), F32)] * 2
    if not has_h0:
        out_specs.append(pl.BlockSpec((1,) + st_shape, lambda b, c: (b, 0, 0, 0, 0)))
        out_shape.append(jax.ShapeDtypeStruct((nb,) + st_shape, F32))
    return pl.pallas_call(
        functools.partial(_ssd_kernel, has_h0=has_h0, emit_state=not has_h0),
        grid=(nb, nc),
        in_specs=in_specs,
        out_specs=out_specs,
        out_shape=out_shape,
        scratch_shapes=[pltpu.VMEM((2, SSD_BC, D_MODEL), F32)],
        compiler_params=_cparams("parallel", "arbitrary"),
        name="ssd_scan",
    )(*args)


def _ml_dir(d, q_ref, k_ref, v_ref, val_ref, cum_ref, valt_ref, cumt_ref, eb_ref, ei_ref, eye_ref,
            y_ref, cn_scr, m_scr):
    mask = _tri_mask(d == 0)
    end = CHUNK - 1 if d == 0 else 0
    b128 = _lane_expand(cum_ref[...], eb_ref[d])
    i128 = _lane_expand(val_ref[...], ei_ref[d])
    li_r = valt_ref[...]
    bc_r = cumt_ref[...]
    m_row = m_scr[d]
    q = q_ref[...]
    k = k_ref[...].astype(F32) * (ML_HEAD_DIM ** -0.5)
    v = v_ref[...]
    b_end = b128[end:end + 1, :]
    wj = b_end - b128 + i128
    m_new = jnp.maximum(b_end + m_row, jnp.max(wj, axis=0, keepdims=True))
    kw = (k * jnp.exp(wj - m_new)).astype(BF16)
    decay = jnp.exp(b_end + m_row - m_new)
    bm = b128 + m_row
    ones = jnp.ones((CHUNK, ML_HEAD_DIM), BF16)
    for hh in range(ML_HEADS):
        sl = slice(hh * ML_HEAD_DIM, (hh + 1) * ML_HEAD_DIM)
        li = G_LI + d * ML_HEADS + hh
        lf = G_LF + d * ML_HEADS + hh
        dmat = jnp.where(mask, b128[:, sl] - bc_r[lf:lf + 1, :] + li_r[li:li + 1, :], -jnp.inf)
        m_t = jnp.maximum(bm[:, sl], jnp.max(dmat, axis=1, keepdims=True))
        qb = q[:, sl].astype(BF16)
        s = _dot_nt(qb, k[:, sl].astype(BF16)) * jnp.exp(dmat - m_t)
        inter = jnp.exp(bm[:, sl] - m_t)
        v1 = jnp.concatenate([v[:, sl].astype(BF16), ones], axis=1)
        cn = cn_scr[d, hh]
        kwt = _dot_nt(eye_ref[...], kw[:, sl]).astype(BF16)
        sv = _dot(jnp.concatenate([s.astype(BF16), kwt], axis=0), v1)
        r = sv[:CHUNK] + jnp.concatenate([inter, inter], axis=1) * _dot(qb, cn.astype(BF16))
        y_ref[:, sl] = r[:, :ML_HEAD_DIM] / jnp.maximum(jnp.abs(r[:, ML_HEAD_DIM:]), jnp.exp(-m_t))
        dch = decay[:, sl]
        cn_scr[d, hh] = jnp.concatenate([dch, dch], axis=1) * cn + sv[CHUNK:]
    m_scr[d] = m_new


def _ml_kernel(*refs, has_init, emit_state):
    qf_ref, kf_ref, vf_ref, qb_ref, kb_ref, vb_ref = refs[0:6]
    gf = refs[6:10]
    gb = refs[10:14]
    consts = refs[14:17]
    k = 17
    if has_init:
        cn0_ref, m0_ref = refs[k:k + 2]
        k += 2
    yf_ref, yb_ref = refs[k:k + 2]
    k += 2
    if emit_state:
        cnout_ref, mout_ref = refs[k:k + 2]
        k += 2
    cn_scr, m_scr = refs[k:k + 2]
    c = pl.program_id(1)

    @pl.when(c == 0)
    def _():
        if has_init:
            cn_scr[...] = cn0_ref[0, 0]
            m_scr[...] = m0_ref[0, 0]
        else:
            cn_scr[...] = jnp.zeros(cn_scr.shape, F32)
            m_scr[...] = jnp.zeros(m_scr.shape, F32)

    _ml_dir(0, qf_ref, kf_ref, vf_ref, *gf, *consts, yf_ref, cn_scr, m_scr)
    _ml_dir(1, qb_ref, kb_ref, vb_ref, *gb, *consts, yb_ref, cn_scr, m_scr)

    if emit_state:
        @pl.when(c == pl.num_programs(1) - 1)
        def _():
            cnout_ref[0] = cn_scr[...]
            mout_ref[0] = m_scr[...]


def _mlstm(qk, u, gates, consts, nb, seq, states, layer):
    nc = seq // CHUNK
    t = nb * seq
    has_init = states is not None
    mvb = U_MV // D_MODEL

    def tok(col, rev):
        if rev:
            return pl.BlockSpec((CHUNK, D_MODEL), lambda b, c: (b * nc + nc - 1 - c, col))
        return pl.BlockSpec((CHUNK, D_MODEL), lambda b, c: (b * nc + c, col))

    gfs, gbs = _gate_specs(nc)
    in_specs = ([tok(0, False), tok(1, False), tok(mvb, False),
                 tok(0, True), tok(1, True), tok(mvb, True)] + gfs + gbs
                + [_const_spec(a) for a in consts])
    args = [qk, qk, u, qk, qk, u] + list(gates) + list(gates) + list(consts)
    cn_shape = (2, ML_HEADS, ML_HEAD_DIM, 2 * ML_HEAD_DIM)
    m_shape = (2, 1, D_MODEL)
    if has_init:
        in_specs += [pl.BlockSpec((1, 1) + cn_shape, lambda b, c: (b, layer, 0, 0, 0, 0)),
                     pl.BlockSpec((1, 1) + m_shape, lambda b, c: (b, layer, 0, 0, 0))]
        args += list(states)
    out_specs = [tok(0, False), tok(0, True)]
    out_shape = [jax.ShapeDtypeStruct((t, D_MODEL), F32)] * 2
    if not has_init:
        out_specs += [pl.BlockSpec((1,) + cn_shape, lambda b, c: (b, 0, 0, 0, 0)),
                      pl.BlockSpec((1,) + m_shape, lambda b, c: (b, 0, 0, 0))]
        out_shape += [jax.ShapeDtypeStruct((nb,) + cn_shape, F32),
                      jax.ShapeDtypeStruct((nb,) + m_shape, F32)]
    return pl.pallas_call(
        functools.partial(_ml_kernel, has_init=has_init, emit_state=not has_init),
        grid=(nb, nc),
        in_specs=in_specs,
        out_specs=out_specs,
        out_shape=out_shape,
        scratch_shapes=[pltpu.VMEM(cn_shape, F32), pltpu.VMEM(m_shape, F32)],
        compiler_params=_cparams("parallel", "arbitrary"),
        name="mlstm_scan",
    )(*args)


def _qkprep_kernel(*refs, rope, emit_f32):
    u_ref, gain_ref = refs[0:2]
    k = 2
    if rope:
        cos_ref, sin_ref = refs[k:k + 2]
        k += 2
    q_ref, k_ref, v_ref = refs[k:k + 3]
    k += 3
    if emit_f32:
        kf_ref, vf_ref = refs[k:k + 2]
        k += 2
    nqk = D_MODEL + ATT_KV
    pair = 2 * ATT_HEAD_DIM
    low = lax.broadcasted_iota(jnp.int32, (u_ref.shape[0], pair), 1) < ATT_HEAD_DIM
    parts = []
    for p in range(nqk // pair):
        x = u_ref[:, p * pair:(p + 1) * pair].astype(F32)
        x2 = x * x
        ss = jnp.where(low, jnp.sum(jnp.where(low, x2, 0.0), axis=-1, keepdims=True),
                       jnp.sum(jnp.where(low, 0.0, x2), axis=-1, keepdims=True))
        parts.append(x * lax.rsqrt(ss * (1.0 / ATT_HEAD_DIM) + EPS))
    xn = jnp.concatenate(parts, axis=1) * gain_ref[...]
    v = u_ref[:, nqk:nqk + ATT_KV]
    if emit_f32:
        kf_ref[...] = xn[:, D_MODEL:nqk]
        vf_ref[...] = v.astype(F32)
    if rope:
        lane = lax.broadcasted_iota(jnp.int32, xn.shape, 1)
        quarter = ATT_HEAD_DIM // 4
        first = jnp.bitwise_and(lane, 2 * quarter - 1) < quarter
        swapped = jnp.where(first, pltpu.roll(xn, nqk - quarter, 1), pltpu.roll(xn, quarter, 1))
        xn = xn * cos_ref[...] + swapped * sin_ref[...]
    q_ref[...] = (xn[:, :D_MODEL] * (ATT_HEAD_DIM ** -0.5)).astype(BF16)
    k_ref[...] = xn[:, D_MODEL:nqk].astype(BF16)
    v_ref[...] = v.astype(BF16)


def _qkprep(u, gain, tables, nb, seq, emit_f32):
    t = nb * seq
    tl = min(seq, 256)
    nl = seq // tl
    nqk = D_MODEL + ATT_KV
    rope = tables is not None
    ub = U_ATT // ATT_QKV
    tok = lambda w: pl.BlockSpec((tl, w), lambda i, b: (b * nl + i, 0))
    in_specs = [pl.BlockSpec((tl, ATT_QKV), lambda i, b: (b * nl + i, ub)),
                pl.BlockSpec((1, nqk), lambda i, b: (0, 0))]
    args = [u, gain]
    if rope:
        in_specs += [pl.BlockSpec((tl, nqk), lambda i, b: (i, 0))] * 2
        args += list(tables)
    out_specs = [tok(D_MODEL), tok(ATT_KV), tok(ATT_KV)]
    out_shape = [jax.ShapeDtypeStruct((t, D_MODEL), BF16),
                 jax.ShapeDtypeStruct((t, ATT_KV), BF16),
                 jax.ShapeDtypeStruct((t, ATT_KV), BF16)]
    if emit_f32:
        out_specs += [tok(ATT_KV), tok(ATT_KV)]
        out_shape += [jax.ShapeDtypeStruct((t, ATT_KV), F32)] * 2
    return pl.pallas_call(
        functools.partial(_qkprep_kernel, rope=rope, emit_f32=emit_f32),
        grid=(nl, nb),
        in_specs=in_specs,
        out_specs=out_specs,
        out_shape=out_shape,
        compiler_params=_cparams("parallel", "parallel"),
        name="qk_prep",
    )(*args)


def _attn_kernel(*refs, has_cache, tq):
    q_ref, kn_ref, vn_ref = refs[0:3]
    k = 3
    if has_cache:
        kc_ref, vc_ref = refs[k:k + 2]
        k += 2
    o_ref = refs[k]
    grp = ATT_HEADS // ATT_KV_HEADS
    hd = ATT_HEAD_DIM
    for kvh in range(ATT_KV_HEADS):
        ksl = slice(kvh * hd, (kvh + 1) * hd)
        qs = jnp.concatenate(
            [q_ref[:, (kvh * grp + g) * hd:(kvh * grp + g + 1) * hd] for g in range(grp)], axis=0)
        s_n = _dot_nt(qs, kn_ref[:, ksl])
        m = jnp.max(s_n, axis=-1, keepdims=True)
        if has_cache:
            s_c = _dot_nt(qs, kc_ref[0, 0, :, ksl].astype(BF16))
            m = jnp.maximum(m, jnp.max(s_c, axis=-1, keepdims=True))
        p_n = jnp.exp(s_n - m)
        l = jnp.sum(p_n, axis=-1, keepdims=True)
        o = _dot(p_n.astype(BF16), vn_ref[:, ksl])
        if has_cache:
            p_c = jnp.exp(s_c - m)
            l = l + jnp.sum(p_c, axis=-1, keepdims=True)
            o = o + _dot(p_c.astype(BF16), vc_ref[0, 0, :, ksl].astype(BF16))
        o = (o / l).astype(o_ref.dtype)
        for g in range(grp):
            o_ref[:, (kvh * grp + g) * hd:(kvh * grp + g + 1) * hd] = o[g * tq:(g + 1) * tq]


def _attn(q, kn, vn, cache, nb, seq, layer):
    t = nb * seq
    tq = min(seq, 128)
    nq = seq // tq
    has_cache = cache is not None
    in_specs = [pl.BlockSpec((tq, D_MODEL), lambda b, i: (b * nq + i, 0)),
                pl.BlockSpec((seq, ATT_KV), lambda b, i: (b, 0)),
                pl.BlockSpec((seq, ATT_KV), lambda b, i: (b, 0))]
    args = [q, kn, vn]
    if has_cache:
        past = cache[0].shape[2]
        spec = pl.BlockSpec((1, 1, past, ATT_KV), lambda b, i: (b, layer, 0, 0))
        in_specs += [spec, spec]
        args += list(cache)
    return pl.pallas_call(
        functools.partial(_attn_kernel, has_cache=has_cache, tq=tq),
        grid=(nb, nq),
        in_specs=in_specs,
        out_specs=pl.BlockSpec((tq, D_MODEL), lambda b, i: (b * nq + i, 0)),
        out_shape=jax.ShapeDtypeStruct((t, D_MODEL), BF16),
        compiler_params=_cparams("parallel", "parallel"),
        name="gqa",
    )(*args)


def _merge_kernel(yf_ref, yb_ref, z_ref, att_ref, hf_ref, hb_ref, mo_ref,
                  g0_ref, g1_ref, g2_ref, x_ref, gate_ref, sgain_ref, mgain_ref,
                  wb_ref, wo_ref, o_ref, b3_scr):
    ys = (yf_ref[...] + yb_ref[...]) * z_ref[...].astype(F32)
    b1 = _rms(ys) * sgain_ref[...]
    for hh in range(ML_HEADS):
        sl = slice(hh * ML_HEAD_DIM, (hh + 1) * ML_HEAD_DIM)
        b3_scr[:, sl] = _rms(hf_ref[:, sl] + hb_ref[:, sl])
    b3 = (b3_scr[...] * mgain_ref[...]) * mo_ref[...].astype(F32)
    merged = (g0_ref[...].astype(F32) * _dot(b1.astype(BF16), wb_ref[0])
              + g1_ref[...].astype(F32) * _dot(att_ref[...], wb_ref[1])
              + g2_ref[...].astype(F32) * _dot(b3.astype(BF16), wb_ref[2]))
    o_ref[...] = x_ref[...] + gate_ref[0] * _dot(merged.astype(BF16), wo_ref[...])


def _merge(yf, yb, u, att, hf, hb, x, gate1, sgain, mgain, wb, wo, seq, layer):
    t = x.shape[0]
    tm = min(t, 256)
    tok = lambda i: (i, 0)
    ucol = lambda off: pl.BlockSpec((tm, D_MODEL), lambda i: (i, off // D_MODEL))
    full = pl.BlockSpec((tm, D_MODEL), tok)
    vec = pl.BlockSpec((1, D_MODEL), lambda i: (0, 0))
    once = pl.Buffered(1)
    in_specs = [full, full, ucol(U_SZ), full, full, full, ucol(U_MO),
                ucol(U_G), ucol(U_G + D_MODEL), ucol(U_G + 2 * D_MODEL), full,
                _mod_spec(gate1.shape[0], seq, tm), vec, vec,
                pl.BlockSpec((None, 3, D_MODEL, D_MODEL), lambda i: (layer, 0, 0, 0), pipeline_mode=once),
                pl.BlockSpec((None, D_MODEL, D_MODEL), lambda i: (layer, 0, 0), pipeline_mode=once)]
    return pl.pallas_call(
        _merge_kernel,
        grid=(t // tm,),
        in_specs=in_specs,
        out_specs=full,
        out_shape=jax.ShapeDtypeStruct((t, D_MODEL), F32),
        scratch_shapes=[pltpu.VMEM((tm, D_MODEL), F32)],
        compiler_params=_cparams("parallel"),
        name="branch_merge",
    )(yf, yb, u, att, hf, hb, u, u, u, u, x, gate1, sgain, mgain, wb, wo)


def _ffn_kernel(x_ref, sc_ref, sh_ref, gate_ref, wi_ref, wo_ref, o_ref):
    x = x_ref[...]
    h = (_rms(x) * (1.0 + sc_ref[0]) + sh_ref[0]).astype(BF16)
    a = _dot(h, wi_ref[:, :FFN_HIDDEN])
    b = _dot(h, wi_ref[:, FFN_HIDDEN:])
    act = (a * _sigmoid(a) * b).astype(BF16)
    o_ref[...] = x + gate_ref[0] * _dot(act, wo_ref[...])


def _ffn(x, scale, shift, gate, wi, wo, seq, layer):
    t = x.shape[0]
    tm = min(t, 256)
    nb = scale.shape[0]
    once = pl.Buffered(1)
    full = pl.BlockSpec((tm, D_MODEL), lambda i: (i, 0))
    return pl.pallas_call(
        _ffn_kernel,
        grid=(t // tm,),
        in_specs=[full, _mod_spec(nb, seq, tm), _mod_spec(nb, seq, tm), _mod_spec(nb, seq, tm),
                  pl.BlockSpec((None, D_MODEL, 2 * FFN_HIDDEN), lambda i: (layer, 0, 0), pipeline_mode=once),
                  pl.BlockSpec((None, FFN_HIDDEN, D_MODEL), lambda i: (layer, 0, 0), pipeline_mode=once)],
        out_specs=full,
        out_shape=jax.ShapeDtypeStruct((t, D_MODEL), F32),
        compiler_params=_cparams("parallel"),
        name="ffn",
    )(x, scale, shift, gate, wi, wo)


def _gate_consts():
    idx = np.arange(GATE_W)
    back = ((idx >= G_DT + SSD_HEADS) & (idx < G_LI)) | ((idx >= G_LF + ML_HEADS) & (idx < G_LF + 2 * ML_HEADS))
    r = np.arange(CHUNK)
    tlo = (r[:, None] >= r[None, :]).astype(np.float32)
    tup = (r[:, None] <= r[None, :]).astype(np.float32)
    return (jnp.asarray(back.astype(np.float32)), jnp.asarray(tlo, BF16), jnp.asarray(tup, BF16))


def _scan_consts():
    def expand(first, heads, width, parts):
        e = np.zeros((2, parts * GATE_W, heads * width), np.float32)
        for d in range(2):
            for h in range(heads):
                for part in range(parts):
                    e[d, part * GATE_W + first + d * heads + h, h * width:(h + 1) * width] = 1.0
        return jnp.asarray(e, BF16)

    ssd = (expand(G_DT, SSD_HEADS, SSD_HEAD_DIM, 1), expand(G_DT, SSD_HEADS, CHUNK, 2),
           jnp.asarray(np.eye(SSD_BC, dtype=np.float32), BF16))
    ml = (expand(G_LF, ML_HEADS, ML_HEAD_DIM, 2), expand(G_LI, ML_HEADS, ML_HEAD_DIM, 2),
          jnp.asarray(np.eye(CHUNK, dtype=np.float32), BF16))
    return ssd, ml


def _ssd_state_to_compact(s):
    lead = s.shape[:-3]
    rep = SSD_HEADS // SSD_GROUPS
    s = s.reshape(lead + (SSD_GROUPS, rep, SSD_HEAD_DIM, SSD_STATE))
    nd = len(lead)
    s = jnp.transpose(s, tuple(range(nd)) + (nd, nd + 3, nd + 1, nd + 2))
    return s.reshape(lead + (SSD_GROUPS, SSD_STATE, SSD_GROUP_W))


def _ssd_state_from_compact(s):
    lead = s.shape[:-3]
    rep = SSD_HEADS // SSD_GROUPS
    s = s.reshape(lead + (SSD_GROUPS, SSD_STATE, rep, SSD_HEAD_DIM))
    nd = len(lead)
    s = jnp.transpose(s, tuple(range(nd)) + (nd, nd + 2, nd + 3, nd + 1))
    return s.reshape(lead + (SSD_HEADS, SSD_HEAD_DIM, SSD_STATE))


def _rope_tables(seq):
    pos = np.arange(seq)
    quarter = ATT_HEAD_DIM // 4
    freqs = jnp.asarray(ROPE_THETA, F32) ** (-jnp.arange(quarter, dtype=F32) / quarter)
    ang_r = jnp.asarray(pos // GRID_W, F32)[:, None] * freqs
    ang_c = jnp.asarray(pos % GRID_W, F32)[:, None] * freqs
    cos = jnp.concatenate([jnp.cos(ang_r)] * 2 + [jnp.cos(ang_c)] * 2, axis=-1)
    sin = jnp.concatenate([-jnp.sin(ang_r), jnp.sin(ang_r), -jnp.sin(ang_c), jnp.sin(ang_c)], axis=-1)
    reps = (D_MODEL + ATT_KV) // ATT_HEAD_DIM
    return jnp.tile(cos, (1, reps)), jnp.tile(sin, (1, reps))


def _split_w_in(w_in):
    sizes = (D_MODEL, D_MODEL, SSD_BC, SSD_BC, 2 * SSD_HEADS,
             D_MODEL, ATT_KV, ATT_KV,
             D_MODEL, D_MODEL, D_MODEL, D_MODEL, 4 * ML_HEADS, 3 * D_MODEL)
    offs = np.cumsum((0,) + sizes)
    return [w_in[:, :, offs[i]:offs[i + 1]] for i in range(len(sizes))]


def kernel(x_prompt, x_sample, cache_k, cache_v, state_ssd, state_ml_c, state_ml_n, state_ml_m,
           c, c_ctx, w_ada, b_ada, w_in, ssd_conv_w, ssd_conv_b, ssd_a_log, ssd_dt_bias, ssd_d,
           ssd_norm, att_q_norm, att_k_norm, ml_conv_w, ml_conv_b, ml_gate_bias, ml_norm,
           w_branch, w_out, w_ffn_in, w_ffn_out, final_norm):
    depth = w_in.shape[0]
    nbp, seqp, _ = x_prompt.shape
    nbs, seqs, _ = x_sample.shape
    past = cache_k.shape[2]

    (s_x, s_z, s_b, s_c, s_dt, a_q, a_k, a_v, m_q, m_k, m_v, m_o, m_g, g) = _split_w_in(w_in)
    w_main = jnp.concatenate([s_x, s_b, s_c, a_q, a_k, a_v, m_q, m_k, m_v, s_z, m_o, g], axis=-1).astype(BF16)
    m_g4 = m_g.reshape(depth, D_MODEL, 2, 2, ML_HEADS)
    w_small = jnp.concatenate(
        [s_dt, m_g4[:, :, :, 0].reshape(depth, D_MODEL, 2 * ML_HEADS),
         m_g4[:, :, :, 1].reshape(depth, D_MODEL, 2 * ML_HEADS),
         jnp.zeros((depth, D_MODEL, GATE_W - G_LF - 2 * ML_HEADS), F32)], axis=-1).astype(BF16)
    w_small_t = jnp.swapaxes(w_small, 1, 2)
    pad = jnp.zeros((depth, GATE_W - G_LF - 2 * ML_HEADS), F32)
    gate_bias = jnp.concatenate(
        [ssd_dt_bias.reshape(depth, 2 * SSD_HEADS), ml_gate_bias[:, :, 0].reshape(depth, 2 * ML_HEADS),
         ml_gate_bias[:, :, 1].reshape(depth, 2 * ML_HEADS), pad], axis=-1)
    a_vec = jnp.concatenate(
        [-jnp.exp(ssd_a_log.reshape(depth, 2 * SSD_HEADS)), jnp.zeros((depth, GATE_W - 2 * SSD_HEADS), F32)], axis=-1)
    d_vec = jnp.repeat(ssd_d, SSD_HEAD_DIM, axis=-1).reshape(depth, 1, D_MODEL)
    qk_gain = jnp.concatenate([jnp.tile(att_q_norm, (1, ATT_HEADS)), jnp.tile(att_k_norm, (1, ATT_KV_HEADS))], axis=-1)
    qk_gain = qk_gain.reshape(depth, 1, D_MODEL + ATT_KV)
    w_branch_b = w_branch.astype(BF16)
    w_out_b = w_out.astype(BF16)
    w_ffn_in_b = w_ffn_in.astype(BF16)
    w_ffn_out_b = w_ffn_out.astype(BF16)
    consts = _gate_consts()
    tables = _rope_tables(seqs)
    cache = (cache_k.reshape(nbs, depth, past, ATT_KV), cache_v.reshape(nbs, depth, past, ATT_KV))
    ssd_consts, ml_consts = _scan_consts()
    ssd_h0 = _ssd_state_to_compact(state_ssd)
    ml_cn0 = jnp.concatenate(
        [state_ml_c, jnp.broadcast_to(state_ml_n[..., None], state_ml_c.shape)], axis=-1)
    ml_m0 = jnp.repeat(state_ml_m, ML_HEAD_DIM, axis=-1).reshape(nbs, depth, 2, 1, D_MODEL)

    rows = 8 * ((1 + nbs + 7) // 8)
    cc = jnp.concatenate([c_ctx[None], c, jnp.zeros((rows - 1 - nbs, D_MODEL), F32)], axis=0)
    mod = _ada(cc, w_ada, b_ada)

    def mods(layer, lo, hi):
        m = mod[layer, lo:hi].reshape(hi - lo, 1, 6, D_MODEL)
        return [m[:, :, i] for i in range(6)]

    def layer_step(x, layer, nb, seq, mod6, ctx):
        shift1, scale1, gate1, shift2, scale2, gate2 = mod6
        h = _norm_mod(x, scale1, shift1, seq)
        u = _proj(h, w_main, layer)
        gates = _gate(h, w_small[layer], w_small_t[layer], gate_bias[layer], a_vec[layer], consts)
        sx = _conv(u, U_CONV, D_MODEL, ssd_conv_w[layer, :, :D_MODEL], ssd_conv_b[layer, :D_MODEL], nb, seq, F32)
        sbc = _conv(u, U_CONV + D_MODEL, 2 * SSD_BC, ssd_conv_w[layer, :, D_MODEL:], ssd_conv_b[layer, D_MODEL:],
                    nb, seq, BF16)
        mqk = _conv(u, U_MQ, 2 * D_MODEL, ml_conv_w[layer], ml_conv_b[layer], nb, seq, BF16)
        if ctx:
            yf, yb = _ssd(sx, sbc, gates, ssd_consts, d_vec[layer], nb, seq, ssd_h0, layer)
            q, kn, vn = _qkprep(u, qk_gain[layer], tables, nb, seq, False)
            att = _attn(q, kn, vn, cache, nb, seq, layer)
            hf, hb = _mlstm(mqk, u, gates, ml_consts, nb, seq, (ml_cn0, ml_m0), layer)
            new = None
        else:
            yf, yb, hst = _ssd(sx, sbc, gates, ssd_consts, d_vec[layer], nb, seq, None, layer)
            q, kn, vn, kf, vf = _qkprep(u, qk_gain[layer], None, nb, seq, True)
            att = _attn(q, kn, vn, None, nb, seq, layer)
            hf, hb, cn, mrow = _mlstm(mqk, u, gates, ml_consts, nb, seq, None, layer)
            new = (kf.reshape(nb, seq, ATT_KV_HEADS, ATT_HEAD_DIM), vf.reshape(nb, seq, ATT_KV_HEADS, ATT_HEAD_DIM),
                   _ssd_state_from_compact(hst), cn[..., :ML_HEAD_DIM], cn[..., ML_HEAD_DIM],
                   mrow[:, :, 0, ::ML_HEAD_DIM])
        x = _merge(yf, yb, u, att, hf, hb, x, gate1,
                   ssd_norm[layer].reshape(1, D_MODEL), ml_norm[layer].reshape(1, D_MODEL),
                   w_branch_b, w_out_b, seq, layer)
        x = _ffn(x, scale2, shift2, gate2, w_ffn_in_b, w_ffn_out_b, seq, layer)
        return x, new

    y_p = x_prompt.reshape(nbp * seqp, D_MODEL)
    y_s = x_sample.reshape(nbs * seqs, D_MODEL)
    news = []
    for layer in range(depth):
        y_p, new = layer_step(y_p, layer, nbp, seqp, mods(layer, 0, 1), False)
        news.append(new)
        y_s, _ = layer_step(y_s, layer, nbs, seqs, mods(layer, 1, 1 + nbs), True)

    y_prompt = _norm_gain(y_p, final_norm).reshape(x_prompt.shape)
    y_sample = _norm_gain(y_s, final_norm).reshape(x_sample.shape)
    stacked = [jnp.stack([n[i] for n in news], axis=1) for i in range(6)]
    return (y_prompt, y_sample) + tuple(stacked)
```

```python
import functools

import numpy as np
import jax
import jax.numpy as jnp
from jax import lax
from jax.experimental import pallas as pl
from jax.experimental.pallas import tpu as pltpu

F32 = jnp.float32
BF16 = jnp.bfloat16

D_MODEL = 1024
GRID_W = 64
EPS = 1e-6
CONV_W = 4
CHUNK = 128
SSD_HEADS = 16
SSD_HEAD_DIM = 64
SSD_GROUPS = 4
SSD_STATE = 64
SSD_BC = SSD_GROUPS * SSD_STATE
SSD_CONV = D_MODEL + 2 * SSD_BC
ATT_HEADS = 16
ATT_HEAD_DIM = 64
ATT_KV_HEADS = 4
ATT_KV = ATT_KV_HEADS * ATT_HEAD_DIM
ATT_QKV = D_MODEL + 2 * ATT_KV
ATT_KV_PAD = ATT_KV_HEADS * 2 * ATT_HEAD_DIM
Q_SCALE = ATT_HEAD_DIM ** -0.5 * 1.4426950408889634
ROPE_THETA = 10000.0
ML_HEADS = 8
ML_HEAD_DIM = 128
FFN_HIDDEN = 2816
GATE_W = 128

U_CONV = 0
U_ATT = 1536
U_MQ = 3072
U_MK = 4096
U_MV = 5120
U_SZ = 6144
U_MO = 7168
U_G = 8192
U_W = 11264

G_DT = 0
G_LI = 32
G_LF = 48

VMEM_LIMIT = 48 * 1024 * 1024
NT_DIMS = (((1,), (1,)), ((), ()))
TN_DIMS = (((0,), (0,)), ((), ()))


def _cparams(*sem):
    return pltpu.CompilerParams(dimension_semantics=sem, vmem_limit_bytes=VMEM_LIMIT)


def _sigmoid(x):
    return 1.0 / (1.0 + jnp.exp(-x))


def _softplus(x):
    return jnp.maximum(x, 0.0) + jnp.log1p(jnp.exp(-jnp.abs(x)))


def _dot(a, b):
    return jnp.dot(a, b, preferred_element_type=F32)


def _dot_nt(a, b):
    return lax.dot_general(a, b, NT_DIMS, preferred_element_type=F32)


def _dot_tn(a, b):
    return lax.dot_general(a, b, TN_DIMS, preferred_element_type=F32)


def _ada_kernel(c_ref, w_ref, b_ref, o_ref):
    c = c_ref[...]
    s = (c * _sigmoid(c)).astype(BF16)
    o_ref[0] = _dot(s, w_ref[0].astype(BF16)) + b_ref[0]


def _ada(cc, w_ada, b_ada):
    depth, _, n6 = w_ada.shape
    rows = cc.shape[0]
    tn = 512
    return pl.pallas_call(
        _ada_kernel,
        grid=(depth, n6 // tn),
        in_specs=[pl.BlockSpec((rows, D_MODEL), lambda l, j: (0, 0)),
                  pl.BlockSpec((1, D_MODEL, tn), lambda l, j: (l, 0, j)),
                  pl.BlockSpec((1, 1, tn), lambda l, j: (l, 0, j))],
        out_specs=pl.BlockSpec((1, rows, tn), lambda l, j: (l, 0, j)),
        out_shape=jax.ShapeDtypeStruct((depth, rows, n6), F32),
        compiler_params=_cparams("parallel", "parallel"),
        name="ada_mod",
    )(cc, w_ada, b_ada.reshape(depth, 1, n6))


def _rms(x):
    return x * lax.rsqrt(jnp.mean(x * x, axis=-1, keepdims=True) + EPS)


def _norm_mod_kernel(x_ref, sc_ref, sh_ref, o_ref):
    o_ref[...] = (_rms(x_ref[...]) * (1.0 + sc_ref[0]) + sh_ref[0]).astype(o_ref.dtype)


def _norm_gain_kernel(x_ref, g_ref, o_ref):
    o_ref[...] = _rms(x_ref[...]) * g_ref[...]


def _mod_spec(nb, seq, tm):
    if nb == 1:
        return pl.BlockSpec((1, 1, D_MODEL), lambda i: (0, 0, 0))
    return pl.BlockSpec((1, 1, D_MODEL), lambda i: ((i * tm) // seq, 0, 0))


def _norm_mod(x, scale, shift, seq):
    t = x.shape[0]
    tm = min(t, 512)
    nb = scale.shape[0]
    return pl.pallas_call(
        _norm_mod_kernel,
        grid=(t // tm,),
        in_specs=[pl.BlockSpec((tm, D_MODEL), lambda i: (i, 0)),
                  _mod_spec(nb, seq, tm), _mod_spec(nb, seq, tm)],
        out_specs=pl.BlockSpec((tm, D_MODEL), lambda i: (i, 0)),
        out_shape=jax.ShapeDtypeStruct((t, D_MODEL), BF16),
        compiler_params=_cparams("parallel"),
        name="norm_mod",
    )(x, scale, shift)


def _norm_gain(x, gain):
    t = x.shape[0]
    tm = min(t, 512)
    return pl.pallas_call(
        _norm_gain_kernel,
        grid=(t // tm,),
        in_specs=[pl.BlockSpec((tm, D_MODEL), lambda i: (i, 0)),
                  pl.BlockSpec((1, D_MODEL), lambda i: (0, 0))],
        out_specs=pl.BlockSpec((tm, D_MODEL), lambda i: (i, 0)),
        out_shape=jax.ShapeDtypeStruct((t, D_MODEL), F32),
        compiler_params=_cparams("parallel"),
        name="final_norm",
    )(x, gain.reshape(1, D_MODEL))


def _sigmoid_tanh(x):
    return 0.5 * jnp.tanh(0.5 * x) + 0.5


def _proj_kernel(a_ref, w_ref, o_ref, *, silu_lo, sig_lo, sub):
    j = pl.program_id(1)
    w = w_ref[...]

    def run(act):
        for r in range(a_ref.shape[0] // sub):
            rows = slice(r * sub, (r + 1) * sub)
            o_ref[rows, :] = act(_dot(a_ref[rows, :], w)).astype(o_ref.dtype)

    @pl.when(j < silu_lo)
    def _():
        run(lambda acc: acc)

    @pl.when(jnp.logical_and(j >= silu_lo, j < sig_lo))
    def _():
        run(lambda acc: acc * _sigmoid_tanh(acc))

    @pl.when(j >= sig_lo)
    def _():
        run(_sigmoid_tanh)


def _proj(h, w_main, layer):
    t = h.shape[0]
    tm = min(t, 2048)
    tn = 1024
    kern = functools.partial(_proj_kernel, silu_lo=U_SZ // tn, sig_lo=U_MO // tn, sub=min(tm, 512))
    return pl.pallas_call(
        kern,
        grid=(t // tm, U_W // tn),
        in_specs=[pl.BlockSpec((tm, D_MODEL), lambda i, j: (i, 0)),
                  pl.BlockSpec((None, D_MODEL, tn), lambda i, j: (layer, 0, j))],
        out_specs=pl.BlockSpec((tm, tn), lambda i, j: (i, j)),
        out_shape=jax.ShapeDtypeStruct((t, U_W), BF16),
        compiler_params=_cparams("parallel", "parallel"),
        name="in_proj",
    )(h, w_main)


def _split3(x):
    hi = x.astype(BF16)
    r1 = x - hi.astype(F32)
    mid = r1.astype(BF16)
    lo = (r1 - mid.astype(F32)).astype(BF16)
    return hi, mid, lo


def _tri_left(tri, x):
    hi, mid, lo = _split3(x)
    return _dot(tri, hi) + _dot(tri, mid) + _dot(tri, lo)


def _tri_right(x, tri):
    hi, mid, lo = _split3(x)
    return _dot(hi, tri) + _dot(mid, tri) + _dot(lo, tri)


def _gate_vals(u, idx, a):
    sp = _softplus(u)
    ls = -_softplus(-u)
    val = jnp.where(idx < G_LI, sp, jnp.where(idx < G_LF, u, jnp.where(idx < G_LF + 16, ls, 0.0)))
    cin = jnp.where(idx < G_LI, sp * a, jnp.where(jnp.logical_and(idx >= G_LF, idx < G_LF + 16), ls, 0.0))
    return val, cin


def _gate_kernel(h_ref, ws_ref, wst_ref, b_ref, bt_ref, a_ref, at_ref, dm_ref, dmt_ref,
                 tlo_ref, tup_ref, val_ref, cum_ref, valt_ref, cumt_ref):
    h = h_ref[...]
    tlo = tlo_ref[...]
    tup = tup_ref[...]
    rows = h.shape[0]
    u = _dot(h, ws_ref[...]) + b_ref[...]
    val, cin = _gate_vals(u, lax.broadcasted_iota(jnp.int32, (rows, GATE_W), 1), a_ref[...])
    val_ref[...] = val
    ut = _dot_nt(wst_ref[...], h) + bt_ref[...]
    valt, cint = _gate_vals(ut, lax.broadcasted_iota(jnp.int32, (GATE_W, rows), 0), at_ref[...])
    valt_ref[...] = valt
    back = dm_ref[...] > 0.5
    back_t = dmt_ref[...] > 0.5
    for c in range(rows // CHUNK):
        sl = slice(c * CHUNK, (c + 1) * CHUNK)
        cum_ref[sl, :] = jnp.where(back, _tri_left(tup, cin[sl]), _tri_left(tlo, cin[sl]))
        cumt_ref[:, sl] = jnp.where(back_t, _tri_right(cint[:, sl], tlo), _tri_right(cint[:, sl], tup))


def _gate(h, ws, wst, bias, avec, consts):
    t = h.shape[0]
    tm = min(t, 4 * CHUNK)
    dmask, tlo, tup = consts
    row = lambda i: (i, 0)
    col = lambda i: (0, i)
    fix = lambda i: (0, 0)
    tm_out = jax.ShapeDtypeStruct((t, GATE_W), F32)
    fm_out = jax.ShapeDtypeStruct((GATE_W, t), F32)
    return pl.pallas_call(
        _gate_kernel,
        grid=(t // tm,),
        in_specs=[pl.BlockSpec((tm, D_MODEL), row),
                  pl.BlockSpec((D_MODEL, GATE_W), fix),
                  pl.BlockSpec((GATE_W, D_MODEL), fix),
                  pl.BlockSpec((1, GATE_W), fix), pl.BlockSpec((GATE_W, 1), fix),
                  pl.BlockSpec((1, GATE_W), fix), pl.BlockSpec((GATE_W, 1), fix),
                  pl.BlockSpec((1, GATE_W), fix), pl.BlockSpec((GATE_W, 1), fix),
                  pl.BlockSpec((CHUNK, CHUNK), fix), pl.BlockSpec((CHUNK, CHUNK), fix)],
        out_specs=[pl.BlockSpec((tm, GATE_W), row), pl.BlockSpec((tm, GATE_W), row),
                   pl.BlockSpec((GATE_W, tm), col), pl.BlockSpec((GATE_W, tm), col)],
        out_shape=[tm_out, tm_out, fm_out, fm_out],
        compiler_params=_cparams("parallel"),
        name="gate_prep",
    )(h, ws, wst, bias.reshape(1, GATE_W), bias.reshape(GATE_W, 1),
      avec.reshape(1, GATE_W), avec.reshape(GATE_W, 1),
      dmask.reshape(1, GATE_W), dmask.reshape(GATE_W, 1), tlo, tup)


def _conv_kernel(x_ref, w_ref, b_ref, o_ref):
    x = x_ref[...].astype(F32)
    seq = x.shape[0]
    row = lax.broadcasted_iota(jnp.int32, x.shape, 0)
    acc = x * w_ref[1:2, :] + b_ref[...]
    acc = acc + jnp.where(row >= 1, pltpu.roll(x, 1, 0), 0.0) * w_ref[0:1, :]
    acc = acc + jnp.where(row < seq - 1, pltpu.roll(x, seq - 1, 0), 0.0) * w_ref[2:3, :]
    acc = acc + jnp.where(row < seq - 2, pltpu.roll(x, seq - 2, 0), 0.0) * w_ref[3:4, :]
    o_ref[...] = (acc * _sigmoid_tanh(acc)).astype(o_ref.dtype)


def _conv(u, col0, width, w, b, nb, seq, out_dtype):
    tc = 256
    c0 = col0 // tc
    return pl.pallas_call(
        _conv_kernel,
        grid=(nb, width // tc),
        in_specs=[pl.BlockSpec((seq, tc), lambda i, j: (i, c0 + j)),
                  pl.BlockSpec((CONV_W, tc), lambda i, j: (0, j)),
                  pl.BlockSpec((1, tc), lambda i, j: (0, j))],
        out_specs=pl.BlockSpec((seq, tc), lambda i, j: (i, j)),
        out_shape=jax.ShapeDtypeStruct((nb * seq, width), out_dtype),
        compiler_params=_cparams("parallel", "parallel"),
        name="dwconv_silu",
    )(u, w, b.reshape(1, width))


def _tri_mask(fwd):
    row = lax.broadcasted_iota(jnp.int32, (CHUNK, CHUNK), 0)
    col = lax.broadcasted_iota(jnp.int32, (CHUNK, CHUNK), 1)
    return row >= col if fwd else row <= col


def _lane_expand(x, e):
    hi, mid, _ = _split3(x)
    return _dot(jnp.concatenate([hi, mid], axis=1), e)


SSD_GROUP_W = (SSD_HEADS // SSD_GROUPS) * SSD_HEAD_DIM


def _ssd_dir(d, x_ref, bc_ref, val_ref, cum_ref, valt_ref, cumt_ref, e64_ref, e128_ref, eye_ref, skip,
             y_ref, st_scr):
    mask = _tri_mask(d == 0)
    end = CHUNK - 1 if d == 0 else 0
    x = x_ref[...]
    bb = bc_ref[:, :SSD_BC]
    cm = bc_ref[:, SSD_BC:]
    lane_c = lax.broadcasted_iota(jnp.int32, cm.shape, 1)
    lane_x = lax.broadcasted_iota(jnp.int32, (CHUNK, 2 * SSD_HEAD_DIM), 1)
    a128 = _lane_expand(cum_ref[...], e128_ref[d])
    a64 = jnp.concatenate(
        [jnp.where(lane_x < SSD_HEAD_DIM, a128[:, (2 * p) * CHUNK:(2 * p + 1) * CHUNK],
                   a128[:, (2 * p + 1) * CHUNK:(2 * p + 2) * CHUNK]) for p in range(SSD_HEADS // 2)], axis=1)
    dt64 = _dot(val_ref[...].astype(BF16), e64_ref[d])
    dt_r = valt_ref[...]
    ac_r = cumt_ref[...]
    st = st_scr[d]
    inter = _dot(cm, st.astype(BF16))
    ea = jnp.exp(a64)
    cm32 = cm.astype(F32)
    c_stack = jnp.concatenate(
        [jnp.where(jnp.logical_and(lane_c >= g * SSD_STATE, lane_c < (g + 1) * SSD_STATE), cm32, 0.0).astype(BF16)
         for g in range(SSD_GROUPS)], axis=0)
    cbt_all = _dot_nt(c_stack, bb)
    bt = _dot_nt(eye_ref[...], bb).astype(BF16)
    yield
    tot = a64[end:end + 1, :]
    xw = (x * (jnp.exp(tot - a64) * dt64)).astype(BF16)
    decay = jnp.exp(tot)
    for g in range(SSD_GROUPS):
        rs = slice(g * SSD_STATE, (g + 1) * SSD_STATE)
        cs = slice(g * SSD_GROUP_W, (g + 1) * SSD_GROUP_W)
        st_scr[d, rs, cs] = decay[:, cs] * st[rs, cs] + _dot(bt[rs, :], xw[:, cs])
    yield
    n_pairs = SSD_HEADS // 2
    weights, xbds = [], []
    for p in range(n_pairs):
        cbt = cbt_all[(p // 2) * CHUNK:(p // 2 + 1) * CHUNK]
        parts = []
        for hh in (2 * p, 2 * p + 1):
            lane = G_DT + d * SSD_HEADS + hh
            seg = jnp.where(mask, a128[:, hh * CHUNK:(hh + 1) * CHUNK] - ac_r[lane:lane + 1, :], -jnp.inf)
            parts.append((cbt * jnp.exp(seg) * dt_r[lane:lane + 1, :]).astype(BF16))
        weights.append(jnp.concatenate(parts, axis=1))
        xp = x[:, p * 2 * SSD_HEAD_DIM:(p + 1) * 2 * SSD_HEAD_DIM]
        xbds.append(jnp.concatenate([jnp.where(lane_x < SSD_HEAD_DIM, xp, 0.0),
                                     jnp.where(lane_x >= SSD_HEAD_DIM, xp, 0.0)], axis=0).astype(BF16))
    yield
    for p in range(n_pairs):
        cs = slice(p * 2 * SSD_HEAD_DIM, (p + 1) * 2 * SSD_HEAD_DIM)
        y = _dot(weights[p], xbds[p]) + ea[:, cs] * inter[:, cs]
        if skip is not None:
            y = y + skip[:, cs] * x[:, cs]
        y_ref[:, cs] = y


def _ssd_kernel(*refs, has_h0, emit_state):
    xf_ref, bcf_ref, xb_ref, bcb_ref = refs[0:4]
    gf = refs[4:8]
    gb = refs[8:12]
    consts = refs[12:15]
    dvec_ref = refs[15]
    k = 16
    if has_h0:
        h0_ref = refs[k]
        k += 1
    yf_ref, yb_ref = refs[k:k + 2]
    k += 2
    if emit_state:
        hout_ref = refs[k]
        k += 1
    st_scr = refs[k]
    c = pl.program_id(1)

    @pl.when(c == 0)
    def _():
        st_scr[...] = jnp.zeros(st_scr.shape, F32)
        if has_h0:
            for d in range(2):
                for g in range(SSD_GROUPS):
                    st_scr[d, g * SSD_STATE:(g + 1) * SSD_STATE,
                           g * SSD_GROUP_W:(g + 1) * SSD_GROUP_W] = h0_ref[0, 0, d, g]

    _interleave(_ssd_dir(0, xf_ref, bcf_ref, *gf, *consts, dvec_ref[...], yf_ref, st_scr),
                _ssd_dir(1, xb_ref, bcb_ref, *gb, *consts, None, yb_ref, st_scr))

    if emit_state:
        @pl.when(c == pl.num_programs(1) - 1)
        def _():
            for d in range(2):
                for g in range(SSD_GROUPS):
                    hout_ref[0, d, g] = st_scr[d, g * SSD_STATE:(g + 1) * SSD_STATE,
                                               g * SSD_GROUP_W:(g + 1) * SSD_GROUP_W]


def _chunk_specs(nc, width):
    fwd = pl.BlockSpec((CHUNK, width), lambda b, c: (b * nc + c, 0))
    bwd = pl.BlockSpec((CHUNK, width), lambda b, c: (b * nc + nc - 1 - c, 0))
    return fwd, bwd


def _gate_specs(nc):
    tf, tb = _chunk_specs(nc, GATE_W)
    ff = pl.BlockSpec((GATE_W, CHUNK), lambda b, c: (0, b * nc + c))
    fb = pl.BlockSpec((GATE_W, CHUNK), lambda b, c: (0, b * nc + nc - 1 - c))
    return [tf, tf, ff, ff], [tb, tb, fb, fb]


def _const_spec(arr):
    nd = arr.ndim
    return pl.BlockSpec(arr.shape, lambda b, c: (0,) * nd)


def _ssd(x, bc, gates, consts, dvec, nb, seq, state, layer):
    nc = seq // CHUNK
    t = nb * seq
    has_h0 = state is not None
    xf, xb = _chunk_specs(nc, D_MODEL)
    bcf, bcb = _chunk_specs(nc, 2 * SSD_BC)
    gfs, gbs = _gate_specs(nc)
    yf, yb = _chunk_specs(nc, D_MODEL)
    st_shape = (2, SSD_GROUPS, SSD_STATE, SSD_GROUP_W)
    in_specs = [xf, bcf, xb, bcb] + gfs + gbs + [_const_spec(a) for a in consts] + [_const_spec(dvec)]
    args = [x, bc, x, bc] + list(gates) + list(gates) + list(consts) + [dvec]
    if has_h0:
        in_specs.append(pl.BlockSpec((1, 1) + st_shape, lambda b, c: (b, layer, 0, 0, 0, 0)))
        args.append(state)
    out_specs = [yf, yb]
    out_shape = [jax.ShapeDtypeStruct((t, D_MODEL), F32)] * 2
    if not has_h0:
        out_specs.append(pl.BlockSpec((1,) + st_shape, lambda b, c: (b, 0, 0, 0, 0)))
        out_shape.append(jax.ShapeDtypeStruct((nb,) + st_shape, F32))
    return pl.pallas_call(
        functools.partial(_ssd_kernel, has_h0=has_h0, emit_state=not has_h0),
        grid=(nb, nc),
        in_specs=in_specs,
        out_specs=out_specs,
        out_shape=out_shape,
        scratch_shapes=[pltpu.VMEM((2, SSD_BC, D_MODEL), F32)],
        compiler_params=_cparams("parallel", "arbitrary"),
        name="ssd_scan",
    )(*args)


def _ml_dir(d, q_ref, k_ref, v_ref, val_ref, cum_ref, valt_ref, cumt_ref, eb_ref, eye_ref,
            y_ref, cn_scr, m_scr):
    mask = _tri_mask(d == 0)
    end = CHUNK - 1 if d == 0 else 0
    heads = range(ML_HEADS)
    sls = [slice(hh * ML_HEAD_DIM, (hh + 1) * ML_HEAD_DIM) for hh in heads]
    kscale = ML_HEAD_DIM ** -0.5
    b128 = _lane_expand(cum_ref[...], eb_ref[d])
    li_r = valt_ref[...]
    bc_r = cumt_ref[...]
    m_row = m_scr[d]
    q = q_ref[...]
    k = k_ref[...]
    v = v_ref[...]
    ones = jnp.ones((CHUNK, ML_HEAD_DIM), BF16)
    eye = eye_ref[...]
    v1 = [jnp.concatenate([v[:, sl], ones], axis=1) for sl in sls]
    cn = [cn_scr[d, hh] for hh in heads]
    kt = [_dot_nt(eye, k[:, sl]) for sl in sls]
    qk = [_dot_nt(q[:, sl], (k[:, sl].astype(F32) * kscale).astype(BF16)) for sl in sls]
    qc = [_dot(q[:, sl], cn[hh].astype(BF16)) for hh, sl in zip(heads, sls)]
    yield
    b_end = b128[end:end + 1, :]
    bm = b128 + m_row
    kwt, dch, m_parts, s, inter, m_t = [], [], [], [], [], []
    for hh, sl in zip(heads, sls):
        b_j = bc_r[G_LF + d * ML_HEADS + hh:G_LF + d * ML_HEADS + hh + 1, :]
        i_j = li_r[G_LI + d * ML_HEADS + hh:G_LI + d * ML_HEADS + hh + 1, :]
        wj = b_end[:, sl] - b_j + i_j
        m_new = jnp.maximum(b_end[:, sl] + m_row[:, sl], jnp.max(wj, axis=1, keepdims=True))
        kwt.append((kt[hh] * (jnp.exp(wj - m_new) * kscale)).astype(BF16))
        dch.append(jnp.exp(b_end[:, sl] + m_row[:, sl] - m_new))
        m_parts.append(m_new)
        dmat = jnp.where(mask, b128[:, sl] - b_j + i_j, -jnp.inf)
        mt = jnp.maximum(bm[:, sl], jnp.max(dmat, axis=1, keepdims=True))
        s.append((qk[hh] * jnp.exp(dmat - mt)).astype(BF16))
        inter.append(jnp.exp(bm[:, sl] - mt))
        m_t.append(mt)
    yield
    upd = [_dot(kwt[hh], v1[hh]) for hh in heads]
    sv = [_dot(s[hh], v1[hh]) for hh in heads]
    yield
    for hh, sl in zip(heads, sls):
        r = sv[hh] + jnp.concatenate([inter[hh], inter[hh]], axis=1) * qc[hh]
        y_ref[:, sl] = r[:, :ML_HEAD_DIM] / jnp.maximum(jnp.abs(r[:, ML_HEAD_DIM:]), jnp.exp(-m_t[hh]))
        cn_scr[d, hh] = jnp.concatenate([dch[hh], dch[hh]], axis=1) * cn[hh] + upd[hh]
    m_scr[d] = jnp.concatenate(m_parts, axis=1)


def _interleave(*gens):
    live = list(gens)
    while live:
        still = []
        for g in live:
            try:
                next(g)
                still.append(g)
            except StopIteration:
                pass
        live = still


def _ml_kernel(*refs, has_init, emit_state):
    qf_ref, kf_ref, vf_ref, qb_ref, kb_ref, vb_ref = refs[0:6]
    gf = refs[6:10]
    gb = refs[10:14]
    consts = refs[14:16]
    k = 16
    if has_init:
        cn0_ref, m0_ref = refs[k:k + 2]
        k += 2
    yf_ref, yb_ref = refs[k:k + 2]
    k += 2
    if emit_state:
        cnout_ref, mout_ref = refs[k:k + 2]
        k += 2
    cn_scr, m_scr = refs[k:k + 2]
    c = pl.program_id(1)

    @pl.when(c == 0)
    def _():
        if has_init:
            cn_scr[...] = cn0_ref[0, 0]
            m_scr[...] = m0_ref[0, 0]
        else:
            cn_scr[...] = jnp.zeros(cn_scr.shape, F32)
            m_scr[...] = jnp.zeros(m_scr.shape, F32)

    _interleave(_ml_dir(0, qf_ref, kf_ref, vf_ref, *gf, *consts, yf_ref, cn_scr, m_scr),
                _ml_dir(1, qb_ref, kb_ref, vb_ref, *gb, *consts, yb_ref, cn_scr, m_scr))

    if emit_state:
        @pl.when(c == pl.num_programs(1) - 1)
        def _():
            cnout_ref[0] = cn_scr[...]
            mout_ref[0] = m_scr[...]


def _mlstm(qk, u, gates, consts, nb, seq, states, layer):
    nc = seq // CHUNK
    t = nb * seq
    has_init = states is not None
    mvb = U_MV // D_MODEL

    def tok(col, rev):
        if rev:
            return pl.BlockSpec((CHUNK, D_MODEL), lambda b, c: (b * nc + nc - 1 - c, col))
        return pl.BlockSpec((CHUNK, D_MODEL), lambda b, c: (b * nc + c, col))

    gfs, gbs = _gate_specs(nc)
    in_specs = ([tok(0, False), tok(1, False), tok(mvb, False),
                 tok(0, True), tok(1, True), tok(mvb, True)] + gfs + gbs
                + [_const_spec(a) for a in consts])
    args = [qk, qk, u, qk, qk, u] + list(gates) + list(gates) + list(consts)
    cn_shape = (2, ML_HEADS, ML_HEAD_DIM, 2 * ML_HEAD_DIM)
    m_shape = (2, 1, D_MODEL)
    if has_init:
        in_specs += [pl.BlockSpec((1, 1) + cn_shape, lambda b, c: (b, layer, 0, 0, 0, 0)),
                     pl.BlockSpec((1, 1) + m_shape, lambda b, c: (b, layer, 0, 0, 0))]
        args += list(states)
    out_specs = [tok(0, False), tok(0, True)]
    out_shape = [jax.ShapeDtypeStruct((t, D_MODEL), F32)] * 2
    if not has_init:
        out_specs += [pl.BlockSpec((1,) + cn_shape, lambda b, c: (b, 0, 0, 0, 0)),
                      pl.BlockSpec((1,) + m_shape, lambda b, c: (b, 0, 0, 0))]
        out_shape += [jax.ShapeDtypeStruct((nb,) + cn_shape, F32),
                      jax.ShapeDtypeStruct((nb,) + m_shape, F32)]
    return pl.pallas_call(
        functools.partial(_ml_kernel, has_init=has_init, emit_state=not has_init),
        grid=(nb, nc),
        in_specs=in_specs,
        out_specs=out_specs,
        out_shape=out_shape,
        scratch_shapes=[pltpu.VMEM(cn_shape, F32), pltpu.VMEM(m_shape, F32)],
        compiler_params=_cparams("parallel", "arbitrary"),
        name="mlstm_scan",
    )(*args)


def _qkprep_kernel(*refs, rope, emit_f32):
    u_ref, gain_ref = refs[0:2]
    k = 2
    if rope:
        cos_ref, sin_ref = refs[k:k + 2]
        k += 2
    q_ref, klo_ref, khi_ref, v1_ref = refs[k:k + 4]
    k += 4
    if emit_f32:
        kf_ref, vf_ref = refs[k:k + 2]
        k += 2
    nqk = D_MODEL + ATT_KV
    pair = 2 * ATT_HEAD_DIM
    low = lax.broadcasted_iota(jnp.int32, (u_ref.shape[0], pair), 1) < ATT_HEAD_DIM
    parts = []
    for p in range(nqk // pair):
        x = u_ref[:, p * pair:(p + 1) * pair].astype(F32)
        x2 = x * x
        ss = jnp.where(low, jnp.sum(jnp.where(low, x2, 0.0), axis=-1, keepdims=True),
                       jnp.sum(jnp.where(low, 0.0, x2), axis=-1, keepdims=True))
        parts.append(x * lax.rsqrt(ss * (1.0 / ATT_HEAD_DIM) + EPS))
    xn = jnp.concatenate(parts, axis=1) * gain_ref[...]
    v = u_ref[:, nqk:nqk + ATT_KV]
    if emit_f32:
        kf_ref[...] = xn[:, D_MODEL:nqk]
        vf_ref[...] = v.astype(F32)
    if rope:
        lane = lax.broadcasted_iota(jnp.int32, xn.shape, 1)
        quarter = ATT_HEAD_DIM // 4
        first = jnp.bitwise_and(lane, 2 * quarter - 1) < quarter
        swapped = jnp.where(first, pltpu.roll(xn, nqk - quarter, 1), pltpu.roll(xn, quarter, 1))
        xn = xn * cos_ref[...] + swapped * sin_ref[...]
    q_ref[...] = (xn[:, :D_MODEL] * Q_SCALE).astype(BF16)

    def spread(a, fill):
        lo_parts, hi_parts = [], []
        for c in (a[:, :pair], a[:, pair:]):
            r = pltpu.roll(c, ATT_HEAD_DIM, 1)
            lo_parts += [jnp.where(low, c, fill), jnp.where(low, r, fill)]
            hi_parts += [jnp.where(low, fill, r), jnp.where(low, fill, c)]
        return jnp.concatenate(lo_parts, axis=1), jnp.concatenate(hi_parts, axis=1)

    k_lo, k_hi = spread(xn[:, D_MODEL:nqk], 0.0)
    klo_ref[...] = k_lo.astype(BF16)
    khi_ref[...] = k_hi.astype(BF16)
    v1_ref[...] = spread(v.astype(F32), 1.0)[0].astype(BF16)


def _qkprep(u, gain, tables, nb, seq, emit_f32):
    t = nb * seq
    tl = min(seq, 256)
    nl = seq // tl
    nqk = D_MODEL + ATT_KV
    rope = tables is not None
    ub = U_ATT // ATT_QKV
    tok = lambda w: pl.BlockSpec((tl, w), lambda i, b: (b * nl + i, 0))
    in_specs = [pl.BlockSpec((tl, ATT_QKV), lambda i, b: (b * nl + i, ub)),
                pl.BlockSpec((1, nqk), lambda i, b: (0, 0))]
    args = [u, gain]
    if rope:
        in_specs += [pl.BlockSpec((tl, nqk), lambda i, b: (i, 0))] * 2
        args += list(tables)
    out_specs = [tok(D_MODEL), tok(ATT_KV_PAD), tok(ATT_KV_PAD), tok(ATT_KV_PAD)]
    out_shape = [jax.ShapeDtypeStruct((t, D_MODEL), BF16)] + [jax.ShapeDtypeStruct((t, ATT_KV_PAD), BF16)] * 3
    if emit_f32:
        out_specs += [tok(ATT_KV), tok(ATT_KV)]
        out_shape += [jax.ShapeDtypeStruct((t, ATT_KV), F32)] * 2
    return pl.pallas_call(
        functools.partial(_qkprep_kernel, rope=rope, emit_f32=emit_f32),
        grid=(nl, nb),
        in_specs=in_specs,
        out_specs=out_specs,
        out_shape=out_shape,
        compiler_params=_cparams("parallel", "parallel"),
        name="qk_prep",
    )(*args)


def _attn_kernel(*refs, has_cache, tq):
    q_ref, klo_ref, khi_ref, v1_ref = refs[0:4]
    k = 4
    if has_cache:
        kclo_ref, kchi_ref, vc1_ref = refs[k:k + 3]
        k += 3
    o_ref = refs[k]
    pair = 2 * ATT_HEAD_DIM
    low = lax.broadcasted_iota(jnp.int32, (tq, pair), 1) < ATT_HEAD_DIM
    units = [(kvh, which) for kvh in range(ATT_KV_HEADS) for which in (0, 1)]

    def keys_of(kvh):
        ks = slice(kvh * pair, (kvh + 1) * pair)
        keys = [(klo_ref[:, ks], khi_ref[:, ks], v1_ref[:, ks])]
        if has_cache:
            keys.append((kclo_ref[0, 0, :, ks], kchi_ref[0, 0, :, ks], vc1_ref[0, 0, :, ks]))
        return keys

    def score(u):
        kvh, which = units[u]
        qp = jnp.concatenate([q_ref[:, (2 * kvh) * pair:(2 * kvh + 1) * pair],
                              q_ref[:, (2 * kvh + 1) * pair:(2 * kvh + 2) * pair]], axis=0)
        return [_dot_nt(qp, kk[which]) for kk in keys_of(kvh)]

    def weights(scores):
        m = jnp.max(scores[0], axis=-1, keepdims=True)
        for s in scores[1:]:
            m = jnp.maximum(m, jnp.max(s, axis=-1, keepdims=True))
        return [jnp.exp2(s - m).astype(BF16) for s in scores]

    def values(u, probs):
        pv = None
        for p, kk in zip(probs, keys_of(units[u][0])):
            part = _dot(p, kk[2])
            pv = part if pv is None else pv + part
        return pv

    n = len(units)
    sc = {0: score(0), 1: score(1)}
    pr = {0: weights(sc.pop(0))}
    pv = {}
    for u in range(n):
        if u + 2 < n:
            sc[u + 2] = score(u + 2)
        pv[u] = values(u, pr.pop(u))
        if u + 1 < n:
            pr[u + 1] = weights(sc.pop(u + 1))
        if units[u][1] == 1:
            kvh = units[u][0]
            lo_half, hi_half = pv.pop(u - 1), pv.pop(u)
            first = lo_half / pltpu.roll(lo_half, ATT_HEAD_DIM, 1)
            second = pltpu.roll(hi_half, ATT_HEAD_DIM, 1) / hi_half
            for half in range(2):
                rows = slice(half * tq, (half + 1) * tq)
                o_ref[:, (2 * kvh + half) * pair:(2 * kvh + half + 1) * pair] = (
                    jnp.where(low, first[rows], second[rows]).astype(o_ref.dtype))


def _attn(q, klo, khi, v1, cache, nb, seq, layer):
    t = nb * seq
    tq = min(seq, 256)
    nq = seq // tq
    has_cache = cache is not None
    kv_spec = pl.BlockSpec((seq, ATT_KV_PAD), lambda b, i: (b, 0))
    in_specs = [pl.BlockSpec((tq, D_MODEL), lambda b, i: (b * nq + i, 0)), kv_spec, kv_spec, kv_spec]
    args = [q, klo, khi, v1]
    if has_cache:
        past = cache[0].shape[2]
        spec = pl.BlockSpec((1, 1, past, ATT_KV_PAD), lambda b, i: (b, layer, 0, 0))
        in_specs += [spec, spec, spec]
        args += list(cache)
    return pl.pallas_call(
        functools.partial(_attn_kernel, has_cache=has_cache, tq=tq),
        grid=(nb, nq),
        in_specs=in_specs,
        out_specs=pl.BlockSpec((tq, D_MODEL), lambda b, i: (b * nq + i, 0)),
        out_shape=jax.ShapeDtypeStruct((t, D_MODEL), BF16),
        compiler_params=_cparams("parallel", "parallel"),
        name="gqa",
    )(*args)


def _merge_kernel(yf_ref, yb_ref, z_ref, att_ref, hf_ref, hb_ref, mo_ref,
                  g0_ref, g1_ref, g2_ref, x_ref, gate_ref, sgain_ref, mgain_ref,
                  wb_ref, wo_ref, o_ref, b3_scr):
    ys = (yf_ref[...] + yb_ref[...]) * z_ref[...].astype(F32)
    b1 = _rms(ys) * sgain_ref[...]
    for hh in range(ML_HEADS):
        sl = slice(hh * ML_HEAD_DIM, (hh + 1) * ML_HEAD_DIM)
        b3_scr[:, sl] = _rms(hf_ref[:, sl] + hb_ref[:, sl])
    b3 = (b3_scr[...] * mgain_ref[...]) * mo_ref[...].astype(F32)
    merged = (g0_ref[...].astype(F32) * _dot(b1.astype(BF16), wb_ref[0])
              + g1_ref[...].astype(F32) * _dot(att_ref[...], wb_ref[1])
              + g2_ref[...].astype(F32) * _dot(b3.astype(BF16), wb_ref[2]))
    o_ref[...] = x_ref[...] + gate_ref[0] * _dot(merged.astype(BF16), wo_ref[...])


def _merge(yf, yb, u, att, hf, hb, x, gate1, sgain, mgain, wb, wo, seq, layer):
    t = x.shape[0]
    tm = min(t, 256)
    tok = lambda i: (i, 0)
    ucol = lambda off: pl.BlockSpec((tm, D_MODEL), lambda i: (i, off // D_MODEL))
    full = pl.BlockSpec((tm, D_MODEL), tok)
    vec = pl.BlockSpec((1, D_MODEL), lambda i: (0, 0))
    once = pl.Buffered(1)
    in_specs = [full, full, ucol(U_SZ), full, full, full, ucol(U_MO),
                ucol(U_G), ucol(U_G + D_MODEL), ucol(U_G + 2 * D_MODEL), full,
                _mod_spec(gate1.shape[0], seq, tm), vec, vec,
                pl.BlockSpec((None, 3, D_MODEL, D_MODEL), lambda i: (layer, 0, 0, 0), pipeline_mode=once),
                pl.BlockSpec((None, D_MODEL, D_MODEL), lambda i: (layer, 0, 0), pipeline_mode=once)]
    return pl.pallas_call(
        _merge_kernel,
        grid=(t // tm,),
        in_specs=in_specs,
        out_specs=full,
        out_shape=jax.ShapeDtypeStruct((t, D_MODEL), F32),
        scratch_shapes=[pltpu.VMEM((tm, D_MODEL), F32)],
        compiler_params=_cparams("parallel"),
        name="branch_merge",
    )(yf, yb, u, att, hf, hb, u, u, u, u, x, gate1, sgain, mgain, wb, wo)


def _ffn_kernel(x_ref, sc_ref, sh_ref, gate_ref, wi_ref, wo_ref, o_ref):
    x = x_ref[...]
    h = (_rms(x) * (1.0 + sc_ref[0]) + sh_ref[0]).astype(BF16)
    a = _dot(h, wi_ref[:, :FFN_HIDDEN])
    b = _dot(h, wi_ref[:, FFN_HIDDEN:])
    act = (a * _sigmoid(a) * b).astype(BF16)
    o_ref[...] = x + gate_ref[0] * _dot(act, wo_ref[...])


def _ffn(x, scale, shift, gate, wi, wo, seq, layer):
    t = x.shape[0]
    tm = min(t, 256)
    nb = scale.shape[0]
    once = pl.Buffered(1)
    full = pl.BlockSpec((tm, D_MODEL), lambda i: (i, 0))
    return pl.pallas_call(
        _ffn_kernel,
        grid=(t // tm,),
        in_specs=[full, _mod_spec(nb, seq, tm), _mod_spec(nb, seq, tm), _mod_spec(nb, seq, tm),
                  pl.BlockSpec((None, D_MODEL, 2 * FFN_HIDDEN), lambda i: (layer, 0, 0), pipeline_mode=once),
                  pl.BlockSpec((None, FFN_HIDDEN, D_MODEL), lambda i: (layer, 0, 0), pipeline_mode=once)],
        out_specs=full,
        out_shape=jax.ShapeDtypeStruct((t, D_MODEL), F32),
        compiler_params=_cparams("parallel"),
        name="ffn",
    )(x, scale, shift, gate, wi, wo)


def _gate_consts():
    idx = np.arange(GATE_W)
    back = ((idx >= G_DT + SSD_HEADS) & (idx < G_LI)) | ((idx >= G_LF + ML_HEADS) & (idx < G_LF + 2 * ML_HEADS))
    r = np.arange(CHUNK)
    tlo = (r[:, None] >= r[None, :]).astype(np.float32)
    tup = (r[:, None] <= r[None, :]).astype(np.float32)
    return (jnp.asarray(back.astype(np.float32)), jnp.asarray(tlo, BF16), jnp.asarray(tup, BF16))


def _scan_consts():
    def expand(first, heads, width, parts):
        e = np.zeros((2, parts * GATE_W, heads * width), np.float32)
        for d in range(2):
            for h in range(heads):
                for part in range(parts):
                    e[d, part * GATE_W + first + d * heads + h, h * width:(h + 1) * width] = 1.0
        return jnp.asarray(e, BF16)

    ssd = (expand(G_DT, SSD_HEADS, SSD_HEAD_DIM, 1), expand(G_DT, SSD_HEADS, CHUNK, 2),
           jnp.asarray(np.eye(SSD_BC, dtype=np.float32), BF16))
    ml = (expand(G_LF, ML_HEADS, ML_HEAD_DIM, 2), jnp.asarray(np.eye(CHUNK, dtype=np.float32), BF16))
    return ssd, ml


def _ssd_state_to_compact(s):
    lead = s.shape[:-3]
    rep = SSD_HEADS // SSD_GROUPS
    s = s.reshape(lead + (SSD_GROUPS, rep, SSD_HEAD_DIM, SSD_STATE))
    nd = len(lead)
    s = jnp.transpose(s, tuple(range(nd)) + (nd, nd + 3, nd + 1, nd + 2))
    return s.reshape(lead + (SSD_GROUPS, SSD_STATE, SSD_GROUP_W))


def _ssd_state_from_compact(s):
    lead = s.shape[:-3]
    rep = SSD_HEADS // SSD_GROUPS
    s = s.reshape(lead + (SSD_GROUPS, SSD_STATE, rep, SSD_HEAD_DIM))
    nd = len(lead)
    s = jnp.transpose(s, tuple(range(nd)) + (nd, nd + 2, nd + 3, nd + 1))
    return s.reshape(lead + (SSD_HEADS, SSD_HEAD_DIM, SSD_STATE))


def _rope_tables(seq):
    pos = np.arange(seq)
    quarter = ATT_HEAD_DIM // 4
    freqs = jnp.asarray(ROPE_THETA, F32) ** (-jnp.arange(quarter, dtype=F32) / quarter)
    ang_r = jnp.asarray(pos // GRID_W, F32)[:, None] * freqs
    ang_c = jnp.asarray(pos % GRID_W, F32)[:, None] * freqs
    cos = jnp.concatenate([jnp.cos(ang_r)] * 2 + [jnp.cos(ang_c)] * 2, axis=-1)
    sin = jnp.concatenate([-jnp.sin(ang_r), jnp.sin(ang_r), -jnp.sin(ang_c), jnp.sin(ang_c)], axis=-1)
    reps = (D_MODEL + ATT_KV) // ATT_HEAD_DIM
    return jnp.tile(cos, (1, reps)), jnp.tile(sin, (1, reps))


def _split_w_in(w_in):
    sizes = (D_MODEL, D_MODEL, SSD_BC, SSD_BC, 2 * SSD_HEADS,
             D_MODEL, ATT_KV, ATT_KV,
             D_MODEL, D_MODEL, D_MODEL, D_MODEL, 4 * ML_HEADS, 3 * D_MODEL)
    offs = np.cumsum((0,) + sizes)
    return [w_in[:, :, offs[i]:offs[i + 1]] for i in range(len(sizes))]


def kernel(x_prompt, x_sample, cache_k, cache_v, state_ssd, state_ml_c, state_ml_n, state_ml_m,
           c, c_ctx, w_ada, b_ada, w_in, ssd_conv_w, ssd_conv_b, ssd_a_log, ssd_dt_bias, ssd_d,
           ssd_norm, att_q_norm, att_k_norm, ml_conv_w, ml_conv_b, ml_gate_bias, ml_norm,
           w_branch, w_out, w_ffn_in, w_ffn_out, final_norm):
    depth = w_in.shape[0]
    nbp, seqp, _ = x_prompt.shape
    nbs, seqs, _ = x_sample.shape
    past = cache_k.shape[2]

    (s_x, s_z, s_b, s_c, s_dt, a_q, a_k, a_v, m_q, m_k, m_v, m_o, m_g, g) = _split_w_in(w_in)
    w_main = jnp.concatenate([s_x, s_b, s_c, a_q, a_k, a_v, m_q, m_k, m_v, s_z, m_o, g], axis=-1).astype(BF16)
    m_g4 = m_g.reshape(depth, D_MODEL, 2, 2, ML_HEADS)
    w_small = jnp.concatenate(
        [s_dt, m_g4[:, :, :, 0].reshape(depth, D_MODEL, 2 * ML_HEADS),
         m_g4[:, :, :, 1].reshape(depth, D_MODEL, 2 * ML_HEADS),
         jnp.zeros((depth, D_MODEL, GATE_W - G_LF - 2 * ML_HEADS), F32)], axis=-1).astype(BF16)
    w_small_t = jnp.swapaxes(w_small, 1, 2)
    pad = jnp.zeros((depth, GATE_W - G_LF - 2 * ML_HEADS), F32)
    gate_bias = jnp.concatenate(
        [ssd_dt_bias.reshape(depth, 2 * SSD_HEADS), ml_gate_bias[:, :, 0].reshape(depth, 2 * ML_HEADS),
         ml_gate_bias[:, :, 1].reshape(depth, 2 * ML_HEADS), pad], axis=-1)
    a_vec = jnp.concatenate(
        [-jnp.exp(ssd_a_log.reshape(depth, 2 * SSD_HEADS)), jnp.zeros((depth, GATE_W - 2 * SSD_HEADS), F32)], axis=-1)
    d_vec = jnp.repeat(ssd_d, SSD_HEAD_DIM, axis=-1).reshape(depth, 1, D_MODEL)
    qk_gain = jnp.concatenate([jnp.tile(att_q_norm, (1, ATT_HEADS)), jnp.tile(att_k_norm, (1, ATT_KV_HEADS))], axis=-1)
    qk_gain = qk_gain.reshape(depth, 1, D_MODEL + ATT_KV)
    w_branch_b = w_branch.astype(BF16)
    w_out_b = w_out.astype(BF16)
    w_ffn_in_b = w_ffn_in.astype(BF16)
    w_ffn_out_b = w_ffn_out.astype(BF16)
    consts = _gate_consts()
    tables = _rope_tables(seqs)
    zeros = jnp.zeros(cache_k.shape, BF16)
    ck, cv = cache_k.astype(BF16), cache_v.astype(BF16)
    pad_shape = (nbs, depth, past, ATT_KV_PAD)
    cache = (jnp.concatenate([ck, zeros], axis=-1).reshape(pad_shape),
             jnp.concatenate([zeros, ck], axis=-1).reshape(pad_shape),
             jnp.concatenate([cv, jnp.ones(cache_v.shape, BF16)], axis=-1).reshape(pad_shape))
    ssd_consts, ml_consts = _scan_consts()
    ssd_h0 = _ssd_state_to_compact(state_ssd)
    ml_cn0 = jnp.concatenate(
        [state_ml_c, jnp.broadcast_to(state_ml_n[..., None], state_ml_c.shape)], axis=-1)
    ml_m0 = jnp.repeat(state_ml_m, ML_HEAD_DIM, axis=-1).reshape(nbs, depth, 2, 1, D_MODEL)

    rows = 8 * ((1 + nbs + 7) // 8)
    cc = jnp.concatenate([c_ctx[None], c, jnp.zeros((rows - 1 - nbs, D_MODEL), F32)], axis=0)
    mod = _ada(cc, w_ada, b_ada)

    def mods(layer, lo, hi):
        m = mod[layer, lo:hi].reshape(hi - lo, 1, 6, D_MODEL)
        return [m[:, :, i] for i in range(6)]

    def layer_step(x, layer, nb, seq, mod6, ctx):
        shift1, scale1, gate1, shift2, scale2, gate2 = mod6
        h = _norm_mod(x, scale1, shift1, seq)
        u = _proj(h, w_main, layer)
        gates = _gate(h, w_small[layer], w_small_t[layer], gate_bias[layer], a_vec[layer], consts)
        sx = _conv(u, U_CONV, D_MODEL, ssd_conv_w[layer, :, :D_MODEL], ssd_conv_b[layer, :D_MODEL], nb, seq, F32)
        sbc = _conv(u, U_CONV + D_MODEL, 2 * SSD_BC, ssd_conv_w[layer, :, D_MODEL:], ssd_conv_b[layer, D_MODEL:],
                    nb, seq, BF16)
        mqk = _conv(u, U_MQ, 2 * D_MODEL, ml_conv_w[layer], ml_conv_b[layer], nb, seq, BF16)
        if ctx:
            yf, yb = _ssd(sx, sbc, gates, ssd_consts, d_vec[layer], nb, seq, ssd_h0, layer)
            q, klo, khi, v1 = _qkprep(u, qk_gain[layer], tables, nb, seq, False)
            att = _attn(q, klo, khi, v1, cache, nb, seq, layer)
            hf, hb = _mlstm(mqk, u, gates, ml_consts, nb, seq, (ml_cn0, ml_m0), layer)
            new = None
        else:
            yf, yb, hst = _ssd(sx, sbc, gates, ssd_consts, d_vec[layer], nb, seq, None, layer)
            q, klo, khi, v1, kf, vf = _qkprep(u, qk_gain[layer], None, nb, seq, True)
            att = _attn(q, klo, khi, v1, None, nb, seq, layer)
            hf, hb, cn, mrow = _mlstm(mqk, u, gates, ml_consts, nb, seq, None, layer)
            new = (kf.reshape(nb, seq, ATT_KV_HEADS, ATT_HEAD_DIM), vf.reshape(nb, seq, ATT_KV_HEADS, ATT_HEAD_DIM),
                   _ssd_state_from_compact(hst), cn[..., :ML_HEAD_DIM], cn[..., ML_HEAD_DIM],
                   mrow[:, :, 0, ::ML_HEAD_DIM])
        x = _merge(yf, yb, u, att, hf, hb, x, gate1,
                   ssd_norm[layer].reshape(1, D_MODEL), ml_norm[layer].reshape(1, D_MODEL),
                   w_branch_b, w_out_b, seq, layer)
        x = _ffn(x, scale2, shift2, gate2, w_ffn_in_b, w_ffn_out_b, seq, layer)
        return x, new

    y_p = x_prompt.reshape(nbp * seqp, D_MODEL)
    y_s = x_sample.reshape(nbs * seqs, D_MODEL)
    news = []
    for layer in range(depth):
        y_p, new = layer_step(y_p, layer, nbp, seqp, mods(layer, 0, 1), False)
        news.append(new)
        y_s, _ = layer_step(y_s, layer, nbs, seqs, mods(layer, 1, 1 + nbs), True)

    y_prompt = _norm_gain(y_p, final_norm).reshape(x_prompt.shape)
    y_sample = _norm_gain(y_s, final_norm).reshape(x_sample.shape)
    stacked = [jnp.stack([n[i] for n in news], axis=1) for i in range(6)]
    return (y_prompt, y_sample) + tuple(stacked)
```

```python
import functools

import numpy as np
import jax
import jax.numpy as jnp
from jax import lax
from jax.experimental import pallas as pl
from jax.experimental.pallas import tpu as pltpu

F32 = jnp.float32
BF16 = jnp.bfloat16

D_MODEL = 1024
GRID_W = 64
EPS = 1e-6
CONV_W = 4
CHUNK = 128
SSD_HEADS = 16
SSD_HEAD_DIM = 64
SSD_GROUPS = 4
SSD_STATE = 64
SSD_BC = SSD_GROUPS * SSD_STATE
SSD_CONV = D_MODEL + 2 * SSD_BC
ATT_HEADS = 16
ATT_HEAD_DIM = 64
ATT_KV_HEADS = 4
ATT_KV = ATT_KV_HEADS * ATT_HEAD_DIM
ATT_QKV = D_MODEL + 2 * ATT_KV
ATT_KV_PAD = ATT_KV_HEADS * 2 * ATT_HEAD_DIM
Q_SCALE = ATT_HEAD_DIM ** -0.5 * 1.4426950408889634
ROPE_THETA = 10000.0
ML_HEADS = 8
ML_HEAD_DIM = 128
FFN_HIDDEN = 2816
GATE_W = 128

U_CONV = 0
U_ATT = 1536
U_MQ = 3072
U_MK = 4096
U_MV = 5120
U_SZ = 6144
U_MO = 7168
U_G = 8192
U_W = 11264

G_DT = 0
G_LI = 32
G_LF = 48

VMEM_LIMIT = 48 * 1024 * 1024
NT_DIMS = (((1,), (1,)), ((), ()))
TN_DIMS = (((0,), (0,)), ((), ()))


def _cparams(*sem):
    return pltpu.CompilerParams(dimension_semantics=sem, vmem_limit_bytes=VMEM_LIMIT)


def _sigmoid(x):
    return 1.0 / (1.0 + jnp.exp(-x))


def _softplus(x):
    return jnp.maximum(x, 0.0) + jnp.log1p(jnp.exp(-jnp.abs(x)))


def _dot(a, b):
    return jnp.dot(a, b, preferred_element_type=F32)


def _dot_nt(a, b):
    return lax.dot_general(a, b, NT_DIMS, preferred_element_type=F32)


def _dot_tn(a, b):
    return lax.dot_general(a, b, TN_DIMS, preferred_element_type=F32)


def _ada_kernel(c_ref, w_ref, b_ref, o_ref):
    c = c_ref[...]
    s = (c * _sigmoid(c)).astype(BF16)
    o_ref[0] = _dot(s, w_ref[0].astype(BF16)) + b_ref[0]


def _ada(cc, w_ada, b_ada):
    depth, _, n6 = w_ada.shape
    rows = cc.shape[0]
    tn = 512
    return pl.pallas_call(
        _ada_kernel,
        grid=(depth, n6 // tn),
        in_specs=[pl.BlockSpec((rows, D_MODEL), lambda l, j: (0, 0)),
                  pl.BlockSpec((1, D_MODEL, tn), lambda l, j: (l, 0, j)),
                  pl.BlockSpec((1, 1, tn), lambda l, j: (l, 0, j))],
        out_specs=pl.BlockSpec((1, rows, tn), lambda l, j: (l, 0, j)),
        out_shape=jax.ShapeDtypeStruct((depth, rows, n6), F32),
        compiler_params=_cparams("parallel", "parallel"),
        name="ada_mod",
    )(cc, w_ada, b_ada.reshape(depth, 1, n6))


def _rms(x):
    return x * lax.rsqrt(jnp.mean(x * x, axis=-1, keepdims=True) + EPS)


def _norm_mod_kernel(x_ref, sc_ref, sh_ref, o_ref):
    o_ref[...] = (_rms(x_ref[...]) * (1.0 + sc_ref[0]) + sh_ref[0]).astype(o_ref.dtype)


def _mod_spec(nb, seq, tm):
    if nb == 1:
        return pl.BlockSpec((1, 1, D_MODEL), lambda i: (0, 0, 0))
    return pl.BlockSpec((1, 1, D_MODEL), lambda i: ((i * tm) // seq, 0, 0))


def _norm_mod(x, scale, shift, seq):
    t = x.shape[0]
    tm = min(t, 512)
    nb = scale.shape[0]
    return pl.pallas_call(
        _norm_mod_kernel,
        grid=(t // tm,),
        in_specs=[pl.BlockSpec((tm, D_MODEL), lambda i: (i, 0)),
                  _mod_spec(nb, seq, tm), _mod_spec(nb, seq, tm)],
        out_specs=pl.BlockSpec((tm, D_MODEL), lambda i: (i, 0)),
        out_shape=jax.ShapeDtypeStruct((t, D_MODEL), BF16),
        compiler_params=_cparams("parallel"),
        name="norm_mod",
    )(x, scale, shift)


def _sigmoid_tanh(x):
    return 0.5 * jnp.tanh(0.5 * x) + 0.5


def _silu_tanh(x):
    h = 0.5 * x
    return h * jnp.tanh(h) + h


def _proj_kernel(a_ref, w_ref, o_ref, *, silu_lo, sig_lo, sub):
    j = pl.program_id(1)
    w = w_ref[...]

    def run(act):
        for r in range(a_ref.shape[0] // sub):
            rows = slice(r * sub, (r + 1) * sub)
            o_ref[rows, :] = act(_dot(a_ref[rows, :], w)).astype(o_ref.dtype)

    @pl.when(j < silu_lo)
    def _():
        run(lambda acc: acc)

    @pl.when(jnp.logical_and(j >= silu_lo, j < sig_lo))
    def _():
        run(_silu_tanh)

    @pl.when(j >= sig_lo)
    def _():
        run(_sigmoid_tanh)


def _proj(h, w_main, layer):
    t = h.shape[0]
    tm = min(t, 2048)
    tn = 1024
    kern = functools.partial(_proj_kernel, silu_lo=U_SZ // tn, sig_lo=U_MO // tn, sub=min(tm, 512))
    return pl.pallas_call(
        kern,
        grid=(t // tm, U_W // tn),
        in_specs=[pl.BlockSpec((tm, D_MODEL), lambda i, j: (i, 0)),
                  pl.BlockSpec((None, D_MODEL, tn), lambda i, j: (layer, 0, j))],
        out_specs=pl.BlockSpec((tm, tn), lambda i, j: (i, j)),
        out_shape=jax.ShapeDtypeStruct((t, U_W), BF16),
        compiler_params=_cparams("parallel", "parallel"),
        name="in_proj",
    )(h, w_main)


def _split3(x):
    hi = x.astype(BF16)
    r1 = x - hi.astype(F32)
    mid = r1.astype(BF16)
    lo = (r1 - mid.astype(F32)).astype(BF16)
    return hi, mid, lo


def _tri_left(tri, x):
    hi, mid, lo = _split3(x)
    return _dot(tri, hi) + _dot(tri, mid) + _dot(tri, lo)


def _tri_right(x, tri):
    hi, mid, lo = _split3(x)
    return _dot(hi, tri) + _dot(mid, tri) + _dot(lo, tri)


def _gate_vals(u, idx, a):
    sp = _softplus(u)
    ls = -_softplus(-u)
    val = jnp.where(idx < G_LI, sp, jnp.where(idx < G_LF, u, jnp.where(idx < G_LF + 16, ls, 0.0)))
    cin = jnp.where(idx < G_LI, sp * a, jnp.where(jnp.logical_and(idx >= G_LF, idx < G_LF + 16), ls, 0.0))
    return val, cin


def _gate_kernel(h_ref, ws_ref, wst_ref, b_ref, bt_ref, a_ref, at_ref, dm_ref, dmt_ref,
                 tlo_ref, tup_ref, val_ref, cum_ref, valt_ref, cumt_ref):
    h = h_ref[...]
    tlo = tlo_ref[...]
    tup = tup_ref[...]
    rows = h.shape[0]
    u = _dot(h, ws_ref[...]) + b_ref[...]
    val, cin = _gate_vals(u, lax.broadcasted_iota(jnp.int32, (rows, GATE_W), 1), a_ref[...])
    val_ref[...] = val
    ut = _dot_nt(wst_ref[...], h) + bt_ref[...]
    valt, cint = _gate_vals(ut, lax.broadcasted_iota(jnp.int32, (GATE_W, rows), 0), at_ref[...])
    valt_ref[...] = valt
    back = dm_ref[...] > 0.5
    back_t = dmt_ref[...] > 0.5
    for c in range(rows // CHUNK):
        sl = slice(c * CHUNK, (c + 1) * CHUNK)
        cum_ref[sl, :] = jnp.where(back, _tri_left(tup, cin[sl]), _tri_left(tlo, cin[sl]))
        cumt_ref[:, sl] = jnp.where(back_t, _tri_right(cint[:, sl], tlo), _tri_right(cint[:, sl], tup))


def _gate(h, ws, wst, bias, avec, consts):
    t = h.shape[0]
    tm = min(t, 4 * CHUNK)
    dmask, tlo, tup = consts
    row = lambda i: (i, 0)
    col = lambda i: (0, i)
    fix = lambda i: (0, 0)
    tm_out = jax.ShapeDtypeStruct((t, GATE_W), F32)
    fm_out = jax.ShapeDtypeStruct((GATE_W, t), F32)
    return pl.pallas_call(
        _gate_kernel,
        grid=(t // tm,),
        in_specs=[pl.BlockSpec((tm, D_MODEL), row),
                  pl.BlockSpec((D_MODEL, GATE_W), fix),
                  pl.BlockSpec((GATE_W, D_MODEL), fix),
                  pl.BlockSpec((1, GATE_W), fix), pl.BlockSpec((GATE_W, 1), fix),
                  pl.BlockSpec((1, GATE_W), fix), pl.BlockSpec((GATE_W, 1), fix),
                  pl.BlockSpec((1, GATE_W), fix), pl.BlockSpec((GATE_W, 1), fix),
                  pl.BlockSpec((CHUNK, CHUNK), fix), pl.BlockSpec((CHUNK, CHUNK), fix)],
        out_specs=[pl.BlockSpec((tm, GATE_W), row), pl.BlockSpec((tm, GATE_W), row),
                   pl.BlockSpec((GATE_W, tm), col), pl.BlockSpec((GATE_W, tm), col)],
        out_shape=[tm_out, tm_out, fm_out, fm_out],
        compiler_params=_cparams("parallel"),
        name="gate_prep",
    )(h, ws, wst, bias.reshape(1, GATE_W), bias.reshape(GATE_W, 1),
      avec.reshape(1, GATE_W), avec.reshape(GATE_W, 1),
      dmask.reshape(1, GATE_W), dmask.reshape(GATE_W, 1), tlo, tup)


def _conv_kernel(x_ref, w_ref, b_ref, o_ref):
    x = x_ref[...].astype(F32)
    seq = x.shape[0]
    row = lax.broadcasted_iota(jnp.int32, x.shape, 0)
    acc = x * w_ref[1:2, :] + b_ref[...]
    acc = acc + jnp.where(row >= 1, pltpu.roll(x, 1, 0), 0.0) * w_ref[0:1, :]
    acc = acc + jnp.where(row < seq - 1, pltpu.roll(x, seq - 1, 0), 0.0) * w_ref[2:3, :]
    acc = acc + jnp.where(row < seq - 2, pltpu.roll(x, seq - 2, 0), 0.0) * w_ref[3:4, :]
    o_ref[...] = _silu_tanh(acc).astype(o_ref.dtype)


def _conv(u, col0, width, w, b, nb, seq, out_dtype):
    tc = 256
    c0 = col0 // tc
    return pl.pallas_call(
        _conv_kernel,
        grid=(nb, width // tc),
        in_specs=[pl.BlockSpec((seq, tc), lambda i, j: (i, c0 + j)),
                  pl.BlockSpec((CONV_W, tc), lambda i, j: (0, j)),
                  pl.BlockSpec((1, tc), lambda i, j: (0, j))],
        out_specs=pl.BlockSpec((seq, tc), lambda i, j: (i, j)),
        out_shape=jax.ShapeDtypeStruct((nb * seq, width), out_dtype),
        compiler_params=_cparams("parallel", "parallel"),
        name="dwconv_silu",
    )(u, w, b.reshape(1, width))


def _tri_mask(fwd):
    row = lax.broadcasted_iota(jnp.int32, (CHUNK, CHUNK), 0)
    col = lax.broadcasted_iota(jnp.int32, (CHUNK, CHUNK), 1)
    return row >= col if fwd else row <= col


def _lane_expand(x, e):
    hi, mid, _ = _split3(x)
    return _dot(jnp.concatenate([hi, mid], axis=1), e)


SSD_GROUP_W = (SSD_HEADS // SSD_GROUPS) * SSD_HEAD_DIM


def _ssd_dir(d, x_ref, bc_ref, val_ref, cum_ref, valt_ref, cumt_ref, e64_ref, e128_ref, eye_ref, skip,
             y_ref, st_scr):
    mask = _tri_mask(d == 0)
    end = CHUNK - 1 if d == 0 else 0
    x = x_ref[...]
    bb = bc_ref[:, :SSD_BC]
    cm = bc_ref[:, SSD_BC:]
    lane_c = lax.broadcasted_iota(jnp.int32, cm.shape, 1)
    lane_x = lax.broadcasted_iota(jnp.int32, (CHUNK, 2 * SSD_HEAD_DIM), 1)
    a128 = _lane_expand(cum_ref[...], e128_ref[d])
    a64 = jnp.concatenate(
        [jnp.where(lane_x < SSD_HEAD_DIM, a128[:, (2 * p) * CHUNK:(2 * p + 1) * CHUNK],
                   a128[:, (2 * p + 1) * CHUNK:(2 * p + 2) * CHUNK]) for p in range(SSD_HEADS // 2)], axis=1)
    dt64 = _dot(val_ref[...].astype(BF16), e64_ref[d])
    dt_r = valt_ref[...]
    ac_r = cumt_ref[...]
    st = st_scr[d]
    inter = _dot(cm, st.astype(BF16))
    ea = jnp.exp(a64)
    cm32 = cm.astype(F32)
    c_stack = jnp.concatenate(
        [jnp.where(jnp.logical_and(lane_c >= g * SSD_STATE, lane_c < (g + 1) * SSD_STATE), cm32, 0.0).astype(BF16)
         for g in range(SSD_GROUPS)], axis=0)
    cbt_all = _dot_nt(c_stack, bb)
    bt = _dot_nt(eye_ref[...], bb).astype(BF16)
    yield
    tot = a64[end:end + 1, :]
    xw = (x * (jnp.exp(tot - a64) * dt64)).astype(BF16)
    decay = jnp.exp(tot)
    for g in range(SSD_GROUPS):
        rs = slice(g * SSD_STATE, (g + 1) * SSD_STATE)
        cs = slice(g * SSD_GROUP_W, (g + 1) * SSD_GROUP_W)
        st_scr[d, rs, cs] = decay[:, cs] * st[rs, cs] + _dot(bt[rs, :], xw[:, cs])
    yield
    n_pairs = SSD_HEADS // 2
    weights, xbds = [], []
    for p in range(n_pairs):
        cbt = cbt_all[(p // 2) * CHUNK:(p // 2 + 1) * CHUNK]
        parts = []
        for hh in (2 * p, 2 * p + 1):
            lane = G_DT + d * SSD_HEADS + hh
            seg = jnp.where(mask, a128[:, hh * CHUNK:(hh + 1) * CHUNK] - ac_r[lane:lane + 1, :], -jnp.inf)
            parts.append((cbt * jnp.exp(seg) * dt_r[lane:lane + 1, :]).astype(BF16))
        weights.append(jnp.concatenate(parts, axis=1))
        xp = x[:, p * 2 * SSD_HEAD_DIM:(p + 1) * 2 * SSD_HEAD_DIM]
        xbds.append(jnp.concatenate([jnp.where(lane_x < SSD_HEAD_DIM, xp, 0.0),
                                     jnp.where(lane_x >= SSD_HEAD_DIM, xp, 0.0)], axis=0).astype(BF16))
    yield
    for p in range(n_pairs):
        cs = slice(p * 2 * SSD_HEAD_DIM, (p + 1) * 2 * SSD_HEAD_DIM)
        y = _dot(weights[p], xbds[p]) + ea[:, cs] * inter[:, cs]
        if skip is not None:
            y = y + skip[:, cs] * x[:, cs]
        y_ref[:, cs] = y.astype(y_ref.dtype)


def _ssd_kernel(*refs, has_h0, emit_state):
    xf_ref, bcf_ref, xb_ref, bcb_ref = refs[0:4]
    gf = refs[4:8]
    gb = refs[8:12]
    consts = refs[12:15]
    dvec_ref = refs[15]
    k = 16
    if has_h0:
        h0_ref = refs[k]
        k += 1
    yf_ref, yb_ref = refs[k:k + 2]
    k += 2
    if emit_state:
        hout_ref = refs[k]
        k += 1
    st_scr = refs[k]
    c = pl.program_id(1)

    @pl.when(c == 0)
    def _():
        st_scr[...] = jnp.zeros(st_scr.shape, F32)
        if has_h0:
            for d in range(2):
                for g in range(SSD_GROUPS):
                    st_scr[d, g * SSD_STATE:(g + 1) * SSD_STATE,
                           g * SSD_GROUP_W:(g + 1) * SSD_GROUP_W] = h0_ref[0, 0, d, g]

    _interleave(_ssd_dir(0, xf_ref, bcf_ref, *gf, *consts, dvec_ref[...], yf_ref, st_scr),
                _ssd_dir(1, xb_ref, bcb_ref, *gb, *consts, None, yb_ref, st_scr))

    if emit_state:
        @pl.when(c == pl.num_programs(1) - 1)
        def _():
            for d in range(2):
                for g in range(SSD_GROUPS):
                    hout_ref[0, d, g] = st_scr[d, g * SSD_STATE:(g + 1) * SSD_STATE,
                                               g * SSD_GROUP_W:(g + 1) * SSD_GROUP_W]


def _chunk_specs(nc, width):
    fwd = pl.BlockSpec((CHUNK, width), lambda b, c: (b * nc + c, 0))
    bwd = pl.BlockSpec((CHUNK, width), lambda b, c: (b * nc + nc - 1 - c, 0))
    return fwd, bwd


def _gate_specs(nc):
    tf, tb = _chunk_specs(nc, GATE_W)
    ff = pl.BlockSpec((GATE_W, CHUNK), lambda b, c: (0, b * nc + c))
    fb = pl.BlockSpec((GATE_W, CHUNK), lambda b, c: (0, b * nc + nc - 1 - c))
    return [tf, tf, ff, ff], [tb, tb, fb, fb]


def _const_spec(arr):
    nd = arr.ndim
    return pl.BlockSpec(arr.shape, lambda b, c: (0,) * nd)


def _ssd(x, bc, gates, consts, dvec, nb, seq, state, layer):
    nc = seq // CHUNK
    t = nb * seq
    has_h0 = state is not None
    xf, xb = _chunk_specs(nc, D_MODEL)
    bcf, bcb = _chunk_specs(nc, 2 * SSD_BC)
    gfs, gbs = _gate_specs(nc)
    yf, yb = _chunk_specs(nc, D_MODEL)
    st_shape = (2, SSD_GROUPS, SSD_STATE, SSD_GROUP_W)
    in_specs = [xf, bcf, xb, bcb] + gfs + gbs + [_const_spec(a) for a in consts] + [_const_spec(dvec)]
    args = [x, bc, x, bc] + list(gates) + list(gates) + list(consts) + [dvec]
    if has_h0:
        in_specs.append(pl.BlockSpec((1, 1) + st_shape, lambda b, c: (b, layer, 0, 0, 0, 0)))
        args.append(state)
    out_specs = [yf, yb]
    out_shape = [jax.ShapeDtypeStruct((t, D_MODEL), BF16)] * 2
    if not has_h0:
        out_specs.append(pl.BlockSpec((1,) + st_shape, lambda b, c: (b, 0, 0, 0, 0)))
        out_shape.append(jax.ShapeDtypeStruct((nb,) + st_shape, F32))
    return pl.pallas_call(
        functools.partial(_ssd_kernel, has_h0=has_h0, emit_state=not has_h0),
        grid=(nb, nc),
        in_specs=in_specs,
        out_specs=out_specs,
        out_shape=out_shape,
        scratch_shapes=[pltpu.VMEM((2, SSD_BC, D_MODEL), F32)],
        compiler_params=_cparams("parallel", "arbitrary"),
        name="ssd_scan",
    )(*args)


def _ml_dir(d, q_ref, k_ref, v_ref, val_ref, cum_ref, valt_ref, cumt_ref, eb_ref, eye_ref,
            y_ref, cn_scr, m_scr):
    mask = _tri_mask(d == 0)
    end = CHUNK - 1 if d == 0 else 0
    heads = range(ML_HEADS)
    sls = [slice(hh * ML_HEAD_DIM, (hh + 1) * ML_HEAD_DIM) for hh in heads]
    kscale = ML_HEAD_DIM ** -0.5
    b128 = _lane_expand(cum_ref[...], eb_ref[d])
    li_r = valt_ref[...]
    bc_r = cumt_ref[...]
    m_row = m_scr[d]
    q = q_ref[...]
    k = k_ref[...]
    v = v_ref[...]
    ones = jnp.ones((CHUNK, ML_HEAD_DIM), BF16)
    eye = eye_ref[...]
    v1 = [jnp.concatenate([v[:, sl], ones], axis=1) for sl in sls]
    cn = [cn_scr[d, hh] for hh in heads]
    kt = [_dot_nt(eye, k[:, sl]) for sl in sls]
    qk = [_dot_nt(q[:, sl], (k[:, sl].astype(F32) * kscale).astype(BF16)) for sl in sls]
    qc = [_dot(q[:, sl], cn[hh].astype(BF16)) for hh, sl in zip(heads, sls)]
    yield
    b_end = b128[end:end + 1, :]
    bm = b128 + m_row
    kwt, dch, m_parts, s, inter, m_t = [], [], [], [], [], []
    for hh, sl in zip(heads, sls):
        b_j = bc_r[G_LF + d * ML_HEADS + hh:G_LF + d * ML_HEADS + hh + 1, :]
        i_j = li_r[G_LI + d * ML_HEADS + hh:G_LI + d * ML_HEADS + hh + 1, :]
        wj = b_end[:, sl] - b_j + i_j
        m_new = jnp.maximum(b_end[:, sl] + m_row[:, sl], jnp.max(wj, axis=1, keepdims=True))
        kwt.append((kt[hh] * (jnp.exp(wj - m_new) * kscale)).astype(BF16))
        dch.append(jnp.exp(b_end[:, sl] + m_row[:, sl] - m_new))
        m_parts.append(m_new)
        dmat = jnp.where(mask, b128[:, sl] - b_j + i_j, -jnp.inf)
        mt = jnp.maximum(bm[:, sl], jnp.max(dmat, axis=1, keepdims=True))
        s.append((qk[hh] * jnp.exp(dmat - mt)).astype(BF16))
        inter.append(jnp.exp(bm[:, sl] - mt))
        m_t.append(mt)
    yield
    upd = [_dot(kwt[hh], v1[hh]) for hh in heads]
    sv = [_dot(s[hh], v1[hh]) for hh in heads]
    yield
    for hh, sl in zip(heads, sls):
        r = sv[hh] + jnp.concatenate([inter[hh], inter[hh]], axis=1) * qc[hh]
        y_ref[:, sl] = (r[:, :ML_HEAD_DIM]
                        / jnp.maximum(jnp.abs(r[:, ML_HEAD_DIM:]), jnp.exp(-m_t[hh]))).astype(y_ref.dtype)
        cn_scr[d, hh] = jnp.concatenate([dch[hh], dch[hh]], axis=1) * cn[hh] + upd[hh]
    m_scr[d] = jnp.concatenate(m_parts, axis=1)


def _interleave(*gens):
    live = list(gens)
    while live:
        still = []
        for g in live:
            try:
                next(g)
                still.append(g)
            except StopIteration:
                pass
        live = still


def _ml_kernel(*refs, has_init, emit_state):
    qf_ref, kf_ref, vf_ref, qb_ref, kb_ref, vb_ref = refs[0:6]
    gf = refs[6:10]
    gb = refs[10:14]
    consts = refs[14:16]
    k = 16
    if has_init:
        cn0_ref, m0_ref = refs[k:k + 2]
        k += 2
    yf_ref, yb_ref = refs[k:k + 2]
    k += 2
    if emit_state:
        cnout_ref, mout_ref = refs[k:k + 2]
        k += 2
    cn_scr, m_scr = refs[k:k + 2]
    c = pl.program_id(1)

    @pl.when(c == 0)
    def _():
        if has_init:
            cn_scr[...] = cn0_ref[0, 0]
            m_scr[...] = m0_ref[0, 0]
        else:
            cn_scr[...] = jnp.zeros(cn_scr.shape, F32)
            m_scr[...] = jnp.zeros(m_scr.shape, F32)

    _interleave(_ml_dir(0, qf_ref, kf_ref, vf_ref, *gf, *consts, yf_ref, cn_scr, m_scr),
                _ml_dir(1, qb_ref, kb_ref, vb_ref, *gb, *consts, yb_ref, cn_scr, m_scr))

    if emit_state:
        @pl.when(c == pl.num_programs(1) - 1)
        def _():
            cnout_ref[0] = cn_scr[...]
            mout_ref[0] = m_scr[...]


def _mlstm(qk, u, gates, consts, nb, seq, states, layer):
    nc = seq // CHUNK
    t = nb * seq
    has_init = states is not None
    mvb = U_MV // D_MODEL

    def tok(col, rev):
        if rev:
            return pl.BlockSpec((CHUNK, D_MODEL), lambda b, c: (b * nc + nc - 1 - c, col))
        return pl.BlockSpec((CHUNK, D_MODEL), lambda b, c: (b * nc + c, col))

    gfs, gbs = _gate_specs(nc)
    in_specs = ([tok(0, False), tok(1, False), tok(mvb, False),
                 tok(0, True), tok(1, True), tok(mvb, True)] + gfs + gbs
                + [_const_spec(a) for a in consts])
    args = [qk, qk, u, qk, qk, u] + list(gates) + list(gates) + list(consts)
    cn_shape = (2, ML_HEADS, ML_HEAD_DIM, 2 * ML_HEAD_DIM)
    m_shape = (2, 1, D_MODEL)
    if has_init:
        in_specs += [pl.BlockSpec((1, 1) + cn_shape, lambda b, c: (b, layer, 0, 0, 0, 0)),
                     pl.BlockSpec((1, 1) + m_shape, lambda b, c: (b, layer, 0, 0, 0))]
        args += list(states)
    out_specs = [tok(0, False), tok(0, True)]
    out_shape = [jax.ShapeDtypeStruct((t, D_MODEL), BF16)] * 2
    if not has_init:
        out_specs += [pl.BlockSpec((1,) + cn_shape, lambda b, c: (b, 0, 0, 0, 0)),
                      pl.BlockSpec((1,) + m_shape, lambda b, c: (b, 0, 0, 0))]
        out_shape += [jax.ShapeDtypeStruct((nb,) + cn_shape, F32),
                      jax.ShapeDtypeStruct((nb,) + m_shape, F32)]
    return pl.pallas_call(
        functools.partial(_ml_kernel, has_init=has_init, emit_state=not has_init),
        grid=(nb, nc),
        in_specs=in_specs,
        out_specs=out_specs,
        out_shape=out_shape,
        scratch_shapes=[pltpu.VMEM(cn_shape, F32), pltpu.VMEM(m_shape, F32)],
        compiler_params=_cparams("parallel", "arbitrary"),
        name="mlstm_scan",
    )(*args)


def _qkprep_kernel(*refs, rope, emit_f32):
    u_ref, gain_ref = refs[0:2]
    k = 2
    if rope:
        cos_ref, sin_ref = refs[k:k + 2]
        k += 2
    q_ref, klo_ref, khi_ref, v1_ref = refs[k:k + 4]
    k += 4
    if emit_f32:
        kf_ref, vf_ref = refs[k:k + 2]
        k += 2
    nqk = D_MODEL + ATT_KV
    pair = 2 * ATT_HEAD_DIM
    low = lax.broadcasted_iota(jnp.int32, (u_ref.shape[0], pair), 1) < ATT_HEAD_DIM
    parts = []
    for p in range(nqk // pair):
        x = u_ref[:, p * pair:(p + 1) * pair].astype(F32)
        x2 = x * x
        ss = jnp.where(low, jnp.sum(jnp.where(low, x2, 0.0), axis=-1, keepdims=True),
                       jnp.sum(jnp.where(low, 0.0, x2), axis=-1, keepdims=True))
        parts.append(x * lax.rsqrt(ss * (1.0 / ATT_HEAD_DIM) + EPS))
    xn = jnp.concatenate(parts, axis=1) * gain_ref[...]
    v = u_ref[:, nqk:nqk + ATT_KV]
    if emit_f32:
        kf_ref[...] = xn[:, D_MODEL:nqk]
        vf_ref[...] = v.astype(F32)
    if rope:
        lane = lax.broadcasted_iota(jnp.int32, xn.shape, 1)
        quarter = ATT_HEAD_DIM // 4
        first = jnp.bitwise_and(lane, 2 * quarter - 1) < quarter
        swapped = jnp.where(first, pltpu.roll(xn, nqk - quarter, 1), pltpu.roll(xn, quarter, 1))
        xn = xn * cos_ref[...] + swapped * sin_ref[...]
    q_ref[...] = (xn[:, :D_MODEL] * Q_SCALE).astype(BF16)

    def spread(a, fill):
        lo_parts, hi_parts = [], []
        for c in (a[:, :pair], a[:, pair:]):
            r = pltpu.roll(c, ATT_HEAD_DIM, 1)
            lo_parts += [jnp.where(low, c, fill), jnp.where(low, r, fill)]
            hi_parts += [jnp.where(low, fill, r), jnp.where(low, fill, c)]
        return jnp.concatenate(lo_parts, axis=1), jnp.concatenate(hi_parts, axis=1)

    k_lo, k_hi = spread(xn[:, D_MODEL:nqk], 0.0)
    klo_ref[...] = k_lo.astype(BF16)
    khi_ref[...] = k_hi.astype(BF16)
    v1_ref[...] = spread(v.astype(F32), 1.0)[0].astype(BF16)


def _qkprep(u, gain, tables, nb, seq, emit_f32):
    t = nb * seq
    tl = min(seq, 256)
    nl = seq // tl
    nqk = D_MODEL + ATT_KV
    rope = tables is not None
    ub = U_ATT // ATT_QKV
    tok = lambda w: pl.BlockSpec((tl, w), lambda i, b: (b * nl + i, 0))
    in_specs = [pl.BlockSpec((tl, ATT_QKV), lambda i, b: (b * nl + i, ub)),
                pl.BlockSpec((1, nqk), lambda i, b: (0, 0))]
    args = [u, gain]
    if rope:
        in_specs += [pl.BlockSpec((tl, nqk), lambda i, b: (i, 0))] * 2
        args += list(tables)
    out_specs = [tok(D_MODEL), tok(ATT_KV_PAD), tok(ATT_KV_PAD), tok(ATT_KV_PAD)]
    out_shape = [jax.ShapeDtypeStruct((t, D_MODEL), BF16)] + [jax.ShapeDtypeStruct((t, ATT_KV_PAD), BF16)] * 3
    if emit_f32:
        out_specs += [tok(ATT_KV), tok(ATT_KV)]
        out_shape += [jax.ShapeDtypeStruct((t, ATT_KV), F32)] * 2
    return pl.pallas_call(
        functools.partial(_qkprep_kernel, rope=rope, emit_f32=emit_f32),
        grid=(nl, nb),
        in_specs=in_specs,
        out_specs=out_specs,
        out_shape=out_shape,
        compiler_params=_cparams("parallel", "parallel"),
        name="qk_prep",
    )(*args)


def _attn_kernel(*refs, has_cache, tq):
    q_ref, klo_ref, khi_ref, v1_ref = refs[0:4]
    k = 4
    if has_cache:
        kclo_ref, kchi_ref, vc1_ref = refs[k:k + 3]
        k += 3
    o_ref = refs[k]
    pair = 2 * ATT_HEAD_DIM
    low = lax.broadcasted_iota(jnp.int32, (tq, pair), 1) < ATT_HEAD_DIM
    units = [(kvh, which) for kvh in range(ATT_KV_HEADS) for which in (0, 1)]

    def keys_of(kvh):
        ks = slice(kvh * pair, (kvh + 1) * pair)
        keys = [(klo_ref[:, ks], khi_ref[:, ks], v1_ref[:, ks])]
        if has_cache:
            keys.append((kclo_ref[0, 0, :, ks], kchi_ref[0, 0, :, ks], vc1_ref[0, 0, :, ks]))
        return keys

    def score(u):
        kvh, which = units[u]
        qp = jnp.concatenate([q_ref[:, (2 * kvh) * pair:(2 * kvh + 1) * pair],
                              q_ref[:, (2 * kvh + 1) * pair:(2 * kvh + 2) * pair]], axis=0)
        return [_dot_nt(qp, kk[which]) for kk in keys_of(kvh)]

    def weights(scores):
        m = jnp.max(scores[0], axis=-1, keepdims=True)
        for s in scores[1:]:
            m = jnp.maximum(m, jnp.max(s, axis=-1, keepdims=True))
        return [jnp.exp2(s - m).astype(BF16) for s in scores]

    def values(u, probs):
        pv = None
        for p, kk in zip(probs, keys_of(units[u][0])):
            part = _dot(p, kk[2])
            pv = part if pv is None else pv + part
        return pv

    n = len(units)
    sc = {0: score(0), 1: score(1)}
    pr = {0: weights(sc.pop(0))}
    pv = {}
    for u in range(n):
        if u + 2 < n:
            sc[u + 2] = score(u + 2)
        pv[u] = values(u, pr.pop(u))
        if u + 1 < n:
            pr[u + 1] = weights(sc.pop(u + 1))
        if units[u][1] == 1:
            kvh = units[u][0]
            lo_half, hi_half = pv.pop(u - 1), pv.pop(u)
            first = lo_half / pltpu.roll(lo_half, ATT_HEAD_DIM, 1)
            second = pltpu.roll(hi_half, ATT_HEAD_DIM, 1) / hi_half
            for half in range(2):
                rows = slice(half * tq, (half + 1) * tq)
                o_ref[:, (2 * kvh + half) * pair:(2 * kvh + half + 1) * pair] = (
                    jnp.where(low, first[rows], second[rows]).astype(o_ref.dtype))


def _attn(q, klo, khi, v1, cache, nb, seq, layer):
    t = nb * seq
    tq = min(seq, 256)
    nq = seq // tq
    has_cache = cache is not None
    kv_spec = pl.BlockSpec((seq, ATT_KV_PAD), lambda b, i: (b, 0))
    in_specs = [pl.BlockSpec((tq, D_MODEL), lambda b, i: (b * nq + i, 0)), kv_spec, kv_spec, kv_spec]
    args = [q, klo, khi, v1]
    if has_cache:
        past = cache[0].shape[2]
        spec = pl.BlockSpec((1, 1, past, ATT_KV_PAD), lambda b, i: (b, layer, 0, 0))
        in_specs += [spec, spec, spec]
        args += list(cache)
    return pl.pallas_call(
        functools.partial(_attn_kernel, has_cache=has_cache, tq=tq),
        grid=(nb, nq),
        in_specs=in_specs,
        out_specs=pl.BlockSpec((tq, D_MODEL), lambda b, i: (b * nq + i, 0)),
        out_shape=jax.ShapeDtypeStruct((t, D_MODEL), BF16),
        compiler_params=_cparams("parallel", "parallel"),
        name="gqa",
    )(*args)


def _merge_kernel(yf_ref, yb_ref, z_ref, att_ref, hf_ref, hb_ref, mo_ref,
                  g0_ref, g1_ref, g2_ref, x_ref, gate_ref, sgain_ref, mgain_ref,
                  wb_ref, wo_ref, o_ref, b3_scr):
    ys = (yf_ref[...].astype(F32) + yb_ref[...].astype(F32)) * z_ref[...].astype(F32)
    b1 = _rms(ys) * sgain_ref[...]
    for hh in range(ML_HEADS):
        sl = slice(hh * ML_HEAD_DIM, (hh + 1) * ML_HEAD_DIM)
        b3_scr[:, sl] = _rms(hf_ref[:, sl].astype(F32) + hb_ref[:, sl].astype(F32))
    b3 = (b3_scr[...] * mgain_ref[...]) * mo_ref[...].astype(F32)
    merged = (g0_ref[...].astype(F32) * _dot(b1.astype(BF16), wb_ref[0])
              + g1_ref[...].astype(F32) * _dot(att_ref[...], wb_ref[1])
              + g2_ref[...].astype(F32) * _dot(b3.astype(BF16), wb_ref[2]))
    o_ref[...] = x_ref[...] + gate_ref[0] * _dot(merged.astype(BF16), wo_ref[...])


def _merge(yf, yb, u, att, hf, hb, x, gate1, sgain, mgain, wb, wo, seq, layer):
    t = x.shape[0]
    tm = min(t, 256)
    tok = lambda i: (i, 0)
    ucol = lambda off: pl.BlockSpec((tm, D_MODEL), lambda i: (i, off // D_MODEL))
    full = pl.BlockSpec((tm, D_MODEL), tok)
    vec = pl.BlockSpec((1, D_MODEL), lambda i: (0, 0))
    once = pl.Buffered(1)
    in_specs = [full, full, ucol(U_SZ), full, full, full, ucol(U_MO),
                ucol(U_G), ucol(U_G + D_MODEL), ucol(U_G + 2 * D_MODEL), full,
                _mod_spec(gate1.shape[0], seq, tm), vec, vec,
                pl.BlockSpec((None, 3, D_MODEL, D_MODEL), lambda i: (layer, 0, 0, 0), pipeline_mode=once),
                pl.BlockSpec((None, D_MODEL, D_MODEL), lambda i: (layer, 0, 0), pipeline_mode=once)]
    return pl.pallas_call(
        _merge_kernel,
        grid=(t // tm,),
        in_specs=in_specs,
        out_specs=full,
        out_shape=jax.ShapeDtypeStruct((t, D_MODEL), F32),
        scratch_shapes=[pltpu.VMEM((tm, D_MODEL), F32)],
        compiler_params=_cparams("parallel"),
        name="branch_merge",
    )(yf, yb, u, att, hf, hb, u, u, u, u, x, gate1, sgain, mgain, wb, wo)


def _ffn_kernel(x_ref, sc_ref, sh_ref, gate_ref, wi_ref, wo_ref, *rest, last):
    x = x_ref[...]
    h = (_rms(x) * (1.0 + sc_ref[0]) + sh_ref[0]).astype(BF16)
    a = _dot(h, wi_ref[:, :FFN_HIDDEN])
    b = _dot(h, wi_ref[:, FFN_HIDDEN:])
    act = (a * _sigmoid(a) * b).astype(BF16)
    y = x + gate_ref[0] * _dot(act, wo_ref[...])
    if last:
        gain_ref, o_ref = rest
        o_ref[...] = _rms(y) * gain_ref[...]
    else:
        nsc_ref, nsh_ref, o_ref, h_ref = rest
        o_ref[...] = y
        h_ref[...] = (_rms(y) * (1.0 + nsc_ref[0]) + nsh_ref[0]).astype(h_ref.dtype)


def _ffn(x, scale, shift, gate, wi, wo, seq, layer, nxt):
    t = x.shape[0]
    tm = min(t, 256)
    nb = scale.shape[0]
    once = pl.Buffered(1)
    full = pl.BlockSpec((tm, D_MODEL), lambda i: (i, 0))
    last = not isinstance(nxt, tuple)
    mod = _mod_spec(nb, seq, tm)
    in_specs = [full, mod, mod, mod,
                pl.BlockSpec((None, D_MODEL, 2 * FFN_HIDDEN), lambda i: (layer, 0, 0), pipeline_mode=once),
                pl.BlockSpec((None, FFN_HIDDEN, D_MODEL), lambda i: (layer, 0, 0), pipeline_mode=once)]
    if last:
        in_specs.append(pl.BlockSpec((1, D_MODEL), lambda i: (0, 0)))
        extra = [nxt]
        out_specs = full
        out_shape = jax.ShapeDtypeStruct((t, D_MODEL), F32)
    else:
        in_specs += [mod, mod]
        extra = list(nxt)
        out_specs = [full, full]
        out_shape = [jax.ShapeDtypeStruct((t, D_MODEL), F32), jax.ShapeDtypeStruct((t, D_MODEL), BF16)]
    return pl.pallas_call(
        functools.partial(_ffn_kernel, last=last),
        grid=(t // tm,),
        in_specs=in_specs,
        out_specs=out_specs,
        out_shape=out_shape,
        compiler_params=_cparams("parallel"),
        name="ffn",
    )(x, scale, shift, gate, wi, wo, *extra)


def _gate_consts():
    idx = np.arange(GATE_W)
    back = ((idx >= G_DT + SSD_HEADS) & (idx < G_LI)) | ((idx >= G_LF + ML_HEADS) & (idx < G_LF + 2 * ML_HEADS))
    r = np.arange(CHUNK)
    tlo = (r[:, None] >= r[None, :]).astype(np.float32)
    tup = (r[:, None] <= r[None, :]).astype(np.float32)
    return (jnp.asarray(back.astype(np.float32)), jnp.asarray(tlo, BF16), jnp.asarray(tup, BF16))


def _scan_consts():
    def expand(first, heads, width, parts):
        e = np.zeros((2, parts * GATE_W, heads * width), np.float32)
        for d in range(2):
            for h in range(heads):
                for part in range(parts):
                    e[d, part * GATE_W + first + d * heads + h, h * width:(h + 1) * width] = 1.0
        return jnp.asarray(e, BF16)

    ssd = (expand(G_DT, SSD_HEADS, SSD_HEAD_DIM, 1), expand(G_DT, SSD_HEADS, CHUNK, 2),
           jnp.asarray(np.eye(SSD_BC, dtype=np.float32), BF16))
    ml = (expand(G_LF, ML_HEADS, ML_HEAD_DIM, 2), jnp.asarray(np.eye(CHUNK, dtype=np.float32), BF16))
    return ssd, ml


def _ssd_state_to_compact(s):
    lead = s.shape[:-3]
    rep = SSD_HEADS // SSD_GROUPS
    s = s.reshape(lead + (SSD_GROUPS, rep, SSD_HEAD_DIM, SSD_STATE))
    nd = len(lead)
    s = jnp.transpose(s, tuple(range(nd)) + (nd, nd + 3, nd + 1, nd + 2))
    return s.reshape(lead + (SSD_GROUPS, SSD_STATE, SSD_GROUP_W))


def _ssd_state_from_compact(s):
    lead = s.shape[:-3]
    rep = SSD_HEADS // SSD_GROUPS
    s = s.reshape(lead + (SSD_GROUPS, SSD_STATE, rep, SSD_HEAD_DIM))
    nd = len(lead)
    s = jnp.transpose(s, tuple(range(nd)) + (nd, nd + 2, nd + 3, nd + 1))
    return s.reshape(lead + (SSD_HEADS, SSD_HEAD_DIM, SSD_STATE))


def _rope_tables(seq):
    pos = np.arange(seq)
    quarter = ATT_HEAD_DIM // 4
    freqs = jnp.asarray(ROPE_THETA, F32) ** (-jnp.arange(quarter, dtype=F32) / quarter)
    ang_r = jnp.asarray(pos // GRID_W, F32)[:, None] * freqs
    ang_c = jnp.asarray(pos % GRID_W, F32)[:, None] * freqs
    cos = jnp.concatenate([jnp.cos(ang_r)] * 2 + [jnp.cos(ang_c)] * 2, axis=-1)
    sin = jnp.concatenate([-jnp.sin(ang_r), jnp.sin(ang_r), -jnp.sin(ang_c), jnp.sin(ang_c)], axis=-1)
    reps = (D_MODEL + ATT_KV) // ATT_HEAD_DIM
    return jnp.tile(cos, (1, reps)), jnp.tile(sin, (1, reps))


def _split_w_in(w_in):
    sizes = (D_MODEL, D_MODEL, SSD_BC, SSD_BC, 2 * SSD_HEADS,
             D_MODEL, ATT_KV, ATT_KV,
             D_MODEL, D_MODEL, D_MODEL, D_MODEL, 4 * ML_HEADS, 3 * D_MODEL)
    offs = np.cumsum((0,) + sizes)
    return [w_in[:, :, offs[i]:offs[i + 1]] for i in range(len(sizes))]


def kernel(x_prompt, x_sample, cache_k, cache_v, state_ssd, state_ml_c, state_ml_n, state_ml_m,
           c, c_ctx, w_ada, b_ada, w_in, ssd_conv_w, ssd_conv_b, ssd_a_log, ssd_dt_bias, ssd_d,
           ssd_norm, att_q_norm, att_k_norm, ml_conv_w, ml_conv_b, ml_gate_bias, ml_norm,
           w_branch, w_out, w_ffn_in, w_ffn_out, final_norm):
    depth = w_in.shape[0]
    nbp, seqp, _ = x_prompt.shape
    nbs, seqs, _ = x_sample.shape
    past = cache_k.shape[2]

    (s_x, s_z, s_b, s_c, s_dt, a_q, a_k, a_v, m_q, m_k, m_v, m_o, m_g, g) = _split_w_in(w_in)
    w_main = jnp.concatenate([s_x, s_b, s_c, a_q, a_k, a_v, m_q, m_k, m_v, s_z, m_o, g], axis=-1).astype(BF16)
    m_g4 = m_g.reshape(depth, D_MODEL, 2, 2, ML_HEADS)
    w_small = jnp.concatenate(
        [s_dt, m_g4[:, :, :, 0].reshape(depth, D_MODEL, 2 * ML_HEADS),
         m_g4[:, :, :, 1].reshape(depth, D_MODEL, 2 * ML_HEADS),
         jnp.zeros((depth, D_MODEL, GATE_W - G_LF - 2 * ML_HEADS), F32)], axis=-1).astype(BF16)
    w_small_t = jnp.swapaxes(w_small, 1, 2)
    pad = jnp.zeros((depth, GATE_W - G_LF - 2 * ML_HEADS), F32)
    gate_bias = jnp.concatenate(
        [ssd_dt_bias.reshape(depth, 2 * SSD_HEADS), ml_gate_bias[:, :, 0].reshape(depth, 2 * ML_HEADS),
         ml_gate_bias[:, :, 1].reshape(depth, 2 * ML_HEADS), pad], axis=-1)
    a_vec = jnp.concatenate(
        [-jnp.exp(ssd_a_log.reshape(depth, 2 * SSD_HEADS)), jnp.zeros((depth, GATE_W - 2 * SSD_HEADS), F32)], axis=-1)
    d_vec = jnp.repeat(ssd_d, SSD_HEAD_DIM, axis=-1).reshape(depth, 1, D_MODEL)
    qk_gain = jnp.concatenate([jnp.tile(att_q_norm, (1, ATT_HEADS)), jnp.tile(att_k_norm, (1, ATT_KV_HEADS))], axis=-1)
    qk_gain = qk_gain.reshape(depth, 1, D_MODEL + ATT_KV)
    w_branch_b = w_branch.astype(BF16)
    w_out_b = w_out.astype(BF16)
    w_ffn_in_b = w_ffn_in.astype(BF16)
    w_ffn_out_b = w_ffn_out.astype(BF16)
    consts = _gate_consts()
    tables = _rope_tables(seqs)
    zeros = jnp.zeros(cache_k.shape, BF16)
    ck, cv = cache_k.astype(BF16), cache_v.astype(BF16)
    pad_shape = (nbs, depth, past, ATT_KV_PAD)
    cache = (jnp.concatenate([ck, zeros], axis=-1).reshape(pad_shape),
             jnp.concatenate([zeros, ck], axis=-1).reshape(pad_shape),
             jnp.concatenate([cv, jnp.ones(cache_v.shape, BF16)], axis=-1).reshape(pad_shape))
    ssd_consts, ml_consts = _scan_consts()
    ssd_h0 = _ssd_state_to_compact(state_ssd)
    ml_cn0 = jnp.concatenate(
        [state_ml_c, jnp.broadcast_to(state_ml_n[..., None], state_ml_c.shape)], axis=-1)
    ml_m0 = jnp.repeat(state_ml_m, ML_HEAD_DIM, axis=-1).reshape(nbs, depth, 2, 1, D_MODEL)

    rows = 8 * ((1 + nbs + 7) // 8)
    cc = jnp.concatenate([c_ctx[None], c, jnp.zeros((rows - 1 - nbs, D_MODEL), F32)], axis=0)
    mod = _ada(cc, w_ada, b_ada)

    def mods(layer, lo, hi):
        m = mod[layer, lo:hi].reshape(hi - lo, 1, 6, D_MODEL)
        return [m[:, :, i] for i in range(6)]

    def layer_step(x, h, layer, nb, seq, mod6, nxt, ctx):
        shift1, scale1, gate1, shift2, scale2, gate2 = mod6
        u = _proj(h, w_main, layer)
        gates = _gate(h, w_small[layer], w_small_t[layer], gate_bias[layer], a_vec[layer], consts)
        sx = _conv(u, U_CONV, D_MODEL, ssd_conv_w[layer, :, :D_MODEL], ssd_conv_b[layer, :D_MODEL], nb, seq, F32)
        sbc = _conv(u, U_CONV + D_MODEL, 2 * SSD_BC, ssd_conv_w[layer, :, D_MODEL:], ssd_conv_b[layer, D_MODEL:],
                    nb, seq, BF16)
        mqk = _conv(u, U_MQ, 2 * D_MODEL, ml_conv_w[layer], ml_conv_b[layer], nb, seq, BF16)
        if ctx:
            yf, yb = _ssd(sx, sbc, gates, ssd_consts, d_vec[layer], nb, seq, ssd_h0, layer)
            q, klo, khi, v1 = _qkprep(u, qk_gain[layer], tables, nb, seq, False)
            att = _attn(q, klo, khi, v1, cache, nb, seq, layer)
            hf, hb = _mlstm(mqk, u, gates, ml_consts, nb, seq, (ml_cn0, ml_m0), layer)
            new = None
        else:
            yf, yb, hst = _ssd(sx, sbc, gates, ssd_consts, d_vec[layer], nb, seq, None, layer)
            q, klo, khi, v1, kf, vf = _qkprep(u, qk_gain[layer], None, nb, seq, True)
            att = _attn(q, klo, khi, v1, None, nb, seq, layer)
            hf, hb, cn, mrow = _mlstm(mqk, u, gates, ml_consts, nb, seq, None, layer)
            new = (kf.reshape(nb, seq, ATT_KV_HEADS, ATT_HEAD_DIM), vf.reshape(nb, seq, ATT_KV_HEADS, ATT_HEAD_DIM),
                   _ssd_state_from_compact(hst), cn[..., :ML_HEAD_DIM], cn[..., ML_HEAD_DIM],
                   mrow[:, :, 0, ::ML_HEAD_DIM])
        x = _merge(yf, yb, u, att, hf, hb, x, gate1,
                   ssd_norm[layer].reshape(1, D_MODEL), ml_norm[layer].reshape(1, D_MODEL),
                   w_branch_b, w_out_b, seq, layer)
        return _ffn(x, scale2, shift2, gate2, w_ffn_in_b, w_ffn_out_b, seq, layer, nxt), new

    def run_group(x, nb, seq, lo, hi, ctx):
        mod6 = [mods(layer, lo, hi) for layer in range(depth)]
        h = _norm_mod(x, mod6[0][1], mod6[0][0], seq)
        news = []
        for layer in range(depth):
            if layer + 1 < depth:
                nxt = (mod6[layer + 1][1], mod6[layer + 1][0])
                (x, h), new = layer_step(x, h, layer, nb, seq, mod6[layer], nxt, ctx)
            else:
                x, new = layer_step(x, h, layer, nb, seq, mod6[layer], final_norm.reshape(1, D_MODEL), ctx)
            news.append(new)
        return x, news

    y_p, news = run_group(x_prompt.reshape(nbp * seqp, D_MODEL), nbp, seqp, 0, 1, False)
    y_s, _ = run_group(x_sample.reshape(nbs * seqs, D_MODEL), nbs, seqs, 1, 1 + nbs, True)
    y_prompt = y_p.reshape(x_prompt.shape)
    y_sample = y_s.reshape(x_sample.shape)
    stacked = [jnp.stack([n[i] for n in news], axis=1) for i in range(6)]
    return (y_prompt, y_sample) + tuple(stacked)
```

```python
import functools

import numpy as np
import jax
import jax.numpy as jnp
from jax import lax
from jax.experimental import pallas as pl
from jax.experimental.pallas import tpu as pltpu

F32 = jnp.float32
BF16 = jnp.bfloat16

D_MODEL = 1024
GRID_W = 64
EPS = 1e-6
CONV_W = 4
CHUNK = 128
SSD_HEADS = 16
SSD_HEAD_DIM = 64
SSD_GROUPS = 4
SSD_STATE = 64
SSD_BC = SSD_GROUPS * SSD_STATE
SSD_CONV = D_MODEL + 2 * SSD_BC
ATT_HEADS = 16
ATT_HEAD_DIM = 64
ATT_KV_HEADS = 4
ATT_KV = ATT_KV_HEADS * ATT_HEAD_DIM
ATT_QKV = D_MODEL + 2 * ATT_KV
ATT_KV_PAD = ATT_KV_HEADS * 2 * ATT_HEAD_DIM
Q_SCALE = ATT_HEAD_DIM ** -0.5 * 1.4426950408889634
ROPE_THETA = 10000.0
ML_HEADS = 8
ML_HEAD_DIM = 128
FFN_HIDDEN = 2816
GATE_W = 128

U_TILE = 512
U_MQ = 0
U_MK = 1024
U_SX = 2048
U_SBC = 3072
U_CONV_W = 3584
U_AKV = 3584
U_AQ = 4096
U_MV = 5120
U_SZ = 6144
U_MO = 7168
U_G = 8192
U_W = 11264

G_DT = 0
G_LI = 32
G_LF = 48

VMEM_LIMIT = 48 * 1024 * 1024
NT_DIMS = (((1,), (1,)), ((), ()))
TN_DIMS = (((0,), (0,)), ((), ()))


def _cparams(*sem):
    return pltpu.CompilerParams(dimension_semantics=sem, vmem_limit_bytes=VMEM_LIMIT)


def _sigmoid(x):
    return 1.0 / (1.0 + jnp.exp(-x))


def _softplus(x):
    return jnp.maximum(x, 0.0) + jnp.log1p(jnp.exp(-jnp.abs(x)))


def _dot(a, b):
    return jnp.dot(a, b, preferred_element_type=F32)


def _dot_nt(a, b):
    return lax.dot_general(a, b, NT_DIMS, preferred_element_type=F32)


def _dot_tn(a, b):
    return lax.dot_general(a, b, TN_DIMS, preferred_element_type=F32)


def _ada_kernel(c_ref, w_ref, b_ref, o_ref):
    c = c_ref[...]
    s = (c * _sigmoid(c)).astype(BF16)
    o_ref[0] = _dot(s, w_ref[0].astype(BF16)) + b_ref[0]


def _ada(cc, w_ada, b_ada):
    depth, _, n6 = w_ada.shape
    rows = cc.shape[0]
    tn = 512
    return pl.pallas_call(
        _ada_kernel,
        grid=(depth, n6 // tn),
        in_specs=[pl.BlockSpec((rows, D_MODEL), lambda l, j: (0, 0)),
                  pl.BlockSpec((1, D_MODEL, tn), lambda l, j: (l, 0, j)),
                  pl.BlockSpec((1, 1, tn), lambda l, j: (l, 0, j))],
        out_specs=pl.BlockSpec((1, rows, tn), lambda l, j: (l, 0, j)),
        out_shape=jax.ShapeDtypeStruct((depth, rows, n6), F32),
        compiler_params=_cparams("parallel", "parallel"),
        name="ada_mod",
    )(cc, w_ada, b_ada.reshape(depth, 1, n6))


def _rms(x):
    return x * lax.rsqrt(jnp.mean(x * x, axis=-1, keepdims=True) + EPS)


def _norm_mod_kernel(x_ref, sc_ref, sh_ref, o_ref):
    o_ref[...] = (_rms(x_ref[...]) * (1.0 + sc_ref[0]) + sh_ref[0]).astype(o_ref.dtype)


def _mod_spec(nb, seq, tm):
    if nb == 1:
        return pl.BlockSpec((1, 1, D_MODEL), lambda i: (0, 0, 0))
    return pl.BlockSpec((1, 1, D_MODEL), lambda i: ((i * tm) // seq, 0, 0))


def _norm_mod(x, scale, shift, seq):
    t = x.shape[0]
    tm = min(t, 512)
    nb = scale.shape[0]
    return pl.pallas_call(
        _norm_mod_kernel,
        grid=(t // tm,),
        in_specs=[pl.BlockSpec((tm, D_MODEL), lambda i: (i, 0)),
                  _mod_spec(nb, seq, tm), _mod_spec(nb, seq, tm)],
        out_specs=pl.BlockSpec((tm, D_MODEL), lambda i: (i, 0)),
        out_shape=jax.ShapeDtypeStruct((t, D_MODEL), BF16),
        compiler_params=_cparams("parallel"),
        name="norm_mod",
    )(x, scale, shift)


def _sigmoid_tanh(x):
    return 0.5 * jnp.tanh(0.5 * x) + 0.5


def _silu_tanh(x):
    h = 0.5 * x
    return h * jnp.tanh(h) + h


CONV_HALO = 16


def _conv_silu(x, w_ref, b_ref, seq, row0):
    rows = x.shape[0]
    pos = jnp.bitwise_and(lax.broadcasted_iota(jnp.int32, x.shape, 0) + row0, seq - 1)
    acc = x * w_ref[1:2, :] + b_ref[...]
    acc = acc + jnp.where(pos >= 1, pltpu.roll(x, 1, 0), 0.0) * w_ref[0:1, :]
    acc = acc + jnp.where(pos < seq - 1, pltpu.roll(x, rows - 1, 0), 0.0) * w_ref[2:3, :]
    acc = acc + jnp.where(pos < seq - 2, pltpu.roll(x, rows - 2, 0), 0.0) * w_ref[3:4, :]
    return _silu_tanh(acc)


def _proj_kernel(a_ref, w_ref, cw_ref, cb_ref, o_ref, sx_ref, *, conv_hi, sx_lo, sx_hi, silu_lo, sig_lo, sub, seq):
    j = pl.program_id(1)
    w = w_ref[...]

    def run(act):
        for r in range(a_ref.shape[0] // sub):
            rows = slice(r * sub, (r + 1) * sub)
            o_ref[rows, :] = act(_dot(a_ref[rows, :], w)).astype(o_ref.dtype)

    def run_conv(emit_f32):
        tm = a_ref.shape[0]
        for r in range(tm // sub):
            r0 = r * sub
            lo, hi = max(r0 - CONV_HALO, 0), min(r0 + sub + CONV_HALO, tm)
            y = _conv_silu(_dot(a_ref[lo:hi, :], w), cw_ref, cb_ref, seq, lo)[r0 - lo:r0 - lo + sub]
            o_ref[r0:r0 + sub, :] = y.astype(o_ref.dtype)
            if emit_f32:
                sx_ref[r0:r0 + sub, :] = y

    is_sx = jnp.logical_and(j >= sx_lo, j < sx_hi)

    @pl.when(jnp.logical_and(j < conv_hi, jnp.logical_not(is_sx)))
    def _():
        run_conv(False)

    @pl.when(is_sx)
    def _():
        run_conv(True)

    @pl.when(jnp.logical_and(j >= conv_hi, j < silu_lo))
    def _():
        run(lambda acc: acc)

    @pl.when(jnp.logical_and(j >= silu_lo, j < sig_lo))
    def _():
        run(_silu_tanh)

    @pl.when(j >= sig_lo)
    def _():
        run(_sigmoid_tanh)


def _proj(h, w_main, conv_w, conv_b, layer, seq):
    t = h.shape[0]
    tm = min(t, 2048)
    tn = U_TILE
    assert tm % seq == 0 and seq & (seq - 1) == 0
    conv_hi = U_CONV_W // tn
    sx_lo, sx_hi = U_SX // tn, (U_SX + D_MODEL) // tn
    kern = functools.partial(_proj_kernel, conv_hi=conv_hi, sx_lo=sx_lo, sx_hi=sx_hi,
                             silu_lo=U_SZ // tn, sig_lo=U_MO // tn, sub=min(tm, 512), seq=seq)
    conv_col = lambda i, j: (layer, 0, jnp.minimum(j, conv_hi - 1))
    return pl.pallas_call(
        kern,
        grid=(t // tm, U_W // tn),
        in_specs=[pl.BlockSpec((tm, D_MODEL), lambda i, j: (i, 0)),
                  pl.BlockSpec((None, D_MODEL, tn), lambda i, j: (layer, 0, j)),
                  pl.BlockSpec((None, CONV_W, tn), conv_col),
                  pl.BlockSpec((None, 1, tn), conv_col)],
        out_specs=[pl.BlockSpec((tm, tn), lambda i, j: (i, j)),
                   pl.BlockSpec((tm, tn), lambda i, j: (i, jnp.clip(j - sx_lo, 0, sx_hi - sx_lo - 1)))],
        out_shape=[jax.ShapeDtypeStruct((t, U_W), BF16), jax.ShapeDtypeStruct((t, D_MODEL), F32)],
        compiler_params=_cparams("parallel", "arbitrary"),
        name="in_proj",
    )(h, w_main, conv_w, conv_b)


def _split3(x):
    hi = x.astype(BF16)
    r1 = x - hi.astype(F32)
    mid = r1.astype(BF16)
    lo = (r1 - mid.astype(F32)).astype(BF16)
    return hi, mid, lo


def _tri_left(tri, x):
    hi, mid, lo = _split3(x)
    return _dot(tri, hi) + _dot(tri, mid) + _dot(tri, lo)


def _tri_right(x, tri):
    hi, mid, lo = _split3(x)
    return _dot(hi, tri) + _dot(mid, tri) + _dot(lo, tri)


def _gate_vals(u, idx, a):
    sp = _softplus(u)
    ls = -_softplus(-u)
    val = jnp.where(idx < G_LI, sp, jnp.where(idx < G_LF, u, jnp.where(idx < G_LF + 16, ls, 0.0)))
    cin = jnp.where(idx < G_LI, sp * a, jnp.where(jnp.logical_and(idx >= G_LF, idx < G_LF + 16), ls, 0.0))
    return val, cin


def _gate_kernel(h_ref, ws_ref, wst_ref, b_ref, bt_ref, a_ref, at_ref, dm_ref, dmt_ref,
                 tlo_ref, tup_ref, val_ref, cum_ref, valt_ref, cumt_ref):
    h = h_ref[...]
    tlo = tlo_ref[...]
    tup = tup_ref[...]
    rows = h.shape[0]
    u = _dot(h, ws_ref[...]) + b_ref[...]
    val, cin = _gate_vals(u, lax.broadcasted_iota(jnp.int32, (rows, GATE_W), 1), a_ref[...])
    val_ref[...] = val
    ut = _dot_nt(wst_ref[...], h) + bt_ref[...]
    valt, cint = _gate_vals(ut, lax.broadcasted_iota(jnp.int32, (GATE_W, rows), 0), at_ref[...])
    valt_ref[...] = valt
    back = dm_ref[...] > 0.5
    back_t = dmt_ref[...] > 0.5
    for c in range(rows // CHUNK):
        sl = slice(c * CHUNK, (c + 1) * CHUNK)
        cum_ref[sl, :] = jnp.where(back, _tri_left(tup, cin[sl]), _tri_left(tlo, cin[sl]))
        cumt_ref[:, sl] = jnp.where(back_t, _tri_right(cint[:, sl], tlo), _tri_right(cint[:, sl], tup))


def _gate(h, ws, wst, bias, avec, consts):
    t = h.shape[0]
    tm = min(t, 4 * CHUNK)
    dmask, tlo, tup = consts
    row = lambda i: (i, 0)
    col = lambda i: (0, i)
    fix = lambda i: (0, 0)
    tm_out = jax.ShapeDtypeStruct((t, GATE_W), F32)
    fm_out = jax.ShapeDtypeStruct((GATE_W, t), F32)
    return pl.pallas_call(
        _gate_kernel,
        grid=(t // tm,),
        in_specs=[pl.BlockSpec((tm, D_MODEL), row),
                  pl.BlockSpec((D_MODEL, GATE_W), fix),
                  pl.BlockSpec((GATE_W, D_MODEL), fix),
                  pl.BlockSpec((1, GATE_W), fix), pl.BlockSpec((GATE_W, 1), fix),
                  pl.BlockSpec((1, GATE_W), fix), pl.BlockSpec((GATE_W, 1), fix),
                  pl.BlockSpec((1, GATE_W), fix), pl.BlockSpec((GATE_W, 1), fix),
                  pl.BlockSpec((CHUNK, CHUNK), fix), pl.BlockSpec((CHUNK, CHUNK), fix)],
        out_specs=[pl.BlockSpec((tm, GATE_W), row), pl.BlockSpec((tm, GATE_W), row),
                   pl.BlockSpec((GATE_W, tm), col), pl.BlockSpec((GATE_W, tm), col)],
        out_shape=[tm_out, tm_out, fm_out, fm_out],
        compiler_params=_cparams("parallel"),
        name="gate_prep",
    )(h, ws, wst, bias.reshape(1, GATE_W), bias.reshape(GATE_W, 1),
      avec.reshape(1, GATE_W), avec.reshape(GATE_W, 1),
      dmask.reshape(1, GATE_W), dmask.reshape(GATE_W, 1), tlo, tup)


def _tri_mask(fwd):
    row = lax.broadcasted_iota(jnp.int32, (CHUNK, CHUNK), 0)
    col = lax.broadcasted_iota(jnp.int32, (CHUNK, CHUNK), 1)
    return row >= col if fwd else row <= col


def _lane_expand(x, e):
    hi, mid, _ = _split3(x)
    return _dot(jnp.concatenate([hi, mid], axis=1), e)


SSD_GROUP_W = (SSD_HEADS // SSD_GROUPS) * SSD_HEAD_DIM


def _ssd_dir(d, x_ref, bc_ref, val_ref, cum_ref, valt_ref, cumt_ref, e64_ref, e128_ref, eye_ref, skip,
             y_ref, st_scr):
    mask = _tri_mask(d == 0)
    end = CHUNK - 1 if d == 0 else 0
    x = x_ref[...]
    bb = bc_ref[:, :SSD_BC]
    cm = bc_ref[:, SSD_BC:]
    lane_c = lax.broadcasted_iota(jnp.int32, cm.shape, 1)
    lane_x = lax.broadcasted_iota(jnp.int32, (CHUNK, 2 * SSD_HEAD_DIM), 1)
    a128 = _lane_expand(cum_ref[...], e128_ref[d])
    a64 = jnp.concatenate(
        [jnp.where(lane_x < SSD_HEAD_DIM, a128[:, (2 * p) * CHUNK:(2 * p + 1) * CHUNK],
                   a128[:, (2 * p + 1) * CHUNK:(2 * p + 2) * CHUNK]) for p in range(SSD_HEADS // 2)], axis=1)
    dt64 = _dot(val_ref[...].astype(BF16), e64_ref[d])
    dt_r = valt_ref[...]
    ac_r = cumt_ref[...]
    st = st_scr[d]
    inter = _dot(cm, st.astype(BF16))
    ea = jnp.exp(a64)
    cm32 = cm.astype(F32)
    c_stack = jnp.concatenate(
        [jnp.where(jnp.logical_and(lane_c >= g * SSD_STATE, lane_c < (g + 1) * SSD_STATE), cm32, 0.0).astype(BF16)
         for g in range(SSD_GROUPS)], axis=0)
    cbt_all = _dot_nt(c_stack, bb)
    bt = _dot_nt(eye_ref[...], bb).astype(BF16)
    yield
    tot = a64[end:end + 1, :]
    xw = (x * (jnp.exp(tot - a64) * dt64)).astype(BF16)
    decay = jnp.exp(tot)
    for g in range(SSD_GROUPS):
        rs = slice(g * SSD_STATE, (g + 1) * SSD_STATE)
        cs = slice(g * SSD_GROUP_W, (g + 1) * SSD_GROUP_W)
        st_scr[d, rs, cs] = decay[:, cs] * st[rs, cs] + _dot(bt[rs, :], xw[:, cs])
    yield
    n_pairs = SSD_HEADS // 2
    weights, xbds = [], []
    for p in range(n_pairs):
        cbt = cbt_all[(p // 2) * CHUNK:(p // 2 + 1) * CHUNK]
        parts = []
        for hh in (2 * p, 2 * p + 1):
            lane = G_DT + d * SSD_HEADS + hh
            seg = jnp.where(mask, a128[:, hh * CHUNK:(hh + 1) * CHUNK] - ac_r[lane:lane + 1, :], -jnp.inf)
            parts.append((cbt * jnp.exp(seg) * dt_r[lane:lane + 1, :]).astype(BF16))
        weights.append(jnp.concatenate(parts, axis=1))
        xp = x[:, p * 2 * SSD_HEAD_DIM:(p + 1) * 2 * SSD_HEAD_DIM]
        xbds.append(jnp.concatenate([jnp.where(lane_x < SSD_HEAD_DIM, xp, 0.0),
                                     jnp.where(lane_x >= SSD_HEAD_DIM, xp, 0.0)], axis=0).astype(BF16))
    yield
    for p in range(n_pairs):
        cs = slice(p * 2 * SSD_HEAD_DIM, (p + 1) * 2 * SSD_HEAD_DIM)
        y = _dot(weights[p], xbds[p]) + ea[:, cs] * inter[:, cs]
        if skip is not None:
            y = y + skip[:, cs] * x[:, cs]
        y_ref[:, cs] = y.astype(y_ref.dtype)


def _ssd_kernel(*refs, has_h0, emit_state):
    xf_ref, bcf_ref, xb_ref, bcb_ref = refs[0:4]
    gf = refs[4:8]
    gb = refs[8:12]
    consts = refs[12:15]
    dvec_ref = refs[15]
    k = 16
    if has_h0:
        h0_ref = refs[k]
        k += 1
    yf_ref, yb_ref = refs[k:k + 2]
    k += 2
    if emit_state:
        hout_ref = refs[k]
        k += 1
    st_scr = refs[k]
    c = pl.program_id(1)

    @pl.when(c == 0)
    def _():
        st_scr[...] = jnp.zeros(st_scr.shape, F32)
        if has_h0:
            for d in range(2):
                for g in range(SSD_GROUPS):
                    st_scr[d, g * SSD_STATE:(g + 1) * SSD_STATE,
                           g * SSD_GROUP_W:(g + 1) * SSD_GROUP_W] = h0_ref[0, 0, d, g]

    _interleave(_ssd_dir(0, xf_ref, bcf_ref, *gf, *consts, dvec_ref[...], yf_ref, st_scr),
                _ssd_dir(1, xb_ref, bcb_ref, *gb, *consts, None, yb_ref, st_scr))

    if emit_state:
        @pl.when(c == pl.num_programs(1) - 1)
        def _():
            for d in range(2):
                for g in range(SSD_GROUPS):
                    hout_ref[0, d, g] = st_scr[d, g * SSD_STATE:(g + 1) * SSD_STATE,
                                               g * SSD_GROUP_W:(g + 1) * SSD_GROUP_W]


def _chunk_specs(nc, width):
    fwd = pl.BlockSpec((CHUNK, width), lambda b, c: (b * nc + c, 0))
    bwd = pl.BlockSpec((CHUNK, width), lambda b, c: (b * nc + nc - 1 - c, 0))
    return fwd, bwd


def _gate_specs(nc):
    tf, tb = _chunk_specs(nc, GATE_W)
    ff = pl.BlockSpec((GATE_W, CHUNK), lambda b, c: (0, b * nc + c))
    fb = pl.BlockSpec((GATE_W, CHUNK), lambda b, c: (0, b * nc + nc - 1 - c))
    return [tf, tf, ff, ff], [tb, tb, fb, fb]


def _const_spec(arr):
    nd = arr.ndim
    return pl.BlockSpec(arr.shape, lambda b, c: (0,) * nd)


def _ssd(x, u, gates, consts, dvec, nb, seq, state, layer):
    nc = seq // CHUNK
    t = nb * seq
    has_h0 = state is not None
    xf, xb = _chunk_specs(nc, D_MODEL)
    bcol = U_SBC // (2 * SSD_BC)
    bcf = pl.BlockSpec((CHUNK, 2 * SSD_BC), lambda b, c: (b * nc + c, bcol))
    bcb = pl.BlockSpec((CHUNK, 2 * SSD_BC), lambda b, c: (b * nc + nc - 1 - c, bcol))
    gfs, gbs = _gate_specs(nc)
    yf, yb = _chunk_specs(nc, D_MODEL)
    st_shape = (2, SSD_GROUPS, SSD_STATE, SSD_GROUP_W)
    in_specs = [xf, bcf, xb, bcb] + gfs + gbs + [_const_spec(a) for a in consts] + [_const_spec(dvec)]
    args = [x, u, x, u] + list(gates) + list(gates) + list(consts) + [dvec]
    if has_h0:
        in_specs.append(pl.BlockSpec((1, 1) + st_shape, lambda b, c: (b, layer, 0, 0, 0, 0)))
        args.append(state)
    out_specs = [yf, yb]
    out_shape = [jax.ShapeDtypeStruct((t, D_MODEL), BF16)] * 2
    if not has_h0:
        out_specs.append(pl.BlockSpec((1,) + st_shape, lambda b, c: (b, 0, 0, 0, 0)))
        out_shape.append(jax.ShapeDtypeStruct((nb,) + st_shape, F32))
    return pl.pallas_call(
        functools.partial(_ssd_kernel, has_h0=has_h0, emit_state=not has_h0),
        grid=(nb, nc),
        in_specs=in_specs,
        out_specs=out_specs,
        out_shape=out_shape,
        scratch_shapes=[pltpu.VMEM((2, SSD_BC, D_MODEL), F32)],
        compiler_params=_cparams("parallel", "arbitrary"),
        name="ssd_scan",
    )(*args)


def _ml_dir(d, q_ref, k_ref, v_ref, val_ref, cum_ref, valt_ref, cumt_ref, eb_ref, eye_ref,
            y_ref, cn_scr, m_scr):
    mask = _tri_mask(d == 0)
    end = CHUNK - 1 if d == 0 else 0
    heads = range(ML_HEADS)
    sls = [slice(hh * ML_HEAD_DIM, (hh + 1) * ML_HEAD_DIM) for hh in heads]
    kscale = ML_HEAD_DIM ** -0.5
    b128 = _lane_expand(cum_ref[...], eb_ref[d])
    li_r = valt_ref[...]
    bc_r = cumt_ref[...]
    m_row = m_scr[d]
    q = q_ref[...]
    k = k_ref[...]
    v = v_ref[...]
    ones = jnp.ones((CHUNK, ML_HEAD_DIM), BF16)
    eye = eye_ref[...]
    v1 = [jnp.concatenate([v[:, sl], ones], axis=1) for sl in sls]
    cn = [cn_scr[d, hh] for hh in heads]
    kt = [_dot_nt(eye, k[:, sl]) for sl in sls]
    qk = [_dot_nt(q[:, sl], (k[:, sl].astype(F32) * kscale).astype(BF16)) for sl in sls]
    qc = [_dot(q[:, sl], cn[hh].astype(BF16)) for hh, sl in zip(heads, sls)]
    yield
    b_end = b128[end:end + 1, :]
    bm = b128 + m_row
    kwt, dch, m_parts, s, inter, m_t = [], [], [], [], [], []
    for hh, sl in zip(heads, sls):
        b_j = bc_r[G_LF + d * ML_HEADS + hh:G_LF + d * ML_HEADS + hh + 1, :]
        i_j = li_r[G_LI + d * ML_HEADS + hh:G_LI + d * ML_HEADS + hh + 1, :]
        wj = b_end[:, sl] - b_j + i_j
        m_new = jnp.maximum(b_end[:, sl] + m_row[:, sl], jnp.max(wj, axis=1, keepdims=True))
        kwt.append((kt[hh] * (jnp.exp(wj - m_new) * kscale)).astype(BF16))
        dch.append(jnp.exp(b_end[:, sl] + m_row[:, sl] - m_new))
        m_parts.append(m_new)
        dmat = jnp.where(mask, b128[:, sl] - b_j + i_j, -jnp.inf)
        mt = jnp.maximum(bm[:, sl], jnp.max(dmat, axis=1, keepdims=True))
        s.append((qk[hh] * jnp.exp(dmat - mt)).astype(BF16))
        inter.append(jnp.exp(bm[:, sl] - mt))
        m_t.append(mt)
    yield
    upd = [_dot(kwt[hh], v1[hh]) for hh in heads]
    sv = [_dot(s[hh], v1[hh]) for hh in heads]
    yield
    for hh, sl in zip(heads, sls):
        r = sv[hh] + jnp.concatenate([inter[hh], inter[hh]], axis=1) * qc[hh]
        y_ref[:, sl] = (r[:, :ML_HEAD_DIM]
                        / jnp.maximum(jnp.abs(r[:, ML_HEAD_DIM:]), jnp.exp(-m_t[hh]))).astype(y_ref.dtype)
        cn_scr[d, hh] = jnp.concatenate([dch[hh], dch[hh]], axis=1) * cn[hh] + upd[hh]
    m_scr[d] = jnp.concatenate(m_parts, axis=1)


def _interleave(*gens):
    live = list(gens)
    while live:
        still = []
        for g in live:
            try:
                next(g)
                still.append(g)
            except StopIteration:
                pass
        live = still


def _ml_kernel(*refs, has_init, emit_state):
    qf_ref, kf_ref, vf_ref, qb_ref, kb_ref, vb_ref = refs[0:6]
    gf = refs[6:10]
    gb = refs[10:14]
    consts = refs[14:16]
    k = 16
    if has_init:
        cn0_ref, m0_ref = refs[k:k + 2]
        k += 2
    yf_ref, yb_ref = refs[k:k + 2]
    k += 2
    if emit_state:
        cnout_ref, mout_ref = refs[k:k + 2]
        k += 2
    cn_scr, m_scr = refs[k:k + 2]
    c = pl.program_id(1)

    @pl.when(c == 0)
    def _():
        if has_init:
            cn_scr[...] = cn0_ref[0, 0]
            m_scr[...] = m0_ref[0, 0]
        else:
            cn_scr[...] = jnp.zeros(cn_scr.shape, F32)
            m_scr[...] = jnp.zeros(m_scr.shape, F32)

    _interleave(_ml_dir(0, qf_ref, kf_ref, vf_ref, *gf, *consts, yf_ref, cn_scr, m_scr),
                _ml_dir(1, qb_ref, kb_ref, vb_ref, *gb, *consts, yb_ref, cn_scr, m_scr))

    if emit_state:
        @pl.when(c == pl.num_programs(1) - 1)
        def _():
            cnout_ref[0] = cn_scr[...]
            mout_ref[0] = m_scr[...]


def _mlstm(u, gates, consts, nb, seq, states, layer):
    nc = seq // CHUNK
    t = nb * seq
    has_init = states is not None
    mqb, mkb, mvb = U_MQ // D_MODEL, U_MK // D_MODEL, U_MV // D_MODEL

    def tok(col, rev):
        if rev:
            return pl.BlockSpec((CHUNK, D_MODEL), lambda b, c: (b * nc + nc - 1 - c, col))
        return pl.BlockSpec((CHUNK, D_MODEL), lambda b, c: (b * nc + c, col))

    gfs, gbs = _gate_specs(nc)
    in_specs = ([tok(mqb, False), tok(mkb, False), tok(mvb, False),
                 tok(mqb, True), tok(mkb, True), tok(mvb, True)] + gfs + gbs
                + [_const_spec(a) for a in consts])
    args = [u] * 6 + list(gates) + list(gates) + list(consts)
    cn_shape = (2, ML_HEADS, ML_HEAD_DIM, 2 * ML_HEAD_DIM)
    m_shape = (2, 1, D_MODEL)
    if has_init:
        in_specs += [pl.BlockSpec((1, 1) + cn_shape, lambda b, c: (b, layer, 0, 0, 0, 0)),
                     pl.BlockSpec((1, 1) + m_shape, lambda b, c: (b, layer, 0, 0, 0))]
        args += list(states)
    out_specs = [tok(0, False), tok(0, True)]
    out_shape = [jax.ShapeDtypeStruct((t, D_MODEL), BF16)] * 2
    if not has_init:
        out_specs += [pl.BlockSpec((1,) + cn_shape, lambda b, c: (b, 0, 0, 0, 0)),
                      pl.BlockSpec((1,) + m_shape, lambda b, c: (b, 0, 0, 0))]
        out_shape += [jax.ShapeDtypeStruct((nb,) + cn_shape, F32),
                      jax.ShapeDtypeStruct((nb,) + m_shape, F32)]
    return pl.pallas_call(
        functools.partial(_ml_kernel, has_init=has_init, emit_state=not has_init),
        grid=(nb, nc),
        in_specs=in_specs,
        out_specs=out_specs,
        out_shape=out_shape,
        scratch_shapes=[pltpu.VMEM(cn_shape, F32), pltpu.VMEM(m_shape, F32)],
        compiler_params=_cparams("parallel", "arbitrary"),
        name="mlstm_scan",
    )(*args)


def _qkprep_kernel(*refs, rope, emit_f32):
    uq_ref, ukv_ref, gain_ref = refs[0:3]
    k = 3
    if rope:
        cos_ref, sin_ref = refs[k:k + 2]
        k += 2
    q_ref, klo_ref, khi_ref, v1_ref = refs[k:k + 4]
    k += 4
    if emit_f32:
        kf_ref, vf_ref = refs[k:k + 2]
        k += 2
    nqk = D_MODEL + ATT_KV
    pair = 2 * ATT_HEAD_DIM
    low = lax.broadcasted_iota(jnp.int32, (uq_ref.shape[0], pair), 1) < ATT_HEAD_DIM
    cols = ([uq_ref[:, p * pair:(p + 1) * pair] for p in range(D_MODEL // pair)]
            + [ukv_ref[:, p * pair:(p + 1) * pair] for p in range(ATT_KV // pair)])
    parts = []
    for col in cols:
        x = col.astype(F32)
        x2 = x * x
        ss = jnp.where(low, jnp.sum(jnp.where(low, x2, 0.0), axis=-1, keepdims=True),
                       jnp.sum(jnp.where(low, 0.0, x2), axis=-1, keepdims=True))
        parts.append(x * lax.rsqrt(ss * (1.0 / ATT_HEAD_DIM) + EPS))
    xn = jnp.concatenate(parts, axis=1) * gain_ref[...]
    v = ukv_ref[:, ATT_KV:]
    if emit_f32:
        kf_ref[...] = xn[:, D_MODEL:nqk]
        vf_ref[...] = v.astype(F32)
    if rope:
        lane = lax.broadcasted_iota(jnp.int32, xn.shape, 1)
        quarter = ATT_HEAD_DIM // 4
        first = jnp.bitwise_and(lane, 2 * quarter - 1) < quarter
        swapped = jnp.where(first, pltpu.roll(xn, nqk - quarter, 1), pltpu.roll(xn, quarter, 1))
        xn = xn * cos_ref[...] + swapped * sin_ref[...]
    q_ref[...] = (xn[:, :D_MODEL] * Q_SCALE).astype(BF16)

    def spread(a, fill):
        lo_parts, hi_parts = [], []
        for c in (a[:, :pair], a[:, pair:]):
            r = pltpu.roll(c, ATT_HEAD_DIM, 1)
            lo_parts += [jnp.where(low, c, fill), jnp.where(low, r, fill)]
            hi_parts += [jnp.where(low, fill, r), jnp.where(low, fill, c)]
        return jnp.concatenate(lo_parts, axis=1), jnp.concatenate(hi_parts, axis=1)

    k_lo, k_hi = spread(xn[:, D_MODEL:nqk], 0.0)
    klo_ref[...] = k_lo.astype(BF16)
    khi_ref[...] = k_hi.astype(BF16)
    v1_ref[...] = spread(v.astype(F32), 1.0)[0].astype(BF16)


def _qkprep(u, gain, tables, nb, seq, emit_f32):
    t = nb * seq
    tl = min(seq, 256)
    nl = seq // tl
    nqk = D_MODEL + ATT_KV
    rope = tables is not None
    tok = lambda w: pl.BlockSpec((tl, w), lambda i, b: (b * nl + i, 0))
    in_specs = [pl.BlockSpec((tl, D_MODEL), lambda i, b: (b * nl + i, U_AQ // D_MODEL)),
                pl.BlockSpec((tl, 2 * ATT_KV), lambda i, b: (b * nl + i, U_AKV // (2 * ATT_KV))),
                pl.BlockSpec((1, nqk), lambda i, b: (0, 0))]
    args = [u, u, gain]
    if rope:
        in_specs += [pl.BlockSpec((tl, nqk), lambda i, b: (i, 0))] * 2
        args += list(tables)
    out_specs = [tok(D_MODEL), tok(ATT_KV_PAD), tok(ATT_KV_PAD), tok(ATT_KV_PAD)]
    out_shape = [jax.ShapeDtypeStruct((t, D_MODEL), BF16)] + [jax.ShapeDtypeStruct((t, ATT_KV_PAD), BF16)] * 3
    if emit_f32:
        out_specs += [tok(ATT_KV), tok(ATT_KV)]
        out_shape += [jax.ShapeDtypeStruct((t, ATT_KV), F32)] * 2
    return pl.pallas_call(
        functools.partial(_qkprep_kernel, rope=rope, emit_f32=emit_f32),
        grid=(nl, nb),
        in_specs=in_specs,
        out_specs=out_specs,
        out_shape=out_shape,
        compiler_params=_cparams("parallel", "parallel"),
        name="qk_prep",
    )(*args)


def _attn_kernel(*refs, has_cache, tq):
    q_ref, klo_ref, khi_ref, v1_ref = refs[0:4]
    k = 4
    if has_cache:
        kclo_ref, kchi_ref, vc1_ref = refs[k:k + 3]
        k += 3
    o_ref = refs[k]
    pair = 2 * ATT_HEAD_DIM
    low = lax.broadcasted_iota(jnp.int32, (tq, pair), 1) < ATT_HEAD_DIM
    units = [(kvh, which) for kvh in range(ATT_KV_HEADS) for which in (0, 1)]

    def keys_of(kvh):
        ks = slice(kvh * pair, (kvh + 1) * pair)
        keys = [(klo_ref[:, ks], khi_ref[:, ks], v1_ref[:, ks])]
        if has_cache:
            keys.append((kclo_ref[0, 0, :, ks], kchi_ref[0, 0, :, ks], vc1_ref[0, 0, :, ks]))
        return keys

    def score(u):
        kvh, which = units[u]
        qp = jnp.concatenate([q_ref[:, (2 * kvh) * pair:(2 * kvh + 1) * pair],
                              q_ref[:, (2 * kvh + 1) * pair:(2 * kvh + 2) * pair]], axis=0)
        return [_dot_nt(qp, kk[which]) for kk in keys_of(kvh)]

    def weights(scores):
        m = jnp.max(scores[0], axis=-1, keepdims=True)
        for s in scores[1:]:
            m = jnp.maximum(m, jnp.max(s, axis=-1, keepdims=True))
        return [jnp.exp2(s - m).astype(BF16) for s in scores]

    def values(u, probs):
        pv = None
        for p, kk in zip(probs, keys_of(units[u][0])):
            part = _dot(p, kk[2])
            pv = part if pv is None else pv + part
        return pv

    n = len(units)
    sc = {0: score(0), 1: score(1)}
    pr = {0: weights(sc.pop(0))}
    pv = {}
    for u in range(n):
        if u + 2 < n:
            sc[u + 2] = score(u + 2)
        pv[u] = values(u, pr.pop(u))
        if u + 1 < n:
            pr[u + 1] = weights(sc.pop(u + 1))
        if units[u][1] == 1:
            kvh = units[u][0]
            lo_half, hi_half = pv.pop(u - 1), pv.pop(u)
            first = lo_half / pltpu.roll(lo_half, ATT_HEAD_DIM, 1)
            second = pltpu.roll(hi_half, ATT_HEAD_DIM, 1) / hi_half
            for half in range(2):
                rows = slice(half * tq, (half + 1) * tq)
                o_ref[:, (2 * kvh + half) * pair:(2 * kvh + half + 1) * pair] = (
                    jnp.where(low, first[rows], second[rows]).astype(o_ref.dtype))


def _attn(q, klo, khi, v1, cache, nb, seq, layer):
    t = nb * seq
    tq = 256 if seq > 256 else min(seq, 128)
    nq = seq // tq
    has_cache = cache is not None
    kv_spec = pl.BlockSpec((seq, ATT_KV_PAD), lambda b, i: (b, 0))
    in_specs = [pl.BlockSpec((tq, D_MODEL), lambda b, i: (b * nq + i, 0)), kv_spec, kv_spec, kv_spec]
    args = [q, klo, khi, v1]
    if has_cache:
        past = cache[0].shape[2]
        spec = pl.BlockSpec((1, 1, past, ATT_KV_PAD), lambda b, i: (b, layer, 0, 0))
        in_specs += [spec, spec, spec]
        args += list(cache)
    return pl.pallas_call(
        functools.partial(_attn_kernel, has_cache=has_cache, tq=tq),
        grid=(nb, nq),
        in_specs=in_specs,
        out_specs=pl.BlockSpec((tq, D_MODEL), lambda b, i: (b * nq + i, 0)),
        out_shape=jax.ShapeDtypeStruct((t, D_MODEL), BF16),
        compiler_params=_cparams("parallel", "parallel"),
        name="gqa",
    )(*args)


def _merge_kernel(yf_ref, yb_ref, z_ref, att_ref, hf_ref, hb_ref, mo_ref,
                  g0_ref, g1_ref, g2_ref, x_ref, gate_ref, sgain_ref, mgain_ref,
                  wb_ref, wo_ref, o_ref, b3_scr):
    ys = (yf_ref[...].astype(F32) + yb_ref[...].astype(F32)) * z_ref[...].astype(F32)
    b1 = _rms(ys) * sgain_ref[...]
    for hh in range(ML_HEADS):
        sl = slice(hh * ML_HEAD_DIM, (hh + 1) * ML_HEAD_DIM)
        b3_scr[:, sl] = _rms(hf_ref[:, sl].astype(F32) + hb_ref[:, sl].astype(F32))
    b3 = (b3_scr[...] * mgain_ref[...]) * mo_ref[...].astype(F32)
    merged = (g0_ref[...].astype(F32) * _dot(b1.astype(BF16), wb_ref[0])
              + g1_ref[...].astype(F32) * _dot(att_ref[...], wb_ref[1])
              + g2_ref[...].astype(F32) * _dot(b3.astype(BF16), wb_ref[2]))
    o_ref[...] = x_ref[...] + gate_ref[0] * _dot(merged.astype(BF16), wo_ref[...])


def _merge(yf, yb, u, att, hf, hb, x, gate1, sgain, mgain, wb, wo, seq, layer):
    t = x.shape[0]
    tm = min(t, 256)
    tok = lambda i: (i, 0)
    ucol = lambda off: pl.BlockSpec((tm, D_MODEL), lambda i: (i, off // D_MODEL))
    full = pl.BlockSpec((tm, D_MODEL), tok)
    vec = pl.BlockSpec((1, D_MODEL), lambda i: (0, 0))
    once = pl.Buffered(1)
    in_specs = [full, full, ucol(U_SZ), full, full, full, ucol(U_MO),
                ucol(U_G), ucol(U_G + D_MODEL), ucol(U_G + 2 * D_MODEL), full,
                _mod_spec(gate1.shape[0], seq, tm), vec, vec,
                pl.BlockSpec((None, 3, D_MODEL, D_MODEL), lambda i: (layer, 0, 0, 0), pipeline_mode=once),
                pl.BlockSpec((None, D_MODEL, D_MODEL), lambda i: (layer, 0, 0), pipeline_mode=once)]
    return pl.pallas_call(
        _merge_kernel,
        grid=(t // tm,),
        in_specs=in_specs,
        out_specs=full,
        out_shape=jax.ShapeDtypeStruct((t, D_MODEL), F32),
        scratch_shapes=[pltpu.VMEM((tm, D_MODEL), F32)],
        compiler_params=_cparams("parallel"),
        name="branch_merge",
    )(yf, yb, u, att, hf, hb, u, u, u, u, x, gate1, sgain, mgain, wb, wo)


def _ffn_kernel(x_ref, sc_ref, sh_ref, gate_ref, wi_ref, wo_ref, *rest, last):
    x = x_ref[...]
    h = (_rms(x) * (1.0 + sc_ref[0]) + sh_ref[0]).astype(BF16)
    a = _dot(h, wi_ref[:, :FFN_HIDDEN])
    b = _dot(h, wi_ref[:, FFN_HIDDEN:])
    act = (a * _sigmoid(a) * b).astype(BF16)
    y = x + gate_ref[0] * _dot(act, wo_ref[...])
    if last:
        gain_ref, o_ref = rest
        o_ref[...] = _rms(y) * gain_ref[...]
    else:
        nsc_ref, nsh_ref, o_ref, h_ref = rest
        o_ref[...] = y
        h_ref[...] = (_rms(y) * (1.0 + nsc_ref[0]) + nsh_ref[0]).astype(h_ref.dtype)


def _ffn(x, scale, shift, gate, wi, wo, seq, layer, nxt):
    t = x.shape[0]
    tm = min(t, 256)
    nb = scale.shape[0]
    once = pl.Buffered(1)
    full = pl.BlockSpec((tm, D_MODEL), lambda i: (i, 0))
    last = not isinstance(nxt, tuple)
    mod = _mod_spec(nb, seq, tm)
    in_specs = [full, mod, mod, mod,
                pl.BlockSpec((None, D_MODEL, 2 * FFN_HIDDEN), lambda i: (layer, 0, 0), pipeline_mode=once),
                pl.BlockSpec((None, FFN_HIDDEN, D_MODEL), lambda i: (layer, 0, 0), pipeline_mode=once)]
    if last:
        in_specs.append(pl.BlockSpec((1, D_MODEL), lambda i: (0, 0)))
        extra = [nxt]
        out_specs = full
        out_shape = jax.ShapeDtypeStruct((t, D_MODEL), F32)
    else:
        in_specs += [mod, mod]
        extra = list(nxt)
        out_specs = [full, full]
        out_shape = [jax.ShapeDtypeStruct((t, D_MODEL), F32), jax.ShapeDtypeStruct((t, D_MODEL), BF16)]
    return pl.pallas_call(
        functools.partial(_ffn_kernel, last=last),
        grid=(t // tm,),
        in_specs=in_specs,
        out_specs=out_specs,
        out_shape=out_shape,
        compiler_params=_cparams("parallel"),
        name="ffn",
    )(x, scale, shift, gate, wi, wo, *extra)


def _gate_consts():
    idx = np.arange(GATE_W)
    back = ((idx >= G_DT + SSD_HEADS) & (idx < G_LI)) | ((idx >= G_LF + ML_HEADS) & (idx < G_LF + 2 * ML_HEADS))
    r = np.arange(CHUNK)
    tlo = (r[:, None] >= r[None, :]).astype(np.float32)
    tup = (r[:, None] <= r[None, :]).astype(np.float32)
    return (jnp.asarray(back.astype(np.float32)), jnp.asarray(tlo, BF16), jnp.asarray(tup, BF16))


def _scan_consts():
    def expand(first, heads, width, parts):
        e = np.zeros((2, parts * GATE_W, heads * width), np.float32)
        for d in range(2):
            for h in range(heads):
                for part in range(parts):
                    e[d, part * GATE_W + first + d * heads + h, h * width:(h + 1) * width] = 1.0
        return jnp.asarray(e, BF16)

    ssd = (expand(G_DT, SSD_HEADS, SSD_HEAD_DIM, 1), expand(G_DT, SSD_HEADS, CHUNK, 2),
           jnp.asarray(np.eye(SSD_BC, dtype=np.float32), BF16))
    ml = (expand(G_LF, ML_HEADS, ML_HEAD_DIM, 2), jnp.asarray(np.eye(CHUNK, dtype=np.float32), BF16))
    return ssd, ml


def _ssd_state_to_compact(s):
    lead = s.shape[:-3]
    rep = SSD_HEADS // SSD_GROUPS
    s = s.reshape(lead + (SSD_GROUPS, rep, SSD_HEAD_DIM, SSD_STATE))
    nd = len(lead)
    s = jnp.transpose(s, tuple(range(nd)) + (nd, nd + 3, nd + 1, nd + 2))
    return s.reshape(lead + (SSD_GROUPS, SSD_STATE, SSD_GROUP_W))


def _ssd_state_from_compact(s):
    lead = s.shape[:-3]
    rep = SSD_HEADS // SSD_GROUPS
    s = s.reshape(lead + (SSD_GROUPS, SSD_STATE, rep, SSD_HEAD_DIM))
    nd = len(lead)
    s = jnp.transpose(s, tuple(range(nd)) + (nd, nd + 2, nd + 3, nd + 1))
    return s.reshape(lead + (SSD_HEADS, SSD_HEAD_DIM, SSD_STATE))


def _rope_tables(seq):
    pos = np.arange(seq)
    quarter = ATT_HEAD_DIM // 4
    freqs = jnp.asarray(ROPE_THETA, F32) ** (-jnp.arange(quarter, dtype=F32) / quarter)
    ang_r = jnp.asarray(pos // GRID_W, F32)[:, None] * freqs
    ang_c = jnp.asarray(pos % GRID_W, F32)[:, None] * freqs
    cos = jnp.concatenate([jnp.cos(ang_r)] * 2 + [jnp.cos(ang_c)] * 2, axis=-1)
    sin = jnp.concatenate([-jnp.sin(ang_r), jnp.sin(ang_r), -jnp.sin(ang_c), jnp.sin(ang_c)], axis=-1)
    reps = (D_MODEL + ATT_KV) // ATT_HEAD_DIM
    return jnp.tile(cos, (1, reps)), jnp.tile(sin, (1, reps))


def _split_w_in(w_in):
    sizes = (D_MODEL, D_MODEL, SSD_BC, SSD_BC, 2 * SSD_HEADS,
             D_MODEL, ATT_KV, ATT_KV,
             D_MODEL, D_MODEL, D_MODEL, D_MODEL, 4 * ML_HEADS, 3 * D_MODEL)
    offs = np.cumsum((0,) + sizes)
    return [w_in[:, :, offs[i]:offs[i + 1]] for i in range(len(sizes))]


def kernel(x_prompt, x_sample, cache_k, cache_v, state_ssd, state_ml_c, state_ml_n, state_ml_m,
           c, c_ctx, w_ada, b_ada, w_in, ssd_conv_w, ssd_conv_b, ssd_a_log, ssd_dt_bias, ssd_d,
           ssd_norm, att_q_norm, att_k_norm, ml_conv_w, ml_conv_b, ml_gate_bias, ml_norm,
           w_branch, w_out, w_ffn_in, w_ffn_out, final_norm):
    depth = w_in.shape[0]
    nbp, seqp, _ = x_prompt.shape
    nbs, seqs, _ = x_sample.shape
    past = cache_k.shape[2]

    (s_x, s_z, s_b, s_c, s_dt, a_q, a_k, a_v, m_q, m_k, m_v, m_o, m_g, g) = _split_w_in(w_in)
    w_main = jnp.concatenate([m_q, m_k, s_x, s_b, s_c, a_k, a_v, a_q, m_v, s_z, m_o, g], axis=-1).astype(BF16)
    conv_w = jnp.concatenate([ml_conv_w, ssd_conv_w], axis=-1)
    conv_b = jnp.concatenate([ml_conv_b, ssd_conv_b], axis=-1).reshape(depth, 1, U_CONV_W)
    m_g4 = m_g.reshape(depth, D_MODEL, 2, 2, ML_HEADS)
    w_small = jnp.concatenate(
        [s_dt, m_g4[:, :, :, 0].reshape(depth, D_MODEL, 2 * ML_HEADS),
         m_g4[:, :, :, 1].reshape(depth, D_MODEL, 2 * ML_HEADS),
         jnp.zeros((depth, D_MODEL, GATE_W - G_LF - 2 * ML_HEADS), F32)], axis=-1).astype(BF16)
    w_small_t = jnp.swapaxes(w_small, 1, 2)
    pad = jnp.zeros((depth, GATE_W - G_LF - 2 * ML_HEADS), F32)
    gate_bias = jnp.concatenate(
        [ssd_dt_bias.reshape(depth, 2 * SSD_HEADS), ml_gate_bias[:, :, 0].reshape(depth, 2 * ML_HEADS),
         ml_gate_bias[:, :, 1].reshape(depth, 2 * ML_HEADS), pad], axis=-1)
    a_vec = jnp.concatenate(
        [-jnp.exp(ssd_a_log.reshape(depth, 2 * SSD_HEADS)), jnp.zeros((depth, GATE_W - 2 * SSD_HEADS), F32)], axis=-1)
    d_vec = jnp.repeat(ssd_d, SSD_HEAD_DIM, axis=-1).reshape(depth, 1, D_MODEL)
    qk_gain = jnp.concatenate([jnp.tile(att_q_norm, (1, ATT_HEADS)), jnp.tile(att_k_norm, (1, ATT_KV_HEADS))], axis=-1)
    qk_gain = qk_gain.reshape(depth, 1, D_MODEL + ATT_KV)
    w_branch_b = w_branch.astype(BF16)
    w_out_b = w_out.astype(BF16)
    w_ffn_in_b = w_ffn_in.astype(BF16)
    w_ffn_out_b = w_ffn_out.astype(BF16)
    consts = _gate_consts()
    tables = _rope_tables(seqs)
    zeros = jnp.zeros(cache_k.shape, BF16)
    ck, cv = cache_k.astype(BF16), cache_v.astype(BF16)
    pad_shape = (nbs, depth, past, ATT_KV_PAD)
    cache = (jnp.concatenate([ck, zeros], axis=-1).reshape(pad_shape),
             jnp.concatenate([zeros, ck], axis=-1).reshape(pad_shape),
             jnp.concatenate([cv, jnp.ones(cache_v.shape, BF16)], axis=-1).reshape(pad_shape))
    ssd_consts, ml_consts = _scan_consts()
    ssd_h0 = _ssd_state_to_compact(state_ssd)
    ml_cn0 = jnp.concatenate(
        [state_ml_c, jnp.broadcast_to(state_ml_n[..., None], state_ml_c.shape)], axis=-1)
    ml_m0 = jnp.repeat(state_ml_m, ML_HEAD_DIM, axis=-1).reshape(nbs, depth, 2, 1, D_MODEL)

    rows = 8 * ((1 + nbs + 7) // 8)
    cc = jnp.concatenate([c_ctx[None], c, jnp.zeros((rows - 1 - nbs, D_MODEL), F32)], axis=0)
    mod = _ada(cc, w_ada, b_ada)

    def mods(layer, lo, hi):
        m = mod[layer, lo:hi].reshape(hi - lo, 1, 6, D_MODEL)
        return [m[:, :, i] for i in range(6)]

    def layer_step(x, h, layer, nb, seq, mod6, nxt, ctx):
        shift1, scale1, gate1, shift2, scale2, gate2 = mod6
        u, sx = _proj(h, w_main, conv_w, conv_b, layer, seq)
        gates = _gate(h, w_small[layer], w_small_t[layer], gate_bias[layer], a_vec[layer], consts)
        if ctx:
            yf, yb = _ssd(sx, u, gates, ssd_consts, d_vec[layer], nb, seq, ssd_h0, layer)
            q, klo, khi, v1 = _qkprep(u, qk_gain[layer], tables, nb, seq, False)
            att = _attn(q, klo, khi, v1, cache, nb, seq, layer)
            hf, hb = _mlstm(u, gates, ml_consts, nb, seq, (ml_cn0, ml_m0), layer)
            new = None
        else:
            yf, yb, hst = _ssd(sx, u, gates, ssd_consts, d_vec[layer], nb, seq, None, layer)
            q, klo, khi, v1, kf, vf = _qkprep(u, qk_gain[layer], None, nb, seq, True)
            att = _attn(q, klo, khi, v1, None, nb, seq, layer)
            hf, hb, cn, mrow = _mlstm(u, gates, ml_consts, nb, seq, None, layer)
            new = (kf.reshape(nb, seq, ATT_KV_HEADS, ATT_HEAD_DIM), vf.reshape(nb, seq, ATT_KV_HEADS, ATT_HEAD_DIM),
                   _ssd_state_from_compact(hst), cn[..., :ML_HEAD_DIM], cn[..., ML_HEAD_DIM],
                   mrow[:, :, 0, ::ML_HEAD_DIM])
        x = _merge(yf, yb, u, att, hf, hb, x, gate1,
                   ssd_norm[layer].reshape(1, D_MODEL), ml_norm[layer].reshape(1, D_MODEL),
                   w_branch_b, w_out_b, seq, layer)
        return _ffn(x, scale2, shift2, gate2, w_ffn_in_b, w_ffn_out_b, seq, layer, nxt), new

    def run_group(x, nb, seq, lo, hi, ctx):
        mod6 = [mods(layer, lo, hi) for layer in range(depth)]
        h = _norm_mod(x, mod6[0][1], mod6[0][0], seq)
        news = []
        for layer in range(depth):
            if layer + 1 < depth:
                nxt = (mod6[layer + 1][1], mod6[layer + 1][0])
                (x, h), new = layer_step(x, h, layer, nb, seq, mod6[layer], nxt, ctx)
            else:
                x, new = layer_step(x, h, layer, nb, seq, mod6[layer], final_norm.reshape(1, D_MODEL), ctx)
            news.append(new)
        return x, news

    y_p, news = run_group(x_prompt.reshape(nbp * seqp, D_MODEL), nbp, seqp, 0, 1, False)
    y_s, _ = run_group(x_sample.reshape(nbs * seqs, D_MODEL), nbs, seqs, 1, 1 + nbs, True)
    y_prompt = y_p.reshape(x_prompt.shape)
    y_sample = y_s.reshape(x_sample.shape)
    stacked = [jnp.stack([n[i] for n in news], axis=1) for i in range(6)]
    return (y_prompt, y_sample) + tuple(stacked)
```

```python
import functools

import numpy as np
import jax
import jax.numpy as jnp
from jax import lax
from jax.experimental import pallas as pl
from jax.experimental.pallas import tpu as pltpu

F32 = jnp.float32
BF16 = jnp.bfloat16

D_MODEL = 1024
GRID_W = 64
EPS = 1e-6
CONV_W = 4
CHUNK = 128
SSD_HEADS = 16
SSD_HEAD_DIM = 64
SSD_GROUPS = 4
SSD_STATE = 64
SSD_BC = SSD_GROUPS * SSD_STATE
SSD_CONV = D_MODEL + 2 * SSD_BC
ATT_HEADS = 16
ATT_HEAD_DIM = 64
ATT_KV_HEADS = 4
ATT_KV = ATT_KV_HEADS * ATT_HEAD_DIM
ATT_QKV = D_MODEL + 2 * ATT_KV
ATT_KV_PAD = ATT_KV_HEADS * 2 * ATT_HEAD_DIM
Q_SCALE = ATT_HEAD_DIM ** -0.5 * 1.4426950408889634
ROPE_THETA = 10000.0
ML_HEADS = 8
ML_HEAD_DIM = 128
FFN_HIDDEN = 2816
GATE_W = 128

U_TILE = 512
U_MQ = 0
U_MK = 1024
U_SX = 2048
U_SBC = 3072
U_CONV_W = 3584
U_AKV = 3584
U_AQ = 4096
U_MV = 5120
U_SZ = 6144
U_MO = 7168
U_G = 8192
U_W = 11264

G_DT = 0
G_LI = 32
G_LF = 48

VMEM_LIMIT = 48 * 1024 * 1024
NT_DIMS = (((1,), (1,)), ((), ()))
TN_DIMS = (((0,), (0,)), ((), ()))


def _cparams(*sem):
    return pltpu.CompilerParams(dimension_semantics=sem, vmem_limit_bytes=VMEM_LIMIT)


def _sigmoid(x):
    return 1.0 / (1.0 + jnp.exp(-x))


def _softplus(x):
    return jnp.maximum(x, 0.0) + jnp.log1p(jnp.exp(-jnp.abs(x)))


def _dot(a, b):
    return jnp.dot(a, b, preferred_element_type=F32)


def _dot_nt(a, b):
    return lax.dot_general(a, b, NT_DIMS, preferred_element_type=F32)


def _dot_tn(a, b):
    return lax.dot_general(a, b, TN_DIMS, preferred_element_type=F32)


def _ada_kernel(c_ref, w_ref, b_ref, o_ref):
    c = c_ref[...]
    s = (c * _sigmoid(c)).astype(BF16)
    o_ref[0] = _dot(s, w_ref[0].astype(BF16)) + b_ref[0]


def _ada(cc, w_ada, b_ada):
    depth, _, n6 = w_ada.shape
    rows = cc.shape[0]
    tn = 512
    return pl.pallas_call(
        _ada_kernel,
        grid=(depth, n6 // tn),
        in_specs=[pl.BlockSpec((rows, D_MODEL), lambda l, j: (0, 0)),
                  pl.BlockSpec((1, D_MODEL, tn), lambda l, j: (l, 0, j)),
                  pl.BlockSpec((1, 1, tn), lambda l, j: (l, 0, j))],
        out_specs=pl.BlockSpec((1, rows, tn), lambda l, j: (l, 0, j)),
        out_shape=jax.ShapeDtypeStruct((depth, rows, n6), F32),
        compiler_params=_cparams("parallel", "parallel"),
        name="ada_mod",
    )(cc, w_ada, b_ada.reshape(depth, 1, n6))


def _rms(x):
    return x * lax.rsqrt(jnp.mean(x * x, axis=-1, keepdims=True) + EPS)


def _norm_mod_kernel(x_ref, sc_ref, sh_ref, o_ref):
    o_ref[...] = (_rms(x_ref[...]) * (1.0 + sc_ref[0]) + sh_ref[0]).astype(o_ref.dtype)


def _mod_spec(nb, seq, tm):
    if nb == 1:
        return pl.BlockSpec((1, 1, D_MODEL), lambda i: (0, 0, 0))
    return pl.BlockSpec((1, 1, D_MODEL), lambda i: ((i * tm) // seq, 0, 0))


def _norm_mod(x, scale, shift, seq):
    t = x.shape[0]
    tm = min(t, 512)
    nb = scale.shape[0]
    return pl.pallas_call(
        _norm_mod_kernel,
        grid=(t // tm,),
        in_specs=[pl.BlockSpec((tm, D_MODEL), lambda i: (i, 0)),
                  _mod_spec(nb, seq, tm), _mod_spec(nb, seq, tm)],
        out_specs=pl.BlockSpec((tm, D_MODEL), lambda i: (i, 0)),
        out_shape=jax.ShapeDtypeStruct((t, D_MODEL), BF16),
        compiler_params=_cparams("parallel"),
        name="norm_mod",
    )(x, scale, shift)


def _sigmoid_tanh(x):
    return 0.5 * jnp.tanh(0.5 * x) + 0.5


def _silu_tanh(x):
    h = 0.5 * x
    return h * jnp.tanh(h) + h


CONV_HALO = 16


def _conv_silu(x, w_ref, b_ref, seq, row0):
    rows = x.shape[0]
    pos = jnp.bitwise_and(lax.broadcasted_iota(jnp.int32, x.shape, 0) + row0, seq - 1)
    acc = x * w_ref[1:2, :] + b_ref[...]
    acc = acc + jnp.where(pos >= 1, pltpu.roll(x, 1, 0), 0.0) * w_ref[0:1, :]
    acc = acc + jnp.where(pos < seq - 1, pltpu.roll(x, rows - 1, 0), 0.0) * w_ref[2:3, :]
    acc = acc + jnp.where(pos < seq - 2, pltpu.roll(x, rows - 2, 0), 0.0) * w_ref[3:4, :]
    return _silu_tanh(acc)


def _proj_kernel(a_ref, w_ref, cw_ref, cb_ref, o_ref, sx_ref, *, conv_hi, sx_lo, sx_hi, silu_lo, sig_lo, sub, seq):
    j = pl.program_id(1)
    w = w_ref[...]

    def run(act):
        for r in range(a_ref.shape[0] // sub):
            rows = slice(r * sub, (r + 1) * sub)
            o_ref[rows, :] = act(_dot(a_ref[rows, :], w)).astype(o_ref.dtype)

    def run_conv(emit_f32):
        tm = a_ref.shape[0]
        for r in range(tm // sub):
            r0 = r * sub
            lo, hi = max(r0 - CONV_HALO, 0), min(r0 + sub + CONV_HALO, tm)
            y = _conv_silu(_dot(a_ref[lo:hi, :], w), cw_ref, cb_ref, seq, lo)[r0 - lo:r0 - lo + sub]
            o_ref[r0:r0 + sub, :] = y.astype(o_ref.dtype)
            if emit_f32:
                sx_ref[r0:r0 + sub, :] = y

    is_sx = jnp.logical_and(j >= sx_lo, j < sx_hi)

    @pl.when(jnp.logical_and(j < conv_hi, jnp.logical_not(is_sx)))
    def _():
        run_conv(False)

    @pl.when(is_sx)
    def _():
        run_conv(True)

    @pl.when(jnp.logical_and(j >= conv_hi, j < silu_lo))
    def _():
        run(lambda acc: acc)

    @pl.when(jnp.logical_and(j >= silu_lo, j < sig_lo))
    def _():
        run(_silu_tanh)

    @pl.when(j >= sig_lo)
    def _():
        run(_sigmoid_tanh)


def _proj(h, w_main, conv_w, conv_b, layer, seq):
    t = h.shape[0]
    tm = min(t, 2048)
    tn = U_TILE
    assert tm % seq == 0 and seq & (seq - 1) == 0
    conv_hi = U_CONV_W // tn
    sx_lo, sx_hi = U_SX // tn, (U_SX + D_MODEL) // tn
    kern = functools.partial(_proj_kernel, conv_hi=conv_hi, sx_lo=sx_lo, sx_hi=sx_hi,
                             silu_lo=U_SZ // tn, sig_lo=U_MO // tn, sub=min(tm, 512), seq=seq)
    conv_col = lambda i, j: (layer, 0, jnp.minimum(j, conv_hi - 1))
    return pl.pallas_call(
        kern,
        grid=(t // tm, U_W // tn),
        in_specs=[pl.BlockSpec((tm, D_MODEL), lambda i, j: (i, 0)),
                  pl.BlockSpec((None, D_MODEL, tn), lambda i, j: (layer, 0, j)),
                  pl.BlockSpec((None, CONV_W, tn), conv_col),
                  pl.BlockSpec((None, 1, tn), conv_col)],
        out_specs=[pl.BlockSpec((tm, tn), lambda i, j: (i, j)),
                   pl.BlockSpec((tm, tn), lambda i, j: (i, jnp.clip(j - sx_lo, 0, sx_hi - sx_lo - 1)))],
        out_shape=[jax.ShapeDtypeStruct((t, U_W), BF16), jax.ShapeDtypeStruct((t, D_MODEL), F32)],
        compiler_params=_cparams("parallel", "arbitrary"),
        name="in_proj",
    )(h, w_main, conv_w, conv_b)


def _split3(x):
    hi = x.astype(BF16)
    r1 = x - hi.astype(F32)
    mid = r1.astype(BF16)
    lo = (r1 - mid.astype(F32)).astype(BF16)
    return hi, mid, lo


def _tri_left(tri, x):
    hi, mid, lo = _split3(x)
    return _dot(tri, hi) + _dot(tri, mid) + _dot(tri, lo)


def _tri_right(x, tri):
    hi, mid, lo = _split3(x)
    return _dot(hi, tri) + _dot(mid, tri) + _dot(lo, tri)


def _gate_vals(u, idx, a):
    sp = _softplus(u)
    ls = -_softplus(-u)
    val = jnp.where(idx < G_LI, sp, jnp.where(idx < G_LF, u, jnp.where(idx < G_LF + 16, ls, 0.0)))
    cin = jnp.where(idx < G_LI, sp * a, jnp.where(jnp.logical_and(idx >= G_LF, idx < G_LF + 16), ls, 0.0))
    return val, cin


def _gate_kernel(h_ref, ws_ref, wst_ref, b_ref, bt_ref, a_ref, at_ref, dm_ref, dmt_ref,
                 tlo_ref, tup_ref, val_ref, cum_ref, valt_ref, cumt_ref):
    h = h_ref[...]
    tlo = tlo_ref[...]
    tup = tup_ref[...]
    rows = h.shape[0]
    u = _dot(h, ws_ref[...]) + b_ref[...]
    val, cin = _gate_vals(u, lax.broadcasted_iota(jnp.int32, (rows, GATE_W), 1), a_ref[...])
    val_ref[...] = val
    ut = _dot_nt(wst_ref[...], h) + bt_ref[...]
    valt, cint = _gate_vals(ut, lax.broadcasted_iota(jnp.int32, (GATE_W, rows), 0), at_ref[...])
    valt_ref[...] = valt
    back = dm_ref[...] > 0.5
    back_t = dmt_ref[...] > 0.5
    for c in range(rows // CHUNK):
        sl = slice(c * CHUNK, (c + 1) * CHUNK)
        cum_ref[sl, :] = jnp.where(back, _tri_left(tup, cin[sl]), _tri_left(tlo, cin[sl]))
        cumt_ref[:, sl] = jnp.where(back_t, _tri_right(cint[:, sl], tlo), _tri_right(cint[:, sl], tup))


def _gate(h, ws, wst, bias, avec, consts):
    t = h.shape[0]
    tm = min(t, 8 * CHUNK)
    dmask, tlo, tup = consts
    row = lambda i: (i, 0)
    col = lambda i: (0, i)
    fix = lambda i: (0, 0)
    tm_out = jax.ShapeDtypeStruct((t, GATE_W), F32)
    fm_out = jax.ShapeDtypeStruct((GATE_W, t), F32)
    return pl.pallas_call(
        _gate_kernel,
        grid=(t // tm,),
        in_specs=[pl.BlockSpec((tm, D_MODEL), row),
                  pl.BlockSpec((D_MODEL, GATE_W), fix),
                  pl.BlockSpec((GATE_W, D_MODEL), fix),
                  pl.BlockSpec((1, GATE_W), fix), pl.BlockSpec((GATE_W, 1), fix),
                  pl.BlockSpec((1, GATE_W), fix), pl.BlockSpec((GATE_W, 1), fix),
                  pl.BlockSpec((1, GATE_W), fix), pl.BlockSpec((GATE_W, 1), fix),
                  pl.BlockSpec((CHUNK, CHUNK), fix), pl.BlockSpec((CHUNK, CHUNK), fix)],
        out_specs=[pl.BlockSpec((tm, GATE_W), row), pl.BlockSpec((tm, GATE_W), row),
                   pl.BlockSpec((GATE_W, tm), col), pl.BlockSpec((GATE_W, tm), col)],
        out_shape=[tm_out, tm_out, fm_out, fm_out],
        compiler_params=_cparams("parallel"),
        name="gate_prep",
    )(h, ws, wst, bias.reshape(1, GATE_W), bias.reshape(GATE_W, 1),
      avec.reshape(1, GATE_W), avec.reshape(GATE_W, 1),
      dmask.reshape(1, GATE_W), dmask.reshape(GATE_W, 1), tlo, tup)


def _tri_mask(fwd):
    row = lax.broadcasted_iota(jnp.int32, (CHUNK, CHUNK), 0)
    col = lax.broadcasted_iota(jnp.int32, (CHUNK, CHUNK), 1)
    return row >= col if fwd else row <= col


def _lane_expand(x, e):
    hi, mid, _ = _split3(x)
    return _dot(jnp.concatenate([hi, mid], axis=1), e)


SSD_GROUP_W = (SSD_HEADS // SSD_GROUPS) * SSD_HEAD_DIM


def _ssd_dir(d, x_ref, bc_ref, val_ref, cum_ref, valt_ref, cumt_ref, e64_ref, e128_ref, eye_ref, skip,
             y_ref, st_scr):
    mask = _tri_mask(d == 0)
    end = CHUNK - 1 if d == 0 else 0
    x = x_ref[...]
    bb = bc_ref[:, :SSD_BC]
    cm = bc_ref[:, SSD_BC:]
    lane_c = lax.broadcasted_iota(jnp.int32, cm.shape, 1)
    lane_x = lax.broadcasted_iota(jnp.int32, (CHUNK, 2 * SSD_HEAD_DIM), 1)
    a128 = _lane_expand(cum_ref[...], e128_ref[d])
    a64 = jnp.concatenate(
        [jnp.where(lane_x < SSD_HEAD_DIM, a128[:, (2 * p) * CHUNK:(2 * p + 1) * CHUNK],
                   a128[:, (2 * p + 1) * CHUNK:(2 * p + 2) * CHUNK]) for p in range(SSD_HEADS // 2)], axis=1)
    dt64 = _dot(val_ref[...].astype(BF16), e64_ref[d])
    dt_r = valt_ref[...]
    ac_r = cumt_ref[...]
    st = st_scr[d]
    inter = _dot(cm, st.astype(BF16))
    ea = jnp.exp(a64)
    cm32 = cm.astype(F32)
    c_stack = jnp.concatenate(
        [jnp.where(jnp.logical_and(lane_c >= g * SSD_STATE, lane_c < (g + 1) * SSD_STATE), cm32, 0.0).astype(BF16)
         for g in range(SSD_GROUPS)], axis=0)
    cbt_all = _dot_nt(c_stack, bb)
    bt = _dot_nt(eye_ref[...], bb).astype(BF16)
    yield
    tot = a64[end:end + 1, :]
    xw = (x * (jnp.exp(tot - a64) * dt64)).astype(BF16)
    decay = jnp.exp(tot)
    for g in range(SSD_GROUPS):
        rs = slice(g * SSD_STATE, (g + 1) * SSD_STATE)
        cs = slice(g * SSD_GROUP_W, (g + 1) * SSD_GROUP_W)
        st_scr[d, rs, cs] = decay[:, cs] * st[rs, cs] + _dot(bt[rs, :], xw[:, cs])
    yield
    n_pairs = SSD_HEADS // 2
    weights, xbds = [], []
    for p in range(n_pairs):
        cbt = cbt_all[(p // 2) * CHUNK:(p // 2 + 1) * CHUNK]
        parts = []
        for hh in (2 * p, 2 * p + 1):
            lane = G_DT + d * SSD_HEADS + hh
            seg = jnp.where(mask, a128[:, hh * CHUNK:(hh + 1) * CHUNK] - ac_r[lane:lane + 1, :], -jnp.inf)
            parts.append((cbt * jnp.exp(seg) * dt_r[lane:lane + 1, :]).astype(BF16))
        weights.append(jnp.concatenate(parts, axis=1))
        xp = x[:, p * 2 * SSD_HEAD_DIM:(p + 1) * 2 * SSD_HEAD_DIM]
        xbds.append(jnp.concatenate([jnp.where(lane_x < SSD_HEAD_DIM, xp, 0.0),
                                     jnp.where(lane_x >= SSD_HEAD_DIM, xp, 0.0)], axis=0).astype(BF16))
    yield
    for p in range(n_pairs):
        cs = slice(p * 2 * SSD_HEAD_DIM, (p + 1) * 2 * SSD_HEAD_DIM)
        y = _dot(weights[p], xbds[p]) + ea[:, cs] * inter[:, cs]
        if skip is not None:
            y = y + skip[:, cs] * x[:, cs]
        y_ref[:, cs] = y.astype(y_ref.dtype)


def _ssd_kernel(*refs, has_h0, emit_state):
    xf_ref, bcf_ref, xb_ref, bcb_ref = refs[0:4]
    gf = refs[4:8]
    gb = refs[8:12]
    consts = refs[12:15]
    dvec_ref = refs[15]
    k = 16
    if has_h0:
        h0_ref = refs[k]
        k += 1
    yf_ref, yb_ref = refs[k:k + 2]
    k += 2
    if emit_state:
        hout_ref = refs[k]
        k += 1
    st_scr = refs[k]
    c = pl.program_id(1)

    @pl.when(c == 0)
    def _():
        st_scr[...] = jnp.zeros(st_scr.shape, F32)
        if has_h0:
            for d in range(2):
                for g in range(SSD_GROUPS):
                    st_scr[d, g * SSD_STATE:(g + 1) * SSD_STATE,
                           g * SSD_GROUP_W:(g + 1) * SSD_GROUP_W] = h0_ref[0, 0, d, g]

    _interleave(_ssd_dir(0, xf_ref, bcf_ref, *gf, *consts, dvec_ref[...], yf_ref, st_scr),
                _ssd_dir(1, xb_ref, bcb_ref, *gb, *consts, None, yb_ref, st_scr))

    if emit_state:
        @pl.when(c == pl.num_programs(1) - 1)
        def _():
            for d in range(2):
                for g in range(SSD_GROUPS):
                    hout_ref[0, d, g] = st_scr[d, g * SSD_STATE:(g + 1) * SSD_STATE,
                                               g * SSD_GROUP_W:(g + 1) * SSD_GROUP_W]


def _chunk_specs(nc, width):
    fwd = pl.BlockSpec((CHUNK, width), lambda b, c: (b * nc + c, 0))
    bwd = pl.BlockSpec((CHUNK, width), lambda b, c: (b * nc + nc - 1 - c, 0))
    return fwd, bwd


def _gate_specs(nc):
    tf, tb = _chunk_specs(nc, GATE_W)
    ff = pl.BlockSpec((GATE_W, CHUNK), lambda b, c: (0, b * nc + c))
    fb = pl.BlockSpec((GATE_W, CHUNK), lambda b, c: (0, b * nc + nc - 1 - c))
    return [tf, tf, ff, ff], [tb, tb, fb, fb]


def _const_spec(arr):
    nd = arr.ndim
    return pl.BlockSpec(arr.shape, lambda b, c: (0,) * nd)


def _ssd(x, u, gates, consts, dvec, nb, seq, state, layer):
    nc = seq // CHUNK
    t = nb * seq
    has_h0 = state is not None
    xf, xb = _chunk_specs(nc, D_MODEL)
    bcol = U_SBC // (2 * SSD_BC)
    bcf = pl.BlockSpec((CHUNK, 2 * SSD_BC), lambda b, c: (b * nc + c, bcol))
    bcb = pl.BlockSpec((CHUNK, 2 * SSD_BC), lambda b, c: (b * nc + nc - 1 - c, bcol))
    gfs, gbs = _gate_specs(nc)
    yf, yb = _chunk_specs(nc, D_MODEL)
    st_shape = (2, SSD_GROUPS, SSD_STATE, SSD_GROUP_W)
    in_specs = [xf, bcf, xb, bcb] + gfs + gbs + [_const_spec(a) for a in consts] + [_const_spec(dvec)]
    args = [x, u, x, u] + list(gates) + list(gates) + list(consts) + [dvec]
    if has_h0:
        in_specs.append(pl.BlockSpec((1, 1) + st_shape, lambda b, c: (b, layer, 0, 0, 0, 0)))
        args.append(state)
    out_specs = [yf, yb]
    out_shape = [jax.ShapeDtypeStruct((t, D_MODEL), BF16)] * 2
    if not has_h0:
        out_specs.append(pl.BlockSpec((1,) + st_shape, lambda b, c: (b, 0, 0, 0, 0)))
        out_shape.append(jax.ShapeDtypeStruct((nb,) + st_shape, F32))
    return pl.pallas_call(
        functools.partial(_ssd_kernel, has_h0=has_h0, emit_state=not has_h0),
        grid=(nb, nc),
        in_specs=in_specs,
        out_specs=out_specs,
        out_shape=out_shape,
        scratch_shapes=[pltpu.VMEM((2, SSD_BC, D_MODEL), F32)],
        compiler_params=_cparams("parallel", "arbitrary"),
        name="ssd_scan",
    )(*args)


def _ml_dir(d, q_ref, k_ref, v_ref, val_ref, cum_ref, valt_ref, cumt_ref, eb_ref, eye_ref,
            y_ref, cn_scr, m_scr):
    mask = _tri_mask(d == 0)
    end = CHUNK - 1 if d == 0 else 0
    heads = range(ML_HEADS)
    sls = [slice(hh * ML_HEAD_DIM, (hh + 1) * ML_HEAD_DIM) for hh in heads]
    kscale = ML_HEAD_DIM ** -0.5
    b128 = _lane_expand(cum_ref[...], eb_ref[d])
    li_r = valt_ref[...]
    bc_r = cumt_ref[...]
    m_row = m_scr[d]
    q = q_ref[...]
    k = k_ref[...]
    v = v_ref[...]
    ones = jnp.ones((CHUNK, ML_HEAD_DIM), BF16)
    eye = eye_ref[...]
    v1 = [jnp.concatenate([v[:, sl], ones], axis=1) for sl in sls]
    cn = [cn_scr[d, hh] for hh in heads]
    kt = [_dot_nt(eye, k[:, sl]) for sl in sls]
    qk = [_dot_nt(q[:, sl], (k[:, sl].astype(F32) * kscale).astype(BF16)) for sl in sls]
    qc = [_dot(q[:, sl], cn[hh].astype(BF16)) for hh, sl in zip(heads, sls)]
    yield
    b_end = b128[end:end + 1, :]
    bm = b128 + m_row
    kwt, dch, m_parts, s, inter, m_t = [], [], [], [], [], []
    for hh, sl in zip(heads, sls):
        b_j = bc_r[G_LF + d * ML_HEADS + hh:G_LF + d * ML_HEADS + hh + 1, :]
        i_j = li_r[G_LI + d * ML_HEADS + hh:G_LI + d * ML_HEADS + hh + 1, :]
        wj = b_end[:, sl] - b_j + i_j
        m_new = jnp.maximum(b_end[:, sl] + m_row[:, sl], jnp.max(wj, axis=1, keepdims=True))
        kwt.append((kt[hh] * (jnp.exp(wj - m_new) * kscale)).astype(BF16))
        dch.append(jnp.exp(b_end[:, sl] + m_row[:, sl] - m_new))
        m_parts.append(m_new)
        dmat = jnp.where(mask, b128[:, sl] - b_j + i_j, -jnp.inf)
        mt = jnp.maximum(bm[:, sl], jnp.max(dmat, axis=1, keepdims=True))
        s.append((qk[hh] * jnp.exp(dmat - mt)).astype(BF16))
        inter.append(jnp.exp(bm[:, sl] - mt))
        m_t.append(mt)
    yield
    upd = [_dot(kwt[hh], v1[hh]) for hh in heads]
    sv = [_dot(s[hh], v1[hh]) for hh in heads]
    yield
    for hh, sl in zip(heads, sls):
        r = sv[hh] + jnp.concatenate([inter[hh], inter[hh]], axis=1) * qc[hh]
        y_ref[:, sl] = (r[:, :ML_HEAD_DIM]
                        / jnp.maximum(jnp.abs(r[:, ML_HEAD_DIM:]), jnp.exp(-m_t[hh]))).astype(y_ref.dtype)
        cn_scr[d, hh] = jnp.concatenate([dch[hh], dch[hh]], axis=1) * cn[hh] + upd[hh]
    m_scr[d] = jnp.concatenate(m_parts, axis=1)


def _interleave(*gens):
    live = list(gens)
    while live:
        still = []
        for g in live:
            try:
                next(g)
                still.append(g)
            except StopIteration:
                pass
        live = still


def _ml_kernel(*refs, has_init, emit_state):
    qf_ref, kf_ref, vf_ref, qb_ref, kb_ref, vb_ref = refs[0:6]
    gf = refs[6:10]
    gb = refs[10:14]
    consts = refs[14:16]
    k = 16
    if has_init:
        cn0_ref, m0_ref = refs[k:k + 2]
        k += 2
    yf_ref, yb_ref = refs[k:k + 2]
    k += 2
    if emit_state:
        cnout_ref, mout_ref = refs[k:k + 2]
        k += 2
    cn_scr, m_scr = refs[k:k + 2]
    c = pl.program_id(1)

    @pl.when(c == 0)
    def _():
        if has_init:
            cn_scr[...] = cn0_ref[0, 0]
            m_scr[...] = m0_ref[0, 0]
        else:
            cn_scr[...] = jnp.zeros(cn_scr.shape, F32)
            m_scr[...] = jnp.zeros(m_scr.shape, F32)

    _interleave(_ml_dir(0, qf_ref, kf_ref, vf_ref, *gf, *consts, yf_ref, cn_scr, m_scr),
                _ml_dir(1, qb_ref, kb_ref, vb_ref, *gb, *consts, yb_ref, cn_scr, m_scr))

    if emit_state:
        @pl.when(c == pl.num_programs(1) - 1)
        def _():
            cnout_ref[0] = cn_scr[...]
            mout_ref[0] = m_scr[...]


def _mlstm(u, gates, consts, nb, seq, states, layer):
    nc = seq // CHUNK
    t = nb * seq
    has_init = states is not None
    mqb, mkb, mvb = U_MQ // D_MODEL, U_MK // D_MODEL, U_MV // D_MODEL

    def tok(col, rev):
        if rev:
            return pl.BlockSpec((CHUNK, D_MODEL), lambda b, c: (b * nc + nc - 1 - c, col))
        return pl.BlockSpec((CHUNK, D_MODEL), lambda b, c: (b * nc + c, col))

    gfs, gbs = _gate_specs(nc)
    in_specs = ([tok(mqb, False), tok(mkb, False), tok(mvb, False),
                 tok(mqb, True), tok(mkb, True), tok(mvb, True)] + gfs + gbs
                + [_const_spec(a) for a in consts])
    args = [u] * 6 + list(gates) + list(gates) + list(consts)
    cn_shape = (2, ML_HEADS, ML_HEAD_DIM, 2 * ML_HEAD_DIM)
    m_shape = (2, 1, D_MODEL)
    if has_init:
        in_specs += [pl.BlockSpec((1, 1) + cn_shape, lambda b, c: (b, layer, 0, 0, 0, 0)),
                     pl.BlockSpec((1, 1) + m_shape, lambda b, c: (b, layer, 0, 0, 0))]
        args += list(states)
    out_specs = [tok(0, False), tok(0, True)]
    out_shape = [jax.ShapeDtypeStruct((t, D_MODEL), BF16)] * 2
    if not has_init:
        out_specs += [pl.BlockSpec((1,) + cn_shape, lambda b, c: (b, 0, 0, 0, 0)),
                      pl.BlockSpec((1,) + m_shape, lambda b, c: (b, 0, 0, 0))]
        out_shape += [jax.ShapeDtypeStruct((nb,) + cn_shape, F32),
                      jax.ShapeDtypeStruct((nb,) + m_shape, F32)]
    return pl.pallas_call(
        functools.partial(_ml_kernel, has_init=has_init, emit_state=not has_init),
        grid=(nb, nc),
        in_specs=in_specs,
        out_specs=out_specs,
        out_shape=out_shape,
        scratch_shapes=[pltpu.VMEM(cn_shape, F32), pltpu.VMEM(m_shape, F32)],
        compiler_params=_cparams("parallel", "arbitrary"),
        name="mlstm_scan",
    )(*args)


def _qkprep_kernel(*refs, rope, emit_f32):
    uq_ref, ukv_ref, gain_ref = refs[0:3]
    k = 3
    if rope:
        cos_ref, sin_ref = refs[k:k + 2]
        k += 2
    q_ref, klo_ref, khi_ref, v1_ref = refs[k:k + 4]
    k += 4
    if emit_f32:
        kf_ref, vf_ref = refs[k:k + 2]
        k += 2
    nqk = D_MODEL + ATT_KV
    pair = 2 * ATT_HEAD_DIM
    low = lax.broadcasted_iota(jnp.int32, (uq_ref.shape[0], pair), 1) < ATT_HEAD_DIM
    cols = ([uq_ref[:, p * pair:(p + 1) * pair] for p in range(D_MODEL // pair)]
            + [ukv_ref[:, p * pair:(p + 1) * pair] for p in range(ATT_KV // pair)])
    parts = []
    for col in cols:
        x = col.astype(F32)
        x2 = x * x
        ss = jnp.where(low, jnp.sum(jnp.where(low, x2, 0.0), axis=-1, keepdims=True),
                       jnp.sum(jnp.where(low, 0.0, x2), axis=-1, keepdims=True))
        parts.append(x * lax.rsqrt(ss * (1.0 / ATT_HEAD_DIM) + EPS))
    xn = jnp.concatenate(parts, axis=1) * gain_ref[...]
    v = ukv_ref[:, ATT_KV:]
    if emit_f32:
        kf_ref[...] = xn[:, D_MODEL:nqk]
        vf_ref[...] = v.astype(F32)
    if rope:
        lane = lax.broadcasted_iota(jnp.int32, xn.shape, 1)
        quarter = ATT_HEAD_DIM // 4
        first = jnp.bitwise_and(lane, 2 * quarter - 1) < quarter
        swapped = jnp.where(first, pltpu.roll(xn, nqk - quarter, 1), pltpu.roll(xn, quarter, 1))
        xn = xn * cos_ref[...] + swapped * sin_ref[...]
    q_ref[...] = (xn[:, :D_MODEL] * Q_SCALE).astype(BF16)

    def spread(a, fill):
        lo_parts, hi_parts = [], []
        for c in (a[:, :pair], a[:, pair:]):
            r = pltpu.roll(c, ATT_HEAD_DIM, 1)
            lo_parts += [jnp.where(low, c, fill), jnp.where(low, r, fill)]
            hi_parts += [jnp.where(low, fill, r), jnp.where(low, fill, c)]
        return jnp.concatenate(lo_parts, axis=1), jnp.concatenate(hi_parts, axis=1)

    k_lo, k_hi = spread(xn[:, D_MODEL:nqk], 0.0)
    klo_ref[...] = k_lo.astype(BF16)
    khi_ref[...] = k_hi.astype(BF16)
    v1_ref[...] = spread(v.astype(F32), 1.0)[0].astype(BF16)


def _qkprep(u, gain, tables, nb, seq, emit_f32):
    t = nb * seq
    tl = min(seq, 256)
    nl = seq // tl
    nqk = D_MODEL + ATT_KV
    rope = tables is not None
    tok = lambda w: pl.BlockSpec((tl, w), lambda i, b: (b * nl + i, 0))
    in_specs = [pl.BlockSpec((tl, D_MODEL), lambda i, b: (b * nl + i, U_AQ // D_MODEL)),
                pl.BlockSpec((tl, 2 * ATT_KV), lambda i, b: (b * nl + i, U_AKV // (2 * ATT_KV))),
                pl.BlockSpec((1, nqk), lambda i, b: (0, 0))]
    args = [u, u, gain]
    if rope:
        in_specs += [pl.BlockSpec((tl, nqk), lambda i, b: (i, 0))] * 2
        args += list(tables)
    out_specs = [tok(D_MODEL), tok(ATT_KV_PAD), tok(ATT_KV_PAD), tok(ATT_KV_PAD)]
    out_shape = [jax.ShapeDtypeStruct((t, D_MODEL), BF16)] + [jax.ShapeDtypeStruct((t, ATT_KV_PAD), BF16)] * 3
    if emit_f32:
        out_specs += [tok(ATT_KV), tok(ATT_KV)]
        out_shape += [jax.ShapeDtypeStruct((t, ATT_KV), F32)] * 2
    return pl.pallas_call(
        functools.partial(_qkprep_kernel, rope=rope, emit_f32=emit_f32),
        grid=(nl, nb),
        in_specs=in_specs,
        out_specs=out_specs,
        out_shape=out_shape,
        compiler_params=_cparams("parallel", "parallel"),
        name="qk_prep",
    )(*args)


def _attn_kernel(*refs, has_cache, tq):
    q_ref, klo_ref, khi_ref, v1_ref = refs[0:4]
    k = 4
    if has_cache:
        kclo_ref, kchi_ref, vc1_ref = refs[k:k + 3]
        k += 3
    o_ref = refs[k]
    pair = 2 * ATT_HEAD_DIM
    low = lax.broadcasted_iota(jnp.int32, (tq, pair), 1) < ATT_HEAD_DIM
    units = [(kvh, which) for kvh in range(ATT_KV_HEADS) for which in (0, 1)]

    def keys_of(kvh):
        ks = slice(kvh * pair, (kvh + 1) * pair)
        keys = [(klo_ref[:, ks], khi_ref[:, ks], v1_ref[:, ks])]
        if has_cache:
            keys.append((kclo_ref[0, 0, :, ks], kchi_ref[0, 0, :, ks], vc1_ref[0, 0, :, ks]))
        return keys

    def score(u):
        kvh, which = units[u]
        qp = jnp.concatenate([q_ref[:, (2 * kvh) * pair:(2 * kvh + 1) * pair],
                              q_ref[:, (2 * kvh + 1) * pair:(2 * kvh + 2) * pair]], axis=0)
        return [_dot_nt(qp, kk[which]) for kk in keys_of(kvh)]

    def weights(scores):
        m = jnp.max(scores[0], axis=-1, keepdims=True)
        for s in scores[1:]:
            m = jnp.maximum(m, jnp.max(s, axis=-1, keepdims=True))
        return [jnp.exp2(s - m).astype(BF16) for s in scores]

    def values(u, probs):
        pv = None
        for p, kk in zip(probs, keys_of(units[u][0])):
            part = _dot(p, kk[2])
            pv = part if pv is None else pv + part
        return pv

    n = len(units)
    sc = {0: score(0), 1: score(1)}
    pr = {0: weights(sc.pop(0))}
    pv = {}
    for u in range(n):
        if u + 2 < n:
            sc[u + 2] = score(u + 2)
        pv[u] = values(u, pr.pop(u))
        if u + 1 < n:
            pr[u + 1] = weights(sc.pop(u + 1))
        if units[u][1] == 1:
            kvh = units[u][0]
            lo_half, hi_half = pv.pop(u - 1), pv.pop(u)
            first = lo_half / pltpu.roll(lo_half, ATT_HEAD_DIM, 1)
            second = pltpu.roll(hi_half, ATT_HEAD_DIM, 1) / hi_half
            for half in range(2):
                rows = slice(half * tq, (half + 1) * tq)
                o_ref[:, (2 * kvh + half) * pair:(2 * kvh + half + 1) * pair] = (
                    jnp.where(low, first[rows], second[rows]).astype(o_ref.dtype))


def _attn(q, klo, khi, v1, cache, nb, seq, layer):
    t = nb * seq
    tq = 256 if seq > 256 else min(seq, 128)
    nq = seq // tq
    has_cache = cache is not None
    kv_spec = pl.BlockSpec((seq, ATT_KV_PAD), lambda b, i: (b, 0))
    in_specs = [pl.BlockSpec((tq, D_MODEL), lambda b, i: (b * nq + i, 0)), kv_spec, kv_spec, kv_spec]
    args = [q, klo, khi, v1]
    if has_cache:
        past = cache[0].shape[2]
        spec = pl.BlockSpec((1, 1, past, ATT_KV_PAD), lambda b, i: (b, layer, 0, 0))
        in_specs += [spec, spec, spec]
        args += list(cache)
    return pl.pallas_call(
        functools.partial(_attn_kernel, has_cache=has_cache, tq=tq),
        grid=(nb, nq),
        in_specs=in_specs,
        out_specs=pl.BlockSpec((tq, D_MODEL), lambda b, i: (b * nq + i, 0)),
        out_shape=jax.ShapeDtypeStruct((t, D_MODEL), BF16),
        compiler_params=_cparams("parallel", "parallel"),
        name="gqa",
    )(*args)


def _merge_kernel(yf_ref, yb_ref, z_ref, att_ref, hf_ref, hb_ref, mo_ref,
                  g0_ref, g1_ref, g2_ref, x_ref, gate_ref, sgain_ref, mgain_ref,
                  wb_ref, wo_ref, o_ref, b3_scr):
    p_att = _dot(att_ref[...], wb_ref[1])
    ys = (yf_ref[...].astype(F32) + yb_ref[...].astype(F32)) * z_ref[...].astype(F32)
    b1 = _rms(ys) * sgain_ref[...]
    p_ssd = _dot(b1.astype(BF16), wb_ref[0])
    for hh in range(ML_HEADS):
        sl = slice(hh * ML_HEAD_DIM, (hh + 1) * ML_HEAD_DIM)
        b3_scr[:, sl] = _rms(hf_ref[:, sl].astype(F32) + hb_ref[:, sl].astype(F32))
    b3 = (b3_scr[...] * mgain_ref[...]) * mo_ref[...].astype(F32)
    p_ml = _dot(b3.astype(BF16), wb_ref[2])
    merged = (g0_ref[...].astype(F32) * p_ssd + g1_ref[...].astype(F32) * p_att
              + g2_ref[...].astype(F32) * p_ml)
    o_ref[...] = x_ref[...] + gate_ref[0] * _dot(merged.astype(BF16), wo_ref[...])


def _merge(yf, yb, u, att, hf, hb, x, gate1, sgain, mgain, wb, wo, seq, layer):
    t = x.shape[0]
    tm = min(t, 256)
    tok = lambda i: (i, 0)
    ucol = lambda off: pl.BlockSpec((tm, D_MODEL), lambda i: (i, off // D_MODEL))
    full = pl.BlockSpec((tm, D_MODEL), tok)
    vec = pl.BlockSpec((1, D_MODEL), lambda i: (0, 0))
    once = pl.Buffered(1)
    in_specs = [full, full, ucol(U_SZ), full, full, full, ucol(U_MO),
                ucol(U_G), ucol(U_G + D_MODEL), ucol(U_G + 2 * D_MODEL), full,
                _mod_spec(gate1.shape[0], seq, tm), vec, vec,
                pl.BlockSpec((None, 3, D_MODEL, D_MODEL), lambda i: (layer, 0, 0, 0), pipeline_mode=once),
                pl.BlockSpec((None, D_MODEL, D_MODEL), lambda i: (layer, 0, 0), pipeline_mode=once)]
    return pl.pallas_call(
        _merge_kernel,
        grid=(t // tm,),
        in_specs=in_specs,
        out_specs=full,
        out_shape=jax.ShapeDtypeStruct((t, D_MODEL), F32),
        scratch_shapes=[pltpu.VMEM((tm, D_MODEL), F32)],
        compiler_params=_cparams("parallel"),
        name="branch_merge",
    )(yf, yb, u, att, hf, hb, u, u, u, u, x, gate1, sgain, mgain, wb, wo)


def _ffn_kernel(x_ref, sc_ref, sh_ref, gate_ref, wi_ref, wo_ref, *rest, last):
    x = x_ref[...]
    h = (_rms(x) * (1.0 + sc_ref[0]) + sh_ref[0]).astype(BF16)
    a = _dot(h, wi_ref[:, :FFN_HIDDEN])
    b = _dot(h, wi_ref[:, FFN_HIDDEN:])
    act = (a * _sigmoid(a) * b).astype(BF16)
    y = x + gate_ref[0] * _dot(act, wo_ref[...])
    if last:
        gain_ref, o_ref = rest
        o_ref[...] = _rms(y) * gain_ref[...]
    else:
        nsc_ref, nsh_ref, o_ref, h_ref = rest
        o_ref[...] = y
        h_ref[...] = (_rms(y) * (1.0 + nsc_ref[0]) + nsh_ref[0]).astype(h_ref.dtype)


def _ffn(x, scale, shift, gate, wi, wo, seq, layer, nxt):
    t = x.shape[0]
    tm = min(t, 256)
    nb = scale.shape[0]
    once = pl.Buffered(1)
    full = pl.BlockSpec((tm, D_MODEL), lambda i: (i, 0))
    last = not isinstance(nxt, tuple)
    mod = _mod_spec(nb, seq, tm)
    in_specs = [full, mod, mod, mod,
                pl.BlockSpec((None, D_MODEL, 2 * FFN_HIDDEN), lambda i: (layer, 0, 0), pipeline_mode=once),
                pl.BlockSpec((None, FFN_HIDDEN, D_MODEL), lambda i: (layer, 0, 0), pipeline_mode=once)]
    if last:
        in_specs.append(pl.BlockSpec((1, D_MODEL), lambda i: (0, 0)))
        extra = [nxt]
        out_specs = full
        out_shape = jax.ShapeDtypeStruct((t, D_MODEL), F32)
    else:
        in_specs += [mod, mod]
        extra = list(nxt)
        out_specs = [full, full]
        out_shape = [jax.ShapeDtypeStruct((t, D_MODEL), F32), jax.ShapeDtypeStruct((t, D_MODEL), BF16)]
    return pl.pallas_call(
        functools.partial(_ffn_kernel, last=last),
        grid=(t // tm,),
        in_specs=in_specs,
        out_specs=out_specs,
        out_shape=out_shape,
        compiler_params=_cparams("parallel"),
        name="ffn",
    )(x, scale, shift, gate, wi, wo, *extra)


def _gate_consts():
    idx = np.arange(GATE_W)
    back = ((idx >= G_DT + SSD_HEADS) & (idx < G_LI)) | ((idx >= G_LF + ML_HEADS) & (idx < G_LF + 2 * ML_HEADS))
    r = np.arange(CHUNK)
    tlo = (r[:, None] >= r[None, :]).astype(np.float32)
    tup = (r[:, None] <= r[None, :]).astype(np.float32)
    return (jnp.asarray(back.astype(np.float32)), jnp.asarray(tlo, BF16), jnp.asarray(tup, BF16))


def _scan_consts():
    def expand(first, heads, width, parts):
        e = np.zeros((2, parts * GATE_W, heads * width), np.float32)
        for d in range(2):
            for h in range(heads):
                for part in range(parts):
                    e[d, part * GATE_W + first + d * heads + h, h * width:(h + 1) * width] = 1.0
        return jnp.asarray(e, BF16)

    ssd = (expand(G_DT, SSD_HEADS, SSD_HEAD_DIM, 1), expand(G_DT, SSD_HEADS, CHUNK, 2),
           jnp.asarray(np.eye(SSD_BC, dtype=np.float32), BF16))
    ml = (expand(G_LF, ML_HEADS, ML_HEAD_DIM, 2), jnp.asarray(np.eye(CHUNK, dtype=np.float32), BF16))
    return ssd, ml


def _ssd_state_to_compact(s):
    lead = s.shape[:-3]
    rep = SSD_HEADS // SSD_GROUPS
    s = s.reshape(lead + (SSD_GROUPS, rep, SSD_HEAD_DIM, SSD_STATE))
    nd = len(lead)
    s = jnp.transpose(s, tuple(range(nd)) + (nd, nd + 3, nd + 1, nd + 2))
    return s.reshape(lead + (SSD_GROUPS, SSD_STATE, SSD_GROUP_W))


def _ssd_state_from_compact(s):
    lead = s.shape[:-3]
    rep = SSD_HEADS // SSD_GROUPS
    s = s.reshape(lead + (SSD_GROUPS, SSD_STATE, rep, SSD_HEAD_DIM))
    nd = len(lead)
    s = jnp.transpose(s, tuple(range(nd)) + (nd, nd + 2, nd + 3, nd + 1))
    return s.reshape(lead + (SSD_HEADS, SSD_HEAD_DIM, SSD_STATE))


def _rope_tables(seq):
    pos = np.arange(seq)
    quarter = ATT_HEAD_DIM // 4
    freqs = jnp.asarray(ROPE_THETA, F32) ** (-jnp.arange(quarter, dtype=F32) / quarter)
    ang_r = jnp.asarray(pos // GRID_W, F32)[:, None] * freqs
    ang_c = jnp.asarray(pos % GRID_W, F32)[:, None] * freqs
    cos = jnp.concatenate([jnp.cos(ang_r)] * 2 + [jnp.cos(ang_c)] * 2, axis=-1)
    sin = jnp.concatenate([-jnp.sin(ang_r), jnp.sin(ang_r), -jnp.sin(ang_c), jnp.sin(ang_c)], axis=-1)
    reps = (D_MODEL + ATT_KV) // ATT_HEAD_DIM
    return jnp.tile(cos, (1, reps)), jnp.tile(sin, (1, reps))


def _split_w_in(w_in):
    sizes = (D_MODEL, D_MODEL, SSD_BC, SSD_BC, 2 * SSD_HEADS,
             D_MODEL, ATT_KV, ATT_KV,
             D_MODEL, D_MODEL, D_MODEL, D_MODEL, 4 * ML_HEADS, 3 * D_MODEL)
    offs = np.cumsum((0,) + sizes)
    return [w_in[:, :, offs[i]:offs[i + 1]] for i in range(len(sizes))]


def kernel(x_prompt, x_sample, cache_k, cache_v, state_ssd, state_ml_c, state_ml_n, state_ml_m,
           c, c_ctx, w_ada, b_ada, w_in, ssd_conv_w, ssd_conv_b, ssd_a_log, ssd_dt_bias, ssd_d,
           ssd_norm, att_q_norm, att_k_norm, ml_conv_w, ml_conv_b, ml_gate_bias, ml_norm,
           w_branch, w_out, w_ffn_in, w_ffn_out, final_norm):
    depth = w_in.shape[0]
    nbp, seqp, _ = x_prompt.shape
    nbs, seqs, _ = x_sample.shape
    past = cache_k.shape[2]

    (s_x, s_z, s_b, s_c, s_dt, a_q, a_k, a_v, m_q, m_k, m_v, m_o, m_g, g) = _split_w_in(w_in)
    w_main = jnp.concatenate([m_q, m_k, s_x, s_b, s_c, a_k, a_v, a_q, m_v, s_z, m_o, g], axis=-1).astype(BF16)
    conv_w = jnp.concatenate([ml_conv_w, ssd_conv_w], axis=-1)
    conv_b = jnp.concatenate([ml_conv_b, ssd_conv_b], axis=-1).reshape(depth, 1, U_CONV_W)
    m_g4 = m_g.reshape(depth, D_MODEL, 2, 2, ML_HEADS)
    w_small = jnp.concatenate(
        [s_dt, m_g4[:, :, :, 0].reshape(depth, D_MODEL, 2 * ML_HEADS),
         m_g4[:, :, :, 1].reshape(depth, D_MODEL, 2 * ML_HEADS),
         jnp.zeros((depth, D_MODEL, GATE_W - G_LF - 2 * ML_HEADS), F32)], axis=-1).astype(BF16)
    w_small_t = jnp.swapaxes(w_small, 1, 2)
    pad = jnp.zeros((depth, GATE_W - G_LF - 2 * ML_HEADS), F32)
    gate_bias = jnp.concatenate(
        [ssd_dt_bias.reshape(depth, 2 * SSD_HEADS), ml_gate_bias[:, :, 0].reshape(depth, 2 * ML_HEADS),
         ml_gate_bias[:, :, 1].reshape(depth, 2 * ML_HEADS), pad], axis=-1)
    a_vec = jnp.concatenate(
        [-jnp.exp(ssd_a_log.reshape(depth, 2 * SSD_HEADS)), jnp.zeros((depth, GATE_W - 2 * SSD_HEADS), F32)], axis=-1)
    d_vec = jnp.repeat(ssd_d, SSD_HEAD_DIM, axis=-1).reshape(depth, 1, D_MODEL)
    qk_gain = jnp.concatenate([jnp.tile(att_q_norm, (1, ATT_HEADS)), jnp.tile(att_k_norm, (1, ATT_KV_HEADS))], axis=-1)
    qk_gain = qk_gain.reshape(depth, 1, D_MODEL + ATT_KV)
    w_branch_b = w_branch.astype(BF16)
    w_out_b = w_out.astype(BF16)
    w_ffn_in_b = w_ffn_in.astype(BF16)
    w_ffn_out_b = w_ffn_out.astype(BF16)
    consts = _gate_consts()
    tables = _rope_tables(seqs)
    zeros = jnp.zeros(cache_k.shape, BF16)
    ck, cv = cache_k.astype(BF16), cache_v.astype(BF16)
    pad_shape = (nbs, depth, past, ATT_KV_PAD)
    cache = (jnp.concatenate([ck, zeros], axis=-1).reshape(pad_shape),
             jnp.concatenate([zeros, ck], axis=-1).reshape(pad_shape),
             jnp.concatenate([cv, jnp.ones(cache_v.shape, BF16)], axis=-1).reshape(pad_shape))
    ssd_consts, ml_consts = _scan_consts()
    ssd_h0 = _ssd_state_to_compact(state_ssd)
    ml_cn0 = jnp.concatenate(
        [state_ml_c, jnp.broadcast_to(state_ml_n[..., None], state_ml_c.shape)], axis=-1)
    ml_m0 = jnp.repeat(state_ml_m, ML_HEAD_DIM, axis=-1).reshape(nbs, depth, 2, 1, D_MODEL)

    rows = 8 * ((1 + nbs + 7) // 8)
    cc = jnp.concatenate([c_ctx[None], c, jnp.zeros((rows - 1 - nbs, D_MODEL), F32)], axis=0)
    mod = _ada(cc, w_ada, b_ada)

    def mods(layer, lo, hi):
        m = mod[layer, lo:hi].reshape(hi - lo, 1, 6, D_MODEL)
        return [m[:, :, i] for i in range(6)]

    def layer_step(x, h, layer, nb, seq, mod6, nxt, ctx):
        shift1, scale1, gate1, shift2, scale2, gate2 = mod6
        u, sx = _proj(h, w_main, conv_w, conv_b, layer, seq)
        gates = _gate(h, w_small[layer], w_small_t[layer], gate_bias[layer], a_vec[layer], consts)
        if ctx:
            yf, yb = _ssd(sx, u, gates, ssd_consts, d_vec[layer], nb, seq, ssd_h0, layer)
            q, klo, khi, v1 = _qkprep(u, qk_gain[layer], tables, nb, seq, False)
            att = _attn(q, klo, khi, v1, cache, nb, seq, layer)
            hf, hb = _mlstm(u, gates, ml_consts, nb, seq, (ml_cn0, ml_m0), layer)
            new = None
        else:
            yf, yb, hst = _ssd(sx, u, gates, ssd_consts, d_vec[layer], nb, seq, None, layer)
            q, klo, khi, v1, kf, vf = _qkprep(u, qk_gain[layer], None, nb, seq, True)
            att = _attn(q, klo, khi, v1, None, nb, seq, layer)
            hf, hb, cn, mrow = _mlstm(u, gates, ml_consts, nb, seq, None, layer)
            new = (kf.reshape(nb, seq, ATT_KV_HEADS, ATT_HEAD_DIM), vf.reshape(nb, seq, ATT_KV_HEADS, ATT_HEAD_DIM),
                   _ssd_state_from_compact(hst), cn[..., :ML_HEAD_DIM], cn[..., ML_HEAD_DIM],
                   mrow[:, :, 0, ::ML_HEAD_DIM])
        x = _merge(yf, yb, u, att, hf, hb, x, gate1,
                   ssd_norm[layer].reshape(1, D_MODEL), ml_norm[layer].reshape(1, D_MODEL),
                   w_branch_b, w_out_b, seq, layer)
        return _ffn(x, scale2, shift2, gate2, w_ffn_in_b, w_ffn_out_b, seq, layer, nxt), new

    def run_group(x, nb, seq, lo, hi, ctx):
        mod6 = [mods(layer, lo, hi) for layer in range(depth)]
        h = _norm_mod(x, mod6[0][1], mod6[0][0], seq)
        news = []
        for layer in range(depth):
            if layer + 1 < depth:
                nxt = (mod6[layer + 1][1], mod6[layer + 1][0])
                (x, h), new = layer_step(x, h, layer, nb, seq, mod6[layer], nxt, ctx)
            else:
                x, new = layer_step(x, h, layer, nb, seq, mod6[layer], final_norm.reshape(1, D_MODEL), ctx)
            news.append(new)
        return x, news

    y_p, news = run_group(x_prompt.reshape(nbp * seqp, D_MODEL), nbp, seqp, 0, 1, False)
    y_s, _ = run_group(x_sample.reshape(nbs * seqs, D_MODEL), nbs, seqs, 1, 1 + nbs, True)
    y_prompt = y_p.reshape(x_prompt.shape)
    y_sample = y_s.reshape(x_sample.shape)
    stacked = [jnp.stack([n[i] for n in news], axis=1) for i in range(6)]
    return (y_prompt, y_sample) + tuple(stacked)
```

```python
import functools

import numpy as np
import jax
import jax.numpy as jnp
from jax import lax
from jax.experimental import pallas as pl
from jax.experimental.pallas import tpu as pltpu

F32 = jnp.float32
BF16 = jnp.bfloat16

D_MODEL = 1024
GRID_W = 64
EPS = 1e-6
CONV_W = 4
CHUNK = 128
SSD_HEADS = 16
SSD_HEAD_DIM = 64
SSD_GROUPS = 4
SSD_STATE = 64
SSD_BC = SSD_GROUPS * SSD_STATE
ATT_HEADS = 16
ATT_HEAD_DIM = 64
ATT_KV_HEADS = 4
ATT_KV = ATT_KV_HEADS * ATT_HEAD_DIM
ATT_KV_PAD = ATT_KV_HEADS * 2 * ATT_HEAD_DIM
Q_SCALE = ATT_HEAD_DIM ** -0.5 * 1.4426950408889634
ROPE_THETA = 10000.0
ML_HEADS = 8
ML_HEAD_DIM = 128
FFN_HIDDEN = 2816
GATE_W = 128

U_TILE = 512
U_MQ = 0
U_MK = 1024
U_SX = 2048
U_SBC = 3072
U_CONV_W = 3584
U_AKV = 3584
U_AQ = 4096
U_MV = 5120
U_SZ = 6144
U_MO = 7168
U_G = 8192
U_W = 11264

G_DT = 0
G_LI = 32
G_LF = 48

VMEM_LIMIT = 48 * 1024 * 1024
NT_DIMS = (((1,), (1,)), ((), ()))


def _cparams(*sem):
    return pltpu.CompilerParams(dimension_semantics=sem, vmem_limit_bytes=VMEM_LIMIT)


def _sigmoid(x):
    return 1.0 / (1.0 + jnp.exp(-x))


def _softplus(x):
    return jnp.maximum(x, 0.0) + jnp.log1p(jnp.exp(-jnp.abs(x)))


def _dot(a, b):
    return jnp.dot(a, b, preferred_element_type=F32)


def _dot_nt(a, b):
    return lax.dot_general(a, b, NT_DIMS, preferred_element_type=F32)


def _ada_kernel(c_ref, w_ref, b_ref, o_ref):
    c = c_ref[...]
    s = (c * _sigmoid(c)).astype(BF16)
    o_ref[0] = _dot(s, w_ref[0].astype(BF16)) + b_ref[0]


def _ada(cc, w_ada, b_ada):
    depth, _, n6 = w_ada.shape
    rows = cc.shape[0]
    tn = 512
    return pl.pallas_call(
        _ada_kernel,
        grid=(depth, n6 // tn),
        in_specs=[pl.BlockSpec((rows, D_MODEL), lambda l, j: (0, 0)),
                  pl.BlockSpec((1, D_MODEL, tn), lambda l, j: (l, 0, j)),
                  pl.BlockSpec((1, 1, tn), lambda l, j: (l, 0, j))],
        out_specs=pl.BlockSpec((1, rows, tn), lambda l, j: (l, 0, j)),
        out_shape=jax.ShapeDtypeStruct((depth, rows, n6), F32),
        compiler_params=_cparams("parallel", "parallel"),
        name="ada_mod",
    )(cc, w_ada, b_ada.reshape(depth, 1, n6))


def _rms(x):
    return x * lax.rsqrt(jnp.mean(x * x, axis=-1, keepdims=True) + EPS)


def _norm_mod_kernel(x_ref, sc_ref, sh_ref, o_ref):
    o_ref[...] = (_rms(x_ref[...]) * (1.0 + sc_ref[0]) + sh_ref[0]).astype(o_ref.dtype)


def _mod_spec(nb, seq, tm):
    if nb == 1:
        return pl.BlockSpec((1, 1, D_MODEL), lambda i: (0, 0, 0))
    return pl.BlockSpec((1, 1, D_MODEL), lambda i: ((i * tm) // seq, 0, 0))


def _norm_mod(x, scale, shift, seq):
    t = x.shape[0]
    tm = min(t, 512)
    nb = scale.shape[0]
    return pl.pallas_call(
        _norm_mod_kernel,
        grid=(t // tm,),
        in_specs=[pl.BlockSpec((tm, D_MODEL), lambda i: (i, 0)),
                  _mod_spec(nb, seq, tm), _mod_spec(nb, seq, tm)],
        out_specs=pl.BlockSpec((tm, D_MODEL), lambda i: (i, 0)),
        out_shape=jax.ShapeDtypeStruct((t, D_MODEL), BF16),
        compiler_params=_cparams("parallel"),
        name="norm_mod",
    )(x, scale, shift)


def _sigmoid_tanh(x):
    return 0.5 * jnp.tanh(0.5 * x) + 0.5


def _silu_tanh(x):
    h = 0.5 * x
    return h * jnp.tanh(h) + h


CONV_HALO = 16


def _conv_silu(x, w_ref, b_ref, seq, row0):
    rows = x.shape[0]
    pos = jnp.bitwise_and(lax.broadcasted_iota(jnp.int32, x.shape, 0) + row0, seq - 1)
    acc = x * w_ref[1:2, :] + b_ref[...]
    acc = acc + jnp.where(pos >= 1, pltpu.roll(x, 1, 0), 0.0) * w_ref[0:1, :]
    acc = acc + jnp.where(pos < seq - 1, pltpu.roll(x, rows - 1, 0), 0.0) * w_ref[2:3, :]
    acc = acc + jnp.where(pos < seq - 2, pltpu.roll(x, rows - 2, 0), 0.0) * w_ref[3:4, :]
    return _silu_tanh(acc)


def _proj_kernel(a_ref, w_ref, cw_ref, cb_ref, o_ref, sx_ref, *, conv_hi, sx_lo, sx_hi, silu_lo, sig_lo, sub, seq):
    j = pl.program_id(1)
    w = w_ref[...]

    def run(act):
        for r in range(a_ref.shape[0] // sub):
            rows = slice(r * sub, (r + 1) * sub)
            o_ref[rows, :] = act(_dot(a_ref[rows, :], w)).astype(o_ref.dtype)

    def run_conv(emit_f32):
        tm = a_ref.shape[0]
        for r in range(tm // sub):
            r0 = r * sub
            lo, hi = max(r0 - CONV_HALO, 0), min(r0 + sub + CONV_HALO, tm)
            y = _conv_silu(_dot(a_ref[lo:hi, :], w), cw_ref, cb_ref, seq, lo)[r0 - lo:r0 - lo + sub]
            o_ref[r0:r0 + sub, :] = y.astype(o_ref.dtype)
            if emit_f32:
                sx_ref[r0:r0 + sub, :] = y

    is_sx = jnp.logical_and(j >= sx_lo, j < sx_hi)

    @pl.when(jnp.logical_and(j < conv_hi, jnp.logical_not(is_sx)))
    def _():
        run_conv(False)

    @pl.when(is_sx)
    def _():
        run_conv(True)

    @pl.when(jnp.logical_and(j >= conv_hi, j < silu_lo))
    def _():
        run(lambda acc: acc)

    @pl.when(jnp.logical_and(j >= silu_lo, j < sig_lo))
    def _():
        run(_silu_tanh)

    @pl.when(j >= sig_lo)
    def _():
        run(_sigmoid_tanh)


def _proj(h, w_main, conv_w, conv_b, layer, seq):
    t = h.shape[0]
    tm = min(t, 2048)
    tn = U_TILE
    assert tm % seq == 0 and seq & (seq - 1) == 0
    conv_hi = U_CONV_W // tn
    sx_lo, sx_hi = U_SX // tn, (U_SX + D_MODEL) // tn
    kern = functools.partial(_proj_kernel, conv_hi=conv_hi, sx_lo=sx_lo, sx_hi=sx_hi,
                             silu_lo=U_SZ // tn, sig_lo=U_MO // tn, sub=min(tm, 512), seq=seq)
    conv_col = lambda i, j: (layer, 0, jnp.minimum(j, conv_hi - 1))
    return pl.pallas_call(
        kern,
        grid=(t // tm, U_W // tn),
        in_specs=[pl.BlockSpec((tm, D_MODEL), lambda i, j: (i, 0)),
                  pl.BlockSpec((None, D_MODEL, tn), lambda i, j: (layer, 0, j)),
                  pl.BlockSpec((None, CONV_W, tn), conv_col),
                  pl.BlockSpec((None, 1, tn), conv_col)],
        out_specs=[pl.BlockSpec((tm, tn), lambda i, j: (i, j)),
                   pl.BlockSpec((tm, tn), lambda i, j: (i, jnp.clip(j - sx_lo, 0, sx_hi - sx_lo - 1)))],
        out_shape=[jax.ShapeDtypeStruct((t, U_W), BF16), jax.ShapeDtypeStruct((t, D_MODEL), F32)],
        compiler_params=_cparams("parallel", "arbitrary"),
        name="in_proj",
    )(h, w_main, conv_w, conv_b)


def _split3(x):
    hi = x.astype(BF16)
    r1 = x - hi.astype(F32)
    mid = r1.astype(BF16)
    lo = (r1 - mid.astype(F32)).astype(BF16)
    return hi, mid, lo


def _tri_left(tri, x):
    hi, mid, lo = _split3(x)
    return _dot(tri, hi) + _dot(tri, mid) + _dot(tri, lo)


def _tri_right(x, tri):
    hi, mid, lo = _split3(x)
    return _dot(hi, tri) + _dot(mid, tri) + _dot(lo, tri)


def _gate_vals(u, idx, a):
    sp = _softplus(u)
    ls = -_softplus(-u)
    val = jnp.where(idx < G_LI, sp, jnp.where(idx < G_LF, u, jnp.where(idx < G_LF + 16, ls, 0.0)))
    cin = jnp.where(idx < G_LI, sp * a, jnp.where(jnp.logical_and(idx >= G_LF, idx < G_LF + 16), ls, 0.0))
    return val, cin


def _gate_kernel(h_ref, ws_ref, wst_ref, b_ref, bt_ref, a_ref, at_ref, dm_ref, dmt_ref,
                 tlo_ref, tup_ref, val_ref, cum_ref, valt_ref, cumt_ref):
    h = h_ref[...]
    tlo = tlo_ref[...]
    tup = tup_ref[...]
    rows = h.shape[0]
    u = _dot(h, ws_ref[...]) + b_ref[...]
    val, cin = _gate_vals(u, lax.broadcasted_iota(jnp.int32, (rows, GATE_W), 1), a_ref[...])
    val_ref[...] = val
    ut = _dot_nt(wst_ref[...], h) + bt_ref[...]
    valt, cint = _gate_vals(ut, lax.broadcasted_iota(jnp.int32, (GATE_W, rows), 0), at_ref[...])
    valt_ref[...] = valt
    back = dm_ref[...] > 0.5
    back_t = dmt_ref[...] > 0.5
    for c in range(rows // CHUNK):
        sl = slice(c * CHUNK, (c + 1) * CHUNK)
        cum_ref[sl, :] = jnp.where(back, _tri_left(tup, cin[sl]), _tri_left(tlo, cin[sl]))
        cumt_ref[:, sl] = jnp.where(back_t, _tri_right(cint[:, sl], tlo), _tri_right(cint[:, sl], tup))


def _gate(h, ws, wst, bias, avec, consts):
    t = h.shape[0]
    tm = min(t, 8 * CHUNK)
    dmask, tlo, tup = consts
    row = lambda i: (i, 0)
    col = lambda i: (0, i)
    fix = lambda i: (0, 0)
    tm_out = jax.ShapeDtypeStruct((t, GATE_W), F32)
    fm_out = jax.ShapeDtypeStruct((GATE_W, t), F32)
    return pl.pallas_call(
        _gate_kernel,
        grid=(t // tm,),
        in_specs=[pl.BlockSpec((tm, D_MODEL), row),
                  pl.BlockSpec((D_MODEL, GATE_W), fix),
                  pl.BlockSpec((GATE_W, D_MODEL), fix),
                  pl.BlockSpec((1, GATE_W), fix), pl.BlockSpec((GATE_W, 1), fix),
                  pl.BlockSpec((1, GATE_W), fix), pl.BlockSpec((GATE_W, 1), fix),
                  pl.BlockSpec((1, GATE_W), fix), pl.BlockSpec((GATE_W, 1), fix),
                  pl.BlockSpec((CHUNK, CHUNK), fix), pl.BlockSpec((CHUNK, CHUNK), fix)],
        out_specs=[pl.BlockSpec((tm, GATE_W), row), pl.BlockSpec((tm, GATE_W), row),
                   pl.BlockSpec((GATE_W, tm), col), pl.BlockSpec((GATE_W, tm), col)],
        out_shape=[tm_out, tm_out, fm_out, fm_out],
        compiler_params=_cparams("parallel"),
        name="gate_prep",
    )(h, ws, wst, bias.reshape(1, GATE_W), bias.reshape(GATE_W, 1),
      avec.reshape(1, GATE_W), avec.reshape(GATE_W, 1),
      dmask.reshape(1, GATE_W), dmask.reshape(GATE_W, 1), tlo, tup)


def _tri_mask(fwd):
    row = lax.broadcasted_iota(jnp.int32, (CHUNK, CHUNK), 0)
    col = lax.broadcasted_iota(jnp.int32, (CHUNK, CHUNK), 1)
    return row >= col if fwd else row <= col


def _lane_expand(x, e):
    hi, mid, _ = _split3(x)
    return _dot(jnp.concatenate([hi, mid], axis=1), e)


SSD_GROUP_W = (SSD_HEADS // SSD_GROUPS) * SSD_HEAD_DIM


def _ssd_dir(d, x_ref, bc_ref, val_ref, cum_ref, valt_ref, cumt_ref, e64_ref, e128_ref, eye_ref, skip,
             y_ref, st_scr):
    mask = _tri_mask(d == 0)
    end = CHUNK - 1 if d == 0 else 0
    x = x_ref[...]
    bb = bc_ref[:, :SSD_BC]
    cm = bc_ref[:, SSD_BC:]
    lane_c = lax.broadcasted_iota(jnp.int32, cm.shape, 1)
    lane_x = lax.broadcasted_iota(jnp.int32, (CHUNK, 2 * SSD_HEAD_DIM), 1)
    a128 = _lane_expand(cum_ref[...], e128_ref[d])
    a64 = jnp.concatenate(
        [jnp.where(lane_x < SSD_HEAD_DIM, a128[:, (2 * p) * CHUNK:(2 * p + 1) * CHUNK],
                   a128[:, (2 * p + 1) * CHUNK:(2 * p + 2) * CHUNK]) for p in range(SSD_HEADS // 2)], axis=1)
    dt64 = _dot(val_ref[...].astype(BF16), e64_ref[d])
    dt_r = valt_ref[...]
    ac_r = cumt_ref[...]
    st = st_scr[d]
    inter = _dot(cm, st.astype(BF16))
    ea = jnp.exp(a64)
    cm32 = cm.astype(F32)
    c_stack = jnp.concatenate(
        [jnp.where(jnp.logical_and(lane_c >= g * SSD_STATE, lane_c < (g + 1) * SSD_STATE), cm32, 0.0).astype(BF16)
         for g in range(SSD_GROUPS)], axis=0)
    cbt_all = _dot_nt(c_stack, bb)
    bt = _dot_nt(eye_ref[...], bb).astype(BF16)
    yield
    tot = a64[end:end + 1, :]
    xw = (x * (jnp.exp(tot - a64) * dt64)).astype(BF16)
    decay = jnp.exp(tot)
    for g in range(SSD_GROUPS):
        rs = slice(g * SSD_STATE, (g + 1) * SSD_STATE)
        cs = slice(g * SSD_GROUP_W, (g + 1) * SSD_GROUP_W)
        st_scr[d, rs, cs] = decay[:, cs] * st[rs, cs] + _dot(bt[rs, :], xw[:, cs])
    yield
    n_pairs = SSD_HEADS // 2
    weights, xbds = [], []
    for p in range(n_pairs):
        cbt = cbt_all[(p // 2) * CHUNK:(p // 2 + 1) * CHUNK]
        parts = []
        for hh in (2 * p, 2 * p + 1):
            lane = G_DT + d * SSD_HEADS + hh
            seg = jnp.where(mask, a128[:, hh * CHUNK:(hh + 1) * CHUNK] - ac_r[lane:lane + 1, :], -jnp.inf)
            parts.append((cbt * jnp.exp(seg) * dt_r[lane:lane + 1, :]).astype(BF16))
        weights.append(jnp.concatenate(parts, axis=1))
        xp = x[:, p * 2 * SSD_HEAD_DIM:(p + 1) * 2 * SSD_HEAD_DIM]
        xbds.append(jnp.concatenate([jnp.where(lane_x < SSD_HEAD_DIM, xp, 0.0),
                                     jnp.where(lane_x >= SSD_HEAD_DIM, xp, 0.0)], axis=0).astype(BF16))
    yield
    for p in range(n_pairs):
        cs = slice(p * 2 * SSD_HEAD_DIM, (p + 1) * 2 * SSD_HEAD_DIM)
        y = _dot(weights[p], xbds[p]) + ea[:, cs] * inter[:, cs]
        if skip is not None:
            y = y + skip[:, cs] * x[:, cs]
        y_ref[:, cs] = y.astype(y_ref.dtype)


def _ssd_kernel(*refs, has_h0, emit_state):
    xf_ref, bcf_ref, xb_ref, bcb_ref = refs[0:4]
    gf = refs[4:8]
    gb = refs[8:12]
    consts = refs[12:15]
    dvec_ref = refs[15]
    k = 16
    if has_h0:
        h0_ref = refs[k]
        k += 1
    yf_ref, yb_ref = refs[k:k + 2]
    k += 2
    if emit_state:
        hout_ref = refs[k]
        k += 1
    st_scr = refs[k]
    c = pl.program_id(1)

    @pl.when(c == 0)
    def _():
        st_scr[...] = jnp.zeros(st_scr.shape, F32)
        if has_h0:
            for d in range(2):
                for g in range(SSD_GROUPS):
                    st_scr[d, g * SSD_STATE:(g + 1) * SSD_STATE,
                           g * SSD_GROUP_W:(g + 1) * SSD_GROUP_W] = h0_ref[0, 0, d, g]

    _interleave(_ssd_dir(0, xf_ref, bcf_ref, *gf, *consts, dvec_ref[...], yf_ref, st_scr),
                _ssd_dir(1, xb_ref, bcb_ref, *gb, *consts, None, yb_ref, st_scr))

    if emit_state:
        @pl.when(c == pl.num_programs(1) - 1)
        def _():
            for d in range(2):
                for g in range(SSD_GROUPS):
                    hout_ref[0, d, g] = st_scr[d, g * SSD_STATE:(g + 1) * SSD_STATE,
                                               g * SSD_GROUP_W:(g + 1) * SSD_GROUP_W]


def _chunk_specs(nc, width):
    fwd = pl.BlockSpec((CHUNK, width), lambda b, c: (b * nc + c, 0))
    bwd = pl.BlockSpec((CHUNK, width), lambda b, c: (b * nc + nc - 1 - c, 0))
    return fwd, bwd


def _gate_specs(nc):
    tf, tb = _chunk_specs(nc, GATE_W)
    ff = pl.BlockSpec((GATE_W, CHUNK), lambda b, c: (0, b * nc + c))
    fb = pl.BlockSpec((GATE_W, CHUNK), lambda b, c: (0, b * nc + nc - 1 - c))
    return [tf, tf, ff, ff], [tb, tb, fb, fb]


def _const_spec(arr):
    nd = arr.ndim
    return pl.BlockSpec(arr.shape, lambda b, c: (0,) * nd)


def _ssd(x, u, gates, consts, dvec, nb, seq, state, layer):
    nc = seq // CHUNK
    t = nb * seq
    has_h0 = state is not None
    xf, xb = _chunk_specs(nc, D_MODEL)
    bcol = U_SBC // (2 * SSD_BC)
    bcf = pl.BlockSpec((CHUNK, 2 * SSD_BC), lambda b, c: (b * nc + c, bcol))
    bcb = pl.BlockSpec((CHUNK, 2 * SSD_BC), lambda b, c: (b * nc + nc - 1 - c, bcol))
    gfs, gbs = _gate_specs(nc)
    yf, yb = _chunk_specs(nc, D_MODEL)
    st_shape = (2, SSD_GROUPS, SSD_STATE, SSD_GROUP_W)
    in_specs = [xf, bcf, xb, bcb] + gfs + gbs + [_const_spec(a) for a in consts] + [_const_spec(dvec)]
    args = [x, u, x, u] + list(gates) + list(gates) + list(consts) + [dvec]
    if has_h0:
        in_specs.append(pl.BlockSpec((1, 1) + st_shape, lambda b, c: (b, layer, 0, 0, 0, 0)))
        args.append(state)
    out_specs = [yf, yb]
    out_shape = [jax.ShapeDtypeStruct((t, D_MODEL), BF16)] * 2
    if not has_h0:
        out_specs.append(pl.BlockSpec((1,) + st_shape, lambda b, c: (b, 0, 0, 0, 0)))
        out_shape.append(jax.ShapeDtypeStruct((nb,) + st_shape, F32))
    return pl.pallas_call(
        functools.partial(_ssd_kernel, has_h0=has_h0, emit_state=not has_h0),
        grid=(nb, nc),
        in_specs=in_specs,
        out_specs=out_specs,
        out_shape=out_shape,
        scratch_shapes=[pltpu.VMEM((2, SSD_BC, D_MODEL), F32)],
        compiler_params=_cparams("parallel", "arbitrary"),
        name="ssd_scan",
    )(*args)


def _ml_dir(d, q_ref, k_ref, v_ref, val_ref, cum_ref, valt_ref, cumt_ref, eb_ref, eye_ref,
            y_ref, cn_scr, m_scr):
    mask = _tri_mask(d == 0)
    end = CHUNK - 1 if d == 0 else 0
    heads = range(ML_HEADS)
    sls = [slice(hh * ML_HEAD_DIM, (hh + 1) * ML_HEAD_DIM) for hh in heads]
    kscale = ML_HEAD_DIM ** -0.5
    b128 = _lane_expand(cum_ref[...], eb_ref[d])
    li_r = valt_ref[...]
    bc_r = cumt_ref[...]
    m_row = m_scr[d]
    q = q_ref[...]
    k = k_ref[...]
    v = v_ref[...]
    ones = jnp.ones((CHUNK, ML_HEAD_DIM), BF16)
    eye = eye_ref[...]
    v1 = [jnp.concatenate([v[:, sl], ones], axis=1) for sl in sls]
    cn = [cn_scr[d, hh] for hh in heads]
    kt = [_dot_nt(eye, k[:, sl]) for sl in sls]
    qk = [_dot_nt(q[:, sl], (k[:, sl].astype(F32) * kscale).astype(BF16)) for sl in sls]
    qc = [_dot(q[:, sl], cn[hh].astype(BF16)) for hh, sl in zip(heads, sls)]
    yield
    b_end = b128[end:end + 1, :]
    bm = b128 + m_row
    kwt, dch, m_parts, s, inter, m_t = [], [], [], [], [], []
    for hh, sl in zip(heads, sls):
        b_j = bc_r[G_LF + d * ML_HEADS + hh:G_LF + d * ML_HEADS + hh + 1, :]
        i_j = li_r[G_LI + d * ML_HEADS + hh:G_LI + d * ML_HEADS + hh + 1, :]
        wj = b_end[:, sl] - b_j + i_j
        m_new = jnp.maximum(b_end[:, sl] + m_row[:, sl], jnp.max(wj, axis=1, keepdims=True))
        kwt.append((kt[hh] * (jnp.exp(wj - m_new) * kscale)).astype(BF16))
        dch.append(jnp.exp(b_end[:, sl] + m_row[:, sl] - m_new))
        m_parts.append(m_new)
        dmat = jnp.where(mask, b128[:, sl] - b_j + i_j, -jnp.inf)
        mt = jnp.maximum(bm[:, sl], jnp.max(dmat, axis=1, keepdims=True))
        s.append((qk[hh] * jnp.exp(dmat - mt)).astype(BF16))
        inter.append(jnp.exp(bm[:, sl] - mt))
        m_t.append(mt)
    yield
    upd = [_dot(kwt[hh], v1[hh]) for hh in heads]
    sv = [_dot(s[hh], v1[hh]) for hh in heads]
    yield
    for hh, sl in zip(heads, sls):
        r = sv[hh] + jnp.concatenate([inter[hh], inter[hh]], axis=1) * qc[hh]
        y_ref[:, sl] = (r[:, :ML_HEAD_DIM]
                        / jnp.maximum(jnp.abs(r[:, ML_HEAD_DIM:]), jnp.exp(-m_t[hh]))).astype(y_ref.dtype)
        cn_scr[d, hh] = jnp.concatenate([dch[hh], dch[hh]], axis=1) * cn[hh] + upd[hh]
    m_scr[d] = jnp.concatenate(m_parts, axis=1)


def _interleave(*gens):
    live = list(gens)
    while live:
        still = []
        for g in live:
            try:
                next(g)
                still.append(g)
            except StopIteration:
                pass
        live = still


def _ml_kernel(*refs, has_init, emit_state):
    qf_ref, kf_ref, vf_ref, qb_ref, kb_ref, vb_ref = refs[0:6]
    gf = refs[6:10]
    gb = refs[10:14]
    consts = refs[14:16]
    k = 16
    if has_init:
        cn0_ref, m0_ref = refs[k:k + 2]
        k += 2
    yf_ref, yb_ref = refs[k:k + 2]
    k += 2
    if emit_state:
        cnout_ref, mout_ref = refs[k:k + 2]
        k += 2
    cn_scr, m_scr = refs[k:k + 2]
    c = pl.program_id(1)

    @pl.when(c == 0)
    def _():
        if has_init:
            cn_scr[...] = cn0_ref[0, 0]
            m_scr[...] = m0_ref[0, 0]
        else:
            cn_scr[...] = jnp.zeros(cn_scr.shape, F32)
            m_scr[...] = jnp.zeros(m_scr.shape, F32)

    _interleave(_ml_dir(0, qf_ref, kf_ref, vf_ref, *gf, *consts, yf_ref, cn_scr, m_scr),
                _ml_dir(1, qb_ref, kb_ref, vb_ref, *gb, *consts, yb_ref, cn_scr, m_scr))

    if emit_state:
        @pl.when(c == pl.num_programs(1) - 1)
        def _():
            cnout_ref[0] = cn_scr[...]
            mout_ref[0] = m_scr[...]


def _mlstm(u, gates, consts, nb, seq, states, layer):
    nc = seq // CHUNK
    t = nb * seq
    has_init = states is not None
    mqb, mkb, mvb = U_MQ // D_MODEL, U_MK // D_MODEL, U_MV // D_MODEL

    def tok(col, rev):
        if rev:
            return pl.BlockSpec((CHUNK, D_MODEL), lambda b, c: (b * nc + nc - 1 - c, col))
        return pl.BlockSpec((CHUNK, D_MODEL), lambda b, c: (b * nc + c, col))

    gfs, gbs = _gate_specs(nc)
    in_specs = ([tok(mqb, False), tok(mkb, False), tok(mvb, False),
                 tok(mqb, True), tok(mkb, True), tok(mvb, True)] + gfs + gbs
                + [_const_spec(a) for a in consts])
    args = [u] * 6 + list(gates) + list(gates) + list(consts)
    cn_shape = (2, ML_HEADS, ML_HEAD_DIM, 2 * ML_HEAD_DIM)
    m_shape = (2, 1, D_MODEL)
    if has_init:
        in_specs += [pl.BlockSpec((1, 1) + cn_shape, lambda b, c: (b, layer, 0, 0, 0, 0)),
                     pl.BlockSpec((1, 1) + m_shape, lambda b, c: (b, layer, 0, 0, 0))]
        args += list(states)
    out_specs = [tok(0, False), tok(0, True)]
    out_shape = [jax.ShapeDtypeStruct((t, D_MODEL), BF16)] * 2
    if not has_init:
        out_specs += [pl.BlockSpec((1,) + cn_shape, lambda b, c: (b, 0, 0, 0, 0)),
                      pl.BlockSpec((1,) + m_shape, lambda b, c: (b, 0, 0, 0))]
        out_shape += [jax.ShapeDtypeStruct((nb,) + cn_shape, F32),
                      jax.ShapeDtypeStruct((nb,) + m_shape, F32)]
    return pl.pallas_call(
        functools.partial(_ml_kernel, has_init=has_init, emit_state=not has_init),
        grid=(nb, nc),
        in_specs=in_specs,
        out_specs=out_specs,
        out_shape=out_shape,
        scratch_shapes=[pltpu.VMEM(cn_shape, F32), pltpu.VMEM(m_shape, F32)],
        compiler_params=_cparams("parallel", "arbitrary"),
        name="mlstm_scan",
    )(*args)


def _qkprep_kernel(*refs, rope, emit_f32):
    uq_ref, ukv_ref, gain_ref = refs[0:3]
    k = 3
    if rope:
        cos_ref, sin_ref = refs[k:k + 2]
        k += 2
    q_ref, klo_ref, khi_ref, v1_ref = refs[k:k + 4]
    k += 4
    if emit_f32:
        kf_ref, vf_ref = refs[k:k + 2]
        k += 2
    nqk = D_MODEL + ATT_KV
    pair = 2 * ATT_HEAD_DIM
    low = lax.broadcasted_iota(jnp.int32, (uq_ref.shape[0], pair), 1) < ATT_HEAD_DIM
    cols = ([uq_ref[:, p * pair:(p + 1) * pair] for p in range(D_MODEL // pair)]
            + [ukv_ref[:, p * pair:(p + 1) * pair] for p in range(ATT_KV // pair)])
    parts = []
    for col in cols:
        x = col.astype(F32)
        x2 = x * x
        ss = jnp.where(low, jnp.sum(jnp.where(low, x2, 0.0), axis=-1, keepdims=True),
                       jnp.sum(jnp.where(low, 0.0, x2), axis=-1, keepdims=True))
        parts.append(x * lax.rsqrt(ss * (1.0 / ATT_HEAD_DIM) + EPS))
    xn = jnp.concatenate(parts, axis=1) * gain_ref[...]
    v = ukv_ref[:, ATT_KV:]
    if emit_f32:
        kf_ref[...] = xn[:, D_MODEL:nqk]
        vf_ref[...] = v.astype(F32)
    if rope:
        lane = lax.broadcasted_iota(jnp.int32, xn.shape, 1)
        quarter = ATT_HEAD_DIM // 4
        first = jnp.bitwise_and(lane, 2 * quarter - 1) < quarter
        swapped = jnp.where(first, pltpu.roll(xn, nqk - quarter, 1), pltpu.roll(xn, quarter, 1))
        xn = xn * cos_ref[...] + swapped * sin_ref[...]
    q_ref[...] = (xn[:, :D_MODEL] * Q_SCALE).astype(BF16)

    def spread(a, fill):
        lo_parts, hi_parts = [], []
        for c in (a[:, :pair], a[:, pair:]):
            r = pltpu.roll(c, ATT_HEAD_DIM, 1)
            lo_parts += [jnp.where(low, c, fill), jnp.where(low, r, fill)]
            hi_parts += [jnp.where(low, fill, r), jnp.where(low, fill, c)]
        return jnp.concatenate(lo_parts, axis=1), jnp.concatenate(hi_parts, axis=1)

    k_lo, k_hi = spread(xn[:, D_MODEL:nqk], 0.0)
    klo_ref[...] = k_lo.astype(BF16)
    khi_ref[...] = k_hi.astype(BF16)
    v1_ref[...] = spread(v.astype(F32), 1.0)[0].astype(BF16)


def _qkprep(u, gain, tables, nb, seq, emit_f32):
    t = nb * seq
    tl = min(seq, 512)
    nl = seq // tl
    nqk = D_MODEL + ATT_KV
    rope = tables is not None
    tok = lambda w: pl.BlockSpec((tl, w), lambda i, b: (b * nl + i, 0))
    in_specs = [pl.BlockSpec((tl, D_MODEL), lambda i, b: (b * nl + i, U_AQ // D_MODEL)),
                pl.BlockSpec((tl, 2 * ATT_KV), lambda i, b: (b * nl + i, U_AKV // (2 * ATT_KV))),
                pl.BlockSpec((1, nqk), lambda i, b: (0, 0))]
    args = [u, u, gain]
    if rope:
        in_specs += [pl.BlockSpec((tl, nqk), lambda i, b: (i, 0))] * 2
        args += list(tables)
    out_specs = [tok(D_MODEL), tok(ATT_KV_PAD), tok(ATT_KV_PAD), tok(ATT_KV_PAD)]
    out_shape = [jax.ShapeDtypeStruct((t, D_MODEL), BF16)] + [jax.ShapeDtypeStruct((t, ATT_KV_PAD), BF16)] * 3
    if emit_f32:
        out_specs += [tok(ATT_KV), tok(ATT_KV)]
        out_shape += [jax.ShapeDtypeStruct((t, ATT_KV), F32)] * 2
    return pl.pallas_call(
        functools.partial(_qkprep_kernel, rope=rope, emit_f32=emit_f32),
        grid=(nl, nb),
        in_specs=in_specs,
        out_specs=out_specs,
        out_shape=out_shape,
        compiler_params=_cparams("parallel", "parallel"),
        name="qk_prep",
    )(*args)


def _attn_kernel(*refs, has_cache, tq):
    q_ref, klo_ref, khi_ref, v1_ref = refs[0:4]
    k = 4
    if has_cache:
        kclo_ref, kchi_ref, vc1_ref = refs[k:k + 3]
        k += 3
    o_ref = refs[k]
    pair = 2 * ATT_HEAD_DIM
    low = lax.broadcasted_iota(jnp.int32, (tq, pair), 1) < ATT_HEAD_DIM
    units = [(kvh, which) for kvh in range(ATT_KV_HEADS) for which in (0, 1)]

    def keys_of(kvh):
        ks = slice(kvh * pair, (kvh + 1) * pair)
        keys = [(klo_ref[:, ks], khi_ref[:, ks], v1_ref[:, ks])]
        if has_cache:
            keys.append((kclo_ref[0, 0, :, ks], kchi_ref[0, 0, :, ks], vc1_ref[0, 0, :, ks]))
        return keys

    def score(u):
        kvh, which = units[u]
        qp = jnp.concatenate([q_ref[:, (2 * kvh) * pair:(2 * kvh + 1) * pair],
                              q_ref[:, (2 * kvh + 1) * pair:(2 * kvh + 2) * pair]], axis=0)
        return [_dot_nt(qp, kk[which]) for kk in keys_of(kvh)]

    def weights(scores):
        m = jnp.max(scores[0], axis=-1, keepdims=True)
        for s in scores[1:]:
            m = jnp.maximum(m, jnp.max(s, axis=-1, keepdims=True))
        return [jnp.exp2(s - m).astype(BF16) for s in scores]

    def values(u, probs):
        pv = None
        for p, kk in zip(probs, keys_of(units[u][0])):
            part = _dot(p, kk[2])
            pv = part if pv is None else pv + part
        return pv

    n = len(units)
    sc = {0: score(0), 1: score(1)}
    pr = {0: weights(sc.pop(0))}
    pv = {}
    for u in range(n):
        if u + 2 < n:
            sc[u + 2] = score(u + 2)
        pv[u] = values(u, pr.pop(u))
        if u + 1 < n:
            pr[u + 1] = weights(sc.pop(u + 1))
        if units[u][1] == 1:
            kvh = units[u][0]
            lo_half, hi_half = pv.pop(u - 1), pv.pop(u)
            first = lo_half / pltpu.roll(lo_half, ATT_HEAD_DIM, 1)
            second = pltpu.roll(hi_half, ATT_HEAD_DIM, 1) / hi_half
            for half in range(2):
                rows = slice(half * tq, (half + 1) * tq)
                o_ref[:, (2 * kvh + half) * pair:(2 * kvh + half + 1) * pair] = (
                    jnp.where(low, first[rows], second[rows]).astype(o_ref.dtype))


def _attn(q, klo, khi, v1, cache, nb, seq, layer):
    t = nb * seq
    tq = 256 if seq > 256 else min(seq, 128)
    nq = seq // tq
    has_cache = cache is not None
    kv_spec = pl.BlockSpec((seq, ATT_KV_PAD), lambda b, i: (b, 0))
    in_specs = [pl.BlockSpec((tq, D_MODEL), lambda b, i: (b * nq + i, 0)), kv_spec, kv_spec, kv_spec]
    args = [q, klo, khi, v1]
    if has_cache:
        past = cache[0].shape[2]
        spec = pl.BlockSpec((1, 1, past, ATT_KV_PAD), lambda b, i: (b, layer, 0, 0))
        in_specs += [spec, spec, spec]
        args += list(cache)
    return pl.pallas_call(
        functools.partial(_attn_kernel, has_cache=has_cache, tq=tq),
        grid=(nb, nq),
        in_specs=in_specs,
        out_specs=pl.BlockSpec((tq, D_MODEL), lambda b, i: (b * nq + i, 0)),
        out_shape=jax.ShapeDtypeStruct((t, D_MODEL), BF16),
        compiler_params=_cparams("parallel", "parallel"),
        name="gqa",
    )(*args)


def _merge_kernel(yf_ref, yb_ref, z_ref, att_ref, hf_ref, hb_ref, mo_ref,
                  g0_ref, g1_ref, g2_ref, x_ref, gate_ref, sgain_ref, mgain_ref,
                  wb_ref, wo_ref, o_ref, b3_scr):
    p_att = _dot(att_ref[...], wb_ref[1])
    ys = (yf_ref[...].astype(F32) + yb_ref[...].astype(F32)) * z_ref[...].astype(F32)
    b1 = _rms(ys) * sgain_ref[...]
    p_ssd = _dot(b1.astype(BF16), wb_ref[0])
    for hh in range(ML_HEADS):
        sl = slice(hh * ML_HEAD_DIM, (hh + 1) * ML_HEAD_DIM)
        b3_scr[:, sl] = _rms(hf_ref[:, sl].astype(F32) + hb_ref[:, sl].astype(F32))
    b3 = (b3_scr[...] * mgain_ref[...]) * mo_ref[...].astype(F32)
    p_ml = _dot(b3.astype(BF16), wb_ref[2])
    merged = (g0_ref[...].astype(F32) * p_ssd + g1_ref[...].astype(F32) * p_att
              + g2_ref[...].astype(F32) * p_ml)
    o_ref[...] = x_ref[...] + gate_ref[0] * _dot(merged.astype(BF16), wo_ref[...])


def _merge(yf, yb, u, att, hf, hb, x, gate1, sgain, mgain, wb, wo, seq, layer):
    t = x.shape[0]
    tm = min(t, 256)
    tok = lambda i: (i, 0)
    ucol = lambda off: pl.BlockSpec((tm, D_MODEL), lambda i: (i, off // D_MODEL))
    full = pl.BlockSpec((tm, D_MODEL), tok)
    vec = pl.BlockSpec((1, D_MODEL), lambda i: (0, 0))
    once = pl.Buffered(1)
    in_specs = [full, full, ucol(U_SZ), full, full, full, ucol(U_MO),
                ucol(U_G), ucol(U_G + D_MODEL), ucol(U_G + 2 * D_MODEL), full,
                _mod_spec(gate1.shape[0], seq, tm), vec, vec,
                pl.BlockSpec((None, 3, D_MODEL, D_MODEL), lambda i: (layer, 0, 0, 0), pipeline_mode=once),
                pl.BlockSpec((None, D_MODEL, D_MODEL), lambda i: (layer, 0, 0), pipeline_mode=once)]
    return pl.pallas_call(
        _merge_kernel,
        grid=(t // tm,),
        in_specs=in_specs,
        out_specs=full,
        out_shape=jax.ShapeDtypeStruct((t, D_MODEL), F32),
        scratch_shapes=[pltpu.VMEM((tm, D_MODEL), F32)],
        compiler_params=_cparams("parallel"),
        name="branch_merge",
    )(yf, yb, u, att, hf, hb, u, u, u, u, x, gate1, sgain, mgain, wb, wo)


def _ffn_kernel(x_ref, sc_ref, sh_ref, gate_ref, wi_ref, wo_ref, *rest, last):
    x = x_ref[...]
    h = (_rms(x) * (1.0 + sc_ref[0]) + sh_ref[0]).astype(BF16)
    a = _dot(h, wi_ref[:, :FFN_HIDDEN])
    b = _dot(h, wi_ref[:, FFN_HIDDEN:])
    act = (a * _sigmoid(a) * b).astype(BF16)
    y = x + gate_ref[0] * _dot(act, wo_ref[...])
    if last:
        gain_ref, o_ref = rest
        o_ref[...] = _rms(y) * gain_ref[...]
    else:
        nsc_ref, nsh_ref, o_ref, h_ref = rest
        o_ref[...] = y
        h_ref[...] = (_rms(y) * (1.0 + nsc_ref[0]) + nsh_ref[0]).astype(h_ref.dtype)


def _ffn(x, scale, shift, gate, wi, wo, seq, layer, nxt):
    t = x.shape[0]
    tm = min(t, 256)
    nb = scale.shape[0]
    once = pl.Buffered(1)
    full = pl.BlockSpec((tm, D_MODEL), lambda i: (i, 0))
    last = not isinstance(nxt, tuple)
    mod = _mod_spec(nb, seq, tm)
    in_specs = [full, mod, mod, mod,
                pl.BlockSpec((None, D_MODEL, 2 * FFN_HIDDEN), lambda i: (layer, 0, 0), pipeline_mode=once),
                pl.BlockSpec((None, FFN_HIDDEN, D_MODEL), lambda i: (layer, 0, 0), pipeline_mode=once)]
    if last:
        in_specs.append(pl.BlockSpec((1, D_MODEL), lambda i: (0, 0)))
        extra = [nxt]
        out_specs = full
        out_shape = jax.ShapeDtypeStruct((t, D_MODEL), F32)
    else:
        in_specs += [mod, mod]
        extra = list(nxt)
        out_specs = [full, full]
        out_shape = [jax.ShapeDtypeStruct((t, D_MODEL), F32), jax.ShapeDtypeStruct((t, D_MODEL), BF16)]
    return pl.pallas_call(
        functools.partial(_ffn_kernel, last=last),
        grid=(t // tm,),
        in_specs=in_specs,
        out_specs=out_specs,
        out_shape=out_shape,
        compiler_params=_cparams("parallel"),
        name="ffn",
    )(x, scale, shift, gate, wi, wo, *extra)


def _gate_consts():
    idx = np.arange(GATE_W)
    back = ((idx >= G_DT + SSD_HEADS) & (idx < G_LI)) | ((idx >= G_LF + ML_HEADS) & (idx < G_LF + 2 * ML_HEADS))
    r = np.arange(CHUNK)
    tlo = (r[:, None] >= r[None, :]).astype(np.float32)
    tup = (r[:, None] <= r[None, :]).astype(np.float32)
    return (jnp.asarray(back.astype(np.float32)), jnp.asarray(tlo, BF16), jnp.asarray(tup, BF16))


def _scan_consts():
    def expand(first, heads, width, parts):
        e = np.zeros((2, parts * GATE_W, heads * width), np.float32)
        for d in range(2):
            for h in range(heads):
                for part in range(parts):
                    e[d, part * GATE_W + first + d * heads + h, h * width:(h + 1) * width] = 1.0
        return jnp.asarray(e, BF16)

    ssd = (expand(G_DT, SSD_HEADS, SSD_HEAD_DIM, 1), expand(G_DT, SSD_HEADS, CHUNK, 2),
           jnp.asarray(np.eye(SSD_BC, dtype=np.float32), BF16))
    ml = (expand(G_LF, ML_HEADS, ML_HEAD_DIM, 2), jnp.asarray(np.eye(CHUNK, dtype=np.float32), BF16))
    return ssd, ml


def _ssd_state_to_compact(s):
    lead = s.shape[:-3]
    rep = SSD_HEADS // SSD_GROUPS
    s = s.reshape(lead + (SSD_GROUPS, rep, SSD_HEAD_DIM, SSD_STATE))
    nd = len(lead)
    s = jnp.transpose(s, tuple(range(nd)) + (nd, nd + 3, nd + 1, nd + 2))
    return s.reshape(lead + (SSD_GROUPS, SSD_STATE, SSD_GROUP_W))


def _ssd_state_from_compact(s):
    lead = s.shape[:-3]
    rep = SSD_HEADS // SSD_GROUPS
    s = s.reshape(lead + (SSD_GROUPS, SSD_STATE, rep, SSD_HEAD_DIM))
    nd = len(lead)
    s = jnp.transpose(s, tuple(range(nd)) + (nd, nd + 2, nd + 3, nd + 1))
    return s.reshape(lead + (SSD_HEADS, SSD_HEAD_DIM, SSD_STATE))


def _rope_tables(seq):
    pos = np.arange(seq)
    quarter = ATT_HEAD_DIM // 4
    freqs = jnp.asarray(ROPE_THETA, F32) ** (-jnp.arange(quarter, dtype=F32) / quarter)
    ang_r = jnp.asarray(pos // GRID_W, F32)[:, None] * freqs
    ang_c = jnp.asarray(pos % GRID_W, F32)[:, None] * freqs
    cos = jnp.concatenate([jnp.cos(ang_r)] * 2 + [jnp.cos(ang_c)] * 2, axis=-1)
    sin = jnp.concatenate([-jnp.sin(ang_r), jnp.sin(ang_r), -jnp.sin(ang_c), jnp.sin(ang_c)], axis=-1)
    reps = (D_MODEL + ATT_KV) // ATT_HEAD_DIM
    return jnp.tile(cos, (1, reps)), jnp.tile(sin, (1, reps))


def _split_w_in(w_in):
    sizes = (D_MODEL, D_MODEL, SSD_BC, SSD_BC, 2 * SSD_HEADS,
             D_MODEL, ATT_KV, ATT_KV,
             D_MODEL, D_MODEL, D_MODEL, D_MODEL, 4 * ML_HEADS, 3 * D_MODEL)
    offs = np.cumsum((0,) + sizes)
    return [w_in[:, :, offs[i]:offs[i + 1]] for i in range(len(sizes))]


def kernel(x_prompt, x_sample, cache_k, cache_v, state_ssd, state_ml_c, state_ml_n, state_ml_m,
           c, c_ctx, w_ada, b_ada, w_in, ssd_conv_w, ssd_conv_b, ssd_a_log, ssd_dt_bias, ssd_d,
           ssd_norm, att_q_norm, att_k_norm, ml_conv_w, ml_conv_b, ml_gate_bias, ml_norm,
           w_branch, w_out, w_ffn_in, w_ffn_out, final_norm):
    depth = w_in.shape[0]
    nbp, seqp, _ = x_prompt.shape
    nbs, seqs, _ = x_sample.shape
    past = cache_k.shape[2]

    (s_x, s_z, s_b, s_c, s_dt, a_q, a_k, a_v, m_q, m_k, m_v, m_o, m_g, g) = _split_w_in(w_in)
    w_main = jnp.concatenate([m_q, m_k, s_x, s_b, s_c, a_k, a_v, a_q, m_v, s_z, m_o, g], axis=-1).astype(BF16)
    conv_w = jnp.concatenate([ml_conv_w, ssd_conv_w], axis=-1)
    conv_b = jnp.concatenate([ml_conv_b, ssd_conv_b], axis=-1).reshape(depth, 1, U_CONV_W)
    m_g4 = m_g.reshape(depth, D_MODEL, 2, 2, ML_HEADS)
    w_small = jnp.concatenate(
        [s_dt, m_g4[:, :, :, 0].reshape(depth, D_MODEL, 2 * ML_HEADS),
         m_g4[:, :, :, 1].reshape(depth, D_MODEL, 2 * ML_HEADS),
         jnp.zeros((depth, D_MODEL, GATE_W - G_LF - 2 * ML_HEADS), F32)], axis=-1).astype(BF16)
    w_small_t = jnp.swapaxes(w_small, 1, 2)
    pad = jnp.zeros((depth, GATE_W - G_LF - 2 * ML_HEADS), F32)
    gate_bias = jnp.concatenate(
        [ssd_dt_bias.reshape(depth, 2 * SSD_HEADS), ml_gate_bias[:, :, 0].reshape(depth, 2 * ML_HEADS),
         ml_gate_bias[:, :, 1].reshape(depth, 2 * ML_HEADS), pad], axis=-1)
    a_vec = jnp.concatenate(
        [-jnp.exp(ssd_a_log.reshape(depth, 2 * SSD_HEADS)), jnp.zeros((depth, GATE_W - 2 * SSD_HEADS), F32)], axis=-1)
    d_vec = jnp.repeat(ssd_d, SSD_HEAD_DIM, axis=-1).reshape(depth, 1, D_MODEL)
    qk_gain = jnp.concatenate([jnp.tile(att_q_norm, (1, ATT_HEADS)), jnp.tile(att_k_norm, (1, ATT_KV_HEADS))], axis=-1)
    qk_gain = qk_gain.reshape(depth, 1, D_MODEL + ATT_KV)
    w_branch_b = w_branch.astype(BF16)
    w_out_b = w_out.astype(BF16)
    w_ffn_in_b = w_ffn_in.astype(BF16)
    w_ffn_out_b = w_ffn_out.astype(BF16)
    consts = _gate_consts()
    tables = _rope_tables(seqs)
    zeros = jnp.zeros(cache_k.shape, BF16)
    ck, cv = cache_k.astype(BF16), cache_v.astype(BF16)
    pad_shape = (nbs, depth, past, ATT_KV_PAD)
    cache = (jnp.concatenate([ck, zeros], axis=-1).reshape(pad_shape),
             jnp.concatenate([zeros, ck], axis=-1).reshape(pad_shape),
             jnp.concatenate([cv, jnp.ones(cache_v.shape, BF16)], axis=-1).reshape(pad_shape))
    ssd_consts, ml_consts = _scan_consts()
    ssd_h0 = _ssd_state_to_compact(state_ssd)
    ml_cn0 = jnp.concatenate(
        [state_ml_c, jnp.broadcast_to(state_ml_n[..., None], state_ml_c.shape)], axis=-1)
    ml_m0 = jnp.repeat(state_ml_m, ML_HEAD_DIM, axis=-1).reshape(nbs, depth, 2, 1, D_MODEL)

    rows = 8 * ((1 + nbs + 7) // 8)
    cc = jnp.concatenate([c_ctx[None], c, jnp.zeros((rows - 1 - nbs, D_MODEL), F32)], axis=0)
    mod = _ada(cc, w_ada, b_ada)

    def mods(layer, lo, hi):
        m = mod[layer, lo:hi].reshape(hi - lo, 1, 6, D_MODEL)
        return [m[:, :, i] for i in range(6)]

    def layer_step(x, h, layer, nb, seq, mod6, nxt, ctx):
        shift1, scale1, gate1, shift2, scale2, gate2 = mod6
        u, sx = _proj(h, w_main, conv_w, conv_b, layer, seq)
        gates = _gate(h, w_small[layer], w_small_t[layer], gate_bias[layer], a_vec[layer], consts)
        if ctx:
            yf, yb = _ssd(sx, u, gates, ssd_consts, d_vec[layer], nb, seq, ssd_h0, layer)
            q, klo, khi, v1 = _qkprep(u, qk_gain[layer], tables, nb, seq, False)
            att = _attn(q, klo, khi, v1, cache, nb, seq, layer)
            hf, hb = _mlstm(u, gates, ml_consts, nb, seq, (ml_cn0, ml_m0), layer)
            new = None
        else:
            yf, yb, hst = _ssd(sx, u, gates, ssd_consts, d_vec[layer], nb, seq, None, layer)
            q, klo, khi, v1, kf, vf = _qkprep(u, qk_gain[layer], None, nb, seq, True)
            att = _attn(q, klo, khi, v1, None, nb, seq, layer)
            hf, hb, cn, mrow = _mlstm(u, gates, ml_consts, nb, seq, None, layer)
            new = (kf.reshape(nb, seq, ATT_KV_HEADS, ATT_HEAD_DIM), vf.reshape(nb, seq, ATT_KV_HEADS, ATT_HEAD_DIM),
                   _ssd_state_from_compact(hst), cn[..., :ML_HEAD_DIM], cn[..., ML_HEAD_DIM],
                   mrow[:, :, 0, ::ML_HEAD_DIM])
        x = _merge(yf, yb, u, att, hf, hb, x, gate1,
                   ssd_norm[layer].reshape(1, D_MODEL), ml_norm[layer].reshape(1, D_MODEL),
                   w_branch_b, w_out_b, seq, layer)
        return _ffn(x, scale2, shift2, gate2, w_ffn_in_b, w_ffn_out_b, seq, layer, nxt), new

    def run_group(x, nb, seq, lo, hi, ctx):
        mod6 = [mods(layer, lo, hi) for layer in range(depth)]
        h = _norm_mod(x, mod6[0][1], mod6[0][0], seq)
        news = []
        for layer in range(depth):
            if layer + 1 < depth:
                nxt = (mod6[layer + 1][1], mod6[layer + 1][0])
                (x, h), new = layer_step(x, h, layer, nb, seq, mod6[layer], nxt, ctx)
            else:
                x, new = layer_step(x, h, layer, nb, seq, mod6[layer], final_norm.reshape(1, D_MODEL), ctx)
            news.append(new)
        return x, news

    y_p, news = run_group(x_prompt.reshape(nbp * seqp, D_MODEL), nbp, seqp, 0, 1, False)
    y_s, _ = run_group(x_sample.reshape(nbs * seqs, D_MODEL), nbs, seqs, 1, 1 + nbs, True)
    y_prompt = y_p.reshape(x_prompt.shape)
    y_sample = y_s.reshape(x_sample.shape)
    stacked = [jnp.stack([n[i] for n in news], axis=1) for i in range(6)]
    return (y_prompt, y_sample) + tuple(stacked)
```
